```python
import jax, jax.numpy as jnp
from jax import lax
import numpy as np

D_MODEL = 1024
BATCH = 8
SEQ = 2048
DEPTH = 4

GRID_W = 64
CTX_LEN = 256
HEAD_DIM = 64
ROPE_THETA = 10000.0
EPS = 1e-6
Q_BLOCK = 128
NEG_INF = -1e30
ATTN_SCALE = HEAD_DIM ** -0.5

A_HEADS = 8
A_KV_HEADS = 2
B_HEADS = 8
B_KV_HEADS = 2
B_WINDOW = 128
C_WIDTH = 512
C_CONV = 31
D_HEADS = 8
NA_KH = 8
NA_KW = 16

A_Q = A_HEADS * HEAD_DIM
A_KV = A_KV_HEADS * HEAD_DIM
B_Q = B_HEADS * HEAD_DIM
B_KV = B_KV_HEADS * HEAD_DIM
D_W = D_HEADS * HEAD_DIM
EVEN_WIDTHS = (A_Q, A_KV, A_KV, A_Q, B_Q, B_KV, B_KV, B_Q)
ODD_WIDTHS = (C_WIDTH, C_WIDTH, C_WIDTH, D_W, D_W, D_W, D_W)
EVEN_IN = sum(EVEN_WIDTHS)
ODD_IN = sum(ODD_WIDTHS)
EVEN_MIX = A_Q + B_Q
ODD_MIX = C_WIDTH + D_W

kernel_name = "hybrid_dit_ctx_prefix_attn_conv_natten"


def rms_norm(x, g):
    xf = x.astype(jnp.float32)
    y = xf * lax.rsqrt(jnp.mean(xf * xf, axis=-1, keepdims=True) + EPS)
    return (y * g.astype(jnp.float32)).astype(x.dtype)


def layer_norm(x, g, b):
    xf = x.astype(jnp.float32)
    mu = jnp.mean(xf, axis=-1, keepdims=True)
    var = jnp.mean(jnp.square(xf - mu), axis=-1, keepdims=True)
    y = (xf - mu) * lax.rsqrt(var + EPS)
    return (y * g.astype(jnp.float32) + b.astype(jnp.float32)).astype(x.dtype)


def split_cols(h, widths):
    offs = [int(o) for o in np.cumsum(widths)[:-1]]
    return jnp.split(h, offs, axis=-1)


def heads(t, n_heads):
    return t.reshape(t.shape[:-1] + (n_heads, HEAD_DIM))


def group(q, kv_heads):
    return q.reshape(q.shape[:-2] + (kv_heads, q.shape[-2] // kv_heads, HEAD_DIM))


def axial_rope_tables(n):
    t = jnp.arange(n)
    row = (t // GRID_W).astype(jnp.float32)
    col = (t % GRID_W).astype(jnp.float32)
    half = HEAD_DIM // 2
    freqs = ROPE_THETA ** (-jnp.arange(0, half, 2, dtype=jnp.float32) / half)
    ang = jnp.concatenate([row[:, None] * freqs, col[:, None] * freqs], axis=-1)
    return jnp.cos(ang), jnp.sin(ang)


def apply_rope(x, cos, sin):
    xf = x.astype(jnp.float32)
    x1, x2 = xf[..., 0::2], xf[..., 1::2]
    c, s = cos[None, :, None, :], sin[None, :, None, :]
    out = jnp.stack([x1 * c - x2 * s, x1 * s + x2 * c], axis=-1).reshape(x.shape)
    return out.astype(x.dtype)


def gqa_attend(q, k, v, bias=None, sink=None):
    s = jnp.einsum("bqhgd,bkhd->bhgqk", q, k, preferred_element_type=jnp.float32) * ATTN_SCALE
    if bias is not None:
        s = s + bias
    if sink is not None:
        sink_col = jnp.broadcast_to(sink.astype(jnp.float32)[None, :, :, None, None], s.shape[:-1] + (1,))
        p = jax.nn.softmax(jnp.concatenate([s, sink_col], axis=-1), axis=-1)[..., :-1]
    else:
        p = jax.nn.softmax(s, axis=-1)
    return jnp.einsum("bhgqk,bkhd->bqhgd", p.astype(v.dtype), v)


def even_mixer(h_lat, h_ctx, w_in, w_out, q_gain, k_gain, sink, cos, sin, update_ctx):
    bsz, n, _ = h_lat.shape
    ctx_len = h_ctx.shape[1]
    n_blk = n // Q_BLOCK
    ga, gb = A_HEADS // A_KV_HEADS, B_HEADS // B_KV_HEADS
    sink = sink.reshape(B_KV_HEADS, gb)

    def project(h):
        aq, ak, av, ag, bq, bk, bv, bg = split_cols(h @ w_in, EVEN_WIDTHS)
        aq = rms_norm(heads(aq, A_HEADS), q_gain)
        ak = rms_norm(heads(ak, A_KV_HEADS), k_gain)
        return (aq, ak, heads(av, A_KV_HEADS), ag,
                heads(bq, B_HEADS), heads(bk, B_KV_HEADS), heads(bv, B_KV_HEADS), bg)

    aq, ak, av, ag, bq, bk, bv, bg = project(h_lat)
    c_aq, c_ak, c_av, c_ag, c_bq, c_bk, c_bv, c_bg = project(h_ctx)
    aq, ak, bq, bk = (apply_rope(t, cos, sin) for t in (aq, ak, bq, bk))

    ka = jnp.concatenate([c_ak, ak], axis=1)
    va = jnp.concatenate([c_av, av], axis=1)
    qa_blk = jnp.moveaxis(group(aq, A_KV_HEADS).reshape(bsz, n_blk, Q_BLOCK, A_KV_HEADS, ga, HEAD_DIM), 1, 0)
    oa = lax.map(lambda qb: gqa_attend(qb, ka, va), qa_blk)
    oa = jnp.moveaxis(oa, 0, 1).reshape(bsz, n, A_Q)

    band = Q_BLOCK + 2 * B_WINDOW
    kb_pad = jnp.pad(bk, ((0, 0), (B_WINDOW, B_WINDOW), (0, 0), (0, 0)))
    vb_pad = jnp.pad(bv, ((0, 0), (B_WINDOW, B_WINDOW), (0, 0), (0, 0)))
    rel = jnp.arange(band)[None, :] - B_WINDOW - jnp.arange(Q_BLOCK)[:, None]
    in_band = jnp.abs(rel) <= B_WINDOW
    ctx_open = jnp.zeros((Q_BLOCK, ctx_len), jnp.float32)
    qb_blk = jnp.moveaxis(group(bq, B_KV_HEADS).reshape(bsz, n_blk, Q_BLOCK, B_KV_HEADS, gb, HEAD_DIM), 1, 0)

    def b_block(args):
        blk, qb = args
        start = blk * Q_BLOCK
        kpos = start - B_WINDOW + jnp.arange(band)
        valid = in_band & ((kpos >= 0) & (kpos < n))[None, :]
        bias = jnp.concatenate([ctx_open, jnp.where(valid, 0.0, NEG_INF)], axis=-1)
        kk = jnp.concatenate([c_bk, lax.dynamic_slice_in_dim(kb_pad, start, band, axis=1)], axis=1)
        vv = jnp.concatenate([c_bv, lax.dynamic_slice_in_dim(vb_pad, start, band, axis=1)], axis=1)
        return gqa_attend(qb, kk, vv, bias, sink)

    ob = lax.map(b_block, (jnp.arange(n_blk), qb_blk))
    ob = jnp.moveaxis(ob, 0, 1).reshape(bsz, n, B_Q)

    y_lat = jnp.concatenate([oa * jax.nn.silu(ag), ob * jax.nn.silu(bg)], axis=-1) @ w_out
    if not update_ctx:
        return y_lat, None
    oa_c = gqa_attend(group(c_aq, A_KV_HEADS), c_ak, c_av).reshape(bsz, ctx_len, A_Q)
    ob_c = gqa_attend(group(c_bq, B_KV_HEADS), c_bk, c_bv, sink=sink).reshape(bsz, ctx_len, B_Q)
    y_ctx = jnp.concatenate([oa_c * jax.nn.silu(c_ag), ob_c * jax.nn.silu(c_bg)], axis=-1) @ w_out
    return y_lat, y_ctx


def conformer_conv(val, glu_gate, silu_gate, dw_w, dw_b, ln_g, ln_b):
    u = val * jax.nn.sigmoid(glu_gate)
    u = lax.conv_general_dilated(u, dw_w.reshape(C_CONV, 1, C_WIDTH), window_strides=(1,),
                                 padding=[(C_CONV // 2, C_CONV // 2)],
                                 dimension_numbers=("NWC", "WIO", "NWC"),
                                 feature_group_count=C_WIDTH) + dw_b
    u = jax.nn.silu(layer_norm(u, ln_g, ln_b))
    return u * jax.nn.silu(silu_gate)


def odd_mixer(h_lat, h_ctx, w_in, w_out, dw_w, dw_b, ln_g, ln_b, rpb, update_ctx):
    bsz, n, _ = h_lat.shape
    ctx_len = h_ctx.shape[1]
    rows = n // GRID_W
    kh = min(NA_KH, rows)

    cv, cgl, cg, dq, dk, dv, dg = split_cols(h_lat @ w_in, ODD_WIDTHS)
    x_cv, x_cgl, x_cg, x_dq, x_dk, x_dv, x_dg = split_cols(h_ctx @ w_in, ODD_WIDTHS)
    c_dk, c_dv = heads(x_dk, D_HEADS), heads(x_dv, D_HEADS)

    oc = conformer_conv(cv, cgl, cg, dw_w, dw_b, ln_g, ln_b)

    q_rows = jnp.moveaxis(heads(dq, D_HEADS).reshape(bsz, rows, GRID_W, D_HEADS, HEAD_DIM), 1, 0)
    k_grid = heads(dk, D_HEADS).reshape(bsz, rows, GRID_W, D_HEADS, HEAD_DIM)
    v_grid = heads(dv, D_HEADS).reshape(bsz, rows, GRID_W, D_HEADS, HEAD_DIM)
    cols = jnp.arange(GRID_W)
    col_start = jnp.clip(cols - NA_KW // 2, 0, GRID_W - NA_KW)
    col_in = (cols[None, :] >= col_start[:, None]) & (cols[None, :] < col_start[:, None] + NA_KW)
    col_mask = jnp.where(col_in, 0.0, NEG_INF)
    dc_idx = jnp.clip(cols[None, :] - cols[:, None] + NA_KW - 1, 0, 2 * NA_KW - 2)
    rpb = rpb.astype(jnp.float32)
    ctx_open = jnp.zeros((D_HEADS, GRID_W, ctx_len), jnp.float32)

    def d_row(args):
        r, qr = args
        rs = jnp.clip(r - kh // 2, 0, rows - kh)
        kr = lax.dynamic_slice_in_dim(k_grid, rs, kh, axis=1).reshape(bsz, kh * GRID_W, D_HEADS, HEAD_DIM)
        vr = lax.dynamic_slice_in_dim(v_grid, rs, kh, axis=1).reshape(bsz, kh * GRID_W, D_HEADS, HEAD_DIM)
        dr_idx = rs + jnp.arange(kh) - r + NA_KH - 1
        bias = rpb[:, dr_idx][:, :, dc_idx] + col_mask[None, None]
        bias = jnp.transpose(bias, (0, 2, 1, 3)).reshape(D_HEADS, GRID_W, kh * GRID_W)
        bias = jnp.concatenate([ctx_open, bias], axis=-1)[None, :, None]
        kk = jnp.concatenate([c_dk, kr], axis=1)
        vv = jnp.concatenate([c_dv, vr], axis=1)
        return gqa_attend(qr[:, :, :, None, :], kk, vv, bias).reshape(bsz, GRID_W, D_W)

    od = lax.map(d_row, (jnp.arange(rows), q_rows))
    od = jnp.moveaxis(od, 0, 1).reshape(bsz, n, D_W)

    y_lat = jnp.concatenate([oc, od * jax.nn.silu(dg)], axis=-1) @ w_out
    if not update_ctx:
        return y_lat, None
    oc_c = conformer_conv(x_cv, x_cgl, x_cg, dw_w, dw_b, ln_g, ln_b)
    od_c = gqa_attend(heads(x_dq, D_HEADS)[:, :, :, None, :], c_dk, c_dv).reshape(bsz, ctx_len, D_W)
    y_ctx = jnp.concatenate([oc_c, od_c * jax.nn.silu(x_dg)], axis=-1) @ w_out
    return y_lat, y_ctx


def modulation(cond, w, b):
    m = jax.nn.silu(cond) @ w + b
    return jnp.split(m, 3, axis=-1)


def setup_inputs(seed: int = 0) -> dict:
    key = jax.random.key(seed)
    ks = jax.random.split(key, 24)
    n_even = (DEPTH + 1) // 2
    n_odd = DEPTH // 2
    f32 = jnp.float32

    def nrm(k, shape, scale):
        return jax.random.normal(k, shape, f32) * scale

    return {
        "x": nrm(ks[0], (BATCH, SEQ, D_MODEL), 1.0),
        "c": nrm(ks[1], (BATCH, D_MODEL), 1.0),
        "ctx": nrm(ks[2], (BATCH, CTX_LEN, D_MODEL), 1.0),
        "c_ctx": nrm(ks[3], (D_MODEL,), 1.0),
        "mod_w": nrm(ks[4], (DEPTH, D_MODEL, 3 * D_MODEL), 0.5 * D_MODEL ** -0.5),
        "mod_b": nrm(ks[5], (DEPTH, 3 * D_MODEL), 0.02),
        "norm_g": 1.0 + nrm(ks[6], (DEPTH, D_MODEL), 0.05),
        "ev_w_in": nrm(ks[7], (n_even, D_MODEL, EVEN_IN), D_MODEL ** -0.5),
        "ev_w_out": nrm(ks[8], (n_even, EVEN_MIX, D_MODEL), EVEN_MIX ** -0.5),
        "a_q_gain": 1.0 + nrm(ks[9], (n_even, HEAD_DIM), 0.05),
        "a_k_gain": 1.0 + nrm(ks[10], (n_even, HEAD_DIM), 0.05),
        "b_sink": nrm(ks[11], (n_even, B_HEADS), 0.5),
        "od_w_in": nrm(ks[12], (n_odd, D_MODEL, ODD_IN), D_MODEL ** -0.5),
        "od_w_out": nrm(ks[13], (n_odd, ODD_MIX, D_MODEL), ODD_MIX ** -0.5),
        "c_dw_w": nrm(ks[14], (n_odd, C_CONV, C_WIDTH), C_CONV ** -0.5),
        "c_dw_b": nrm(ks[15], (n_odd, C_WIDTH), 0.02),
        "c_ln_g": 1.0 + nrm(ks[16], (n_odd, C_WIDTH), 0.05),
        "c_ln_b": nrm(ks[17], (n_odd, C_WIDTH), 0.02),
        "d_rpb": nrm(ks[18], (n_odd, D_HEADS, 2 * NA_KH - 1, 2 * NA_KW - 1), 0.1),
        "final_g": 1.0 + nrm(ks[19], (D_MODEL,), 0.05),
    }


def reference(x, c, ctx, c_ctx, mod_w, mod_b, norm_g, ev_w_in, ev_w_out, a_q_gain, a_k_gain, b_sink,
              od_w_in, od_w_out, c_dw_w, c_dw_b, c_ln_g, c_ln_b, d_rpb, final_g):
    n = x.shape[1]
    cos, sin = axial_rope_tables(n)
    x_lat, x_ctx = x, ctx
    for i in range(DEPTH):
        update_ctx = i < DEPTH - 1
        sh_l, sc_l, g_l = modulation(c, mod_w[i], mod_b[i])
        sh_c, sc_c, g_c = modulation(c_ctx, mod_w[i], mod_b[i])
        h_lat = rms_norm(x_lat, norm_g[i]) * (1.0 + sc_l[:, None, :]) + sh_l[:, None, :]
        h_ctx = rms_norm(x_ctx, norm_g[i]) * (1.0 + sc_c) + sh_c
        j = i // 2
        if i % 2 == 0:
            y_lat, y_ctx = even_mixer(h_lat, h_ctx, ev_w_in[j], ev_w_out[j], a_q_gain[j], a_k_gain[j],
                                      b_sink[j], cos, sin, update_ctx)
        else:
            y_lat, y_ctx = odd_mixer(h_lat, h_ctx, od_w_in[j], od_w_out[j], c_dw_w[j], c_dw_b[j],
                                     c_ln_g[j], c_ln_b[j], d_rpb[j], update_ctx)
        x_lat = x_lat + g_l[:, None, :] * y_lat
        if update_ctx:
            x_ctx = x_ctx + g_c * y_ctx
    return rms_norm(x_lat, final_g)
```

```python
import functools

import numpy as np
import jax
import jax.numpy as jnp
from jax import lax
from jax.experimental import pallas as pl
from jax.experimental.pallas import tpu as pltpu

D_MODEL = 1024
BATCH = 8
SEQ = 2048
DEPTH = 4
GRID_W = 64
CTX_LEN = 256
HEAD_DIM = 64
ROPE_THETA = 10000.0
EPS = 1e-6
NEG_INF = -1e30
ATTN_SCALE = HEAD_DIM ** -0.5
A_HEADS = 8
A_KV_HEADS = 2
B_HEADS = 8
B_KV_HEADS = 2
B_WINDOW = 128
C_WIDTH = 512
C_CONV = 31
D_HEADS = 8
NA_KH = 8
NA_KW = 16
ROWS = SEQ // GRID_W

T_ALL = CTX_LEN + SEQ
ROW_TILE = 256
N_TILES = T_ALL // ROW_TILE
LANES = 128
MXU_N = 256
MOD_ROWS = 16
CTX_MOD_ROW = BATCH
VMEM_LIMIT = 48 * 1024 * 1024

EVEN_IN = 2560
ODD_IN = 3584
HALO = 16
BAND_B = 512
BAND_D = 768

F32 = jnp.float32
BF16 = jnp.bfloat16
NT_DIMS = (((1,), (1,)), ((), ()))


def _cparams(n_grid):
    return pltpu.CompilerParams(dimension_semantics=("arbitrary",) * n_grid,
                                vmem_limit_bytes=VMEM_LIMIT)


def _silu(v):
    return v * jax.nn.sigmoid(v)


def _mod_kernel(cond_ref, w_ref, b_ref, o_ref):
    cnd = cond_ref[...]
    act = _silu(cnd).astype(BF16)
    o_ref[...] = jnp.dot(act, w_ref[...].astype(BF16), preferred_element_type=F32) + b_ref[...]


def _modulation_all(cond, mod_w, mod_b):
    n_col = 3
    return pl.pallas_call(
        _mod_kernel,
        grid=(DEPTH, n_col),
        in_specs=[
            pl.BlockSpec((MOD_ROWS, D_MODEL), lambda i, j: (0, 0)),
            pl.BlockSpec((None, D_MODEL, D_MODEL), lambda i, j: (i, 0, j)),
            pl.BlockSpec((None, 1, D_MODEL), lambda i, j: (i, 0, j)),
        ],
        out_specs=pl.BlockSpec((None, MOD_ROWS, D_MODEL), lambda i, j: (i, 0, j)),
        out_shape=jax.ShapeDtypeStruct((DEPTH, MOD_ROWS, 3 * D_MODEL), F32),
        compiler_params=_cparams(2),
    )(cond, mod_w, mod_b.reshape(DEPTH, 1, 3 * D_MODEL))


def _mod_spec(layer):
    return pl.BlockSpec((None, 1, 3 * D_MODEL),
                        lambda b, t: (layer * MOD_ROWS + jnp.where(t == 0, CTX_MOD_ROW, b), 0, 0))


def _normed_input(x_ref, mod_ref, g_ref):
    x = x_ref[...]
    ms = jnp.mean(x * x, axis=-1, keepdims=True)
    y = x * lax.rsqrt(ms + EPS) * g_ref[...]
    shift = mod_ref[:, 0:D_MODEL]
    scale = mod_ref[:, D_MODEL:2 * D_MODEL]
    return (y * (1.0 + scale) + shift).astype(BF16)


def _even_pieces():
    pieces = []
    for j in range(4):
        pieces.append((0 + LANES * j, "aq", 0 + LANES * j))
    pieces.append((512, "ak", 2048))
    pieces.append((640, "copy", 2176))
    for j in range(4):
        pieces.append((768 + LANES * j, "gate", 512 + LANES * j))
    for j in range(4):
        pieces.append((1280 + LANES * j, "bq", 1024 + LANES * j))
    pieces.append((1792, "bk", 2304))
    pieces.append((1920, "copy", 2432))
    for j in range(4):
        pieces.append((2048 + LANES * j, "gate", 1536 + LANES * j))
    return pieces


def _rope(v, cos, sin_even, sin_odd):
    nxt = pltpu.roll(v, LANES - 1, 1)
    prv = pltpu.roll(v, 1, 1)
    return v * cos + nxt * sin_even + prv * sin_odd


def _head_rms(v, gain, bd_ref):
    ms = jnp.dot((v * v).astype(BF16), bd_ref[...], preferred_element_type=F32)
    return v * lax.rsqrt(ms + EPS) * gain


def _inproj_even_kernel(x_ref, mod_ref, g_ref, w_ref, qg_ref, kg_ref, cos_ref, se_ref, so_ref, bd_ref, o_ref):
    h = _normed_input(x_ref, mod_ref, g_ref)
    cos, s_even, s_odd = cos_ref[...], se_ref[...], so_ref[...]
    pieces = _even_pieces()
    for c in range(EVEN_IN // MXU_N):
        acc = jnp.dot(h, w_ref[:, c * MXU_N:(c + 1) * MXU_N], preferred_element_type=F32)
        for half in range(MXU_N // LANES):
            src = c * MXU_N + half * LANES
            (_, kind, dst), = [p for p in pieces if p[0] == src]
            v = acc[:, half * LANES:(half + 1) * LANES]
            if kind == "aq":
                v = _rope(_head_rms(v, qg_ref[...], bd_ref), cos, s_even, s_odd) * ATTN_SCALE
            elif kind == "ak":
                v = _rope(_head_rms(v, kg_ref[...], bd_ref), cos, s_even, s_odd)
            elif kind == "bq":
                v = _rope(v, cos, s_even, s_odd) * ATTN_SCALE
            elif kind == "bk":
                v = _rope(v, cos, s_even, s_odd)
            elif kind == "gate":
                v = _silu(v)
            o_ref[:, dst:dst + LANES] = v.astype(BF16)


def _rope_tables():
    t = np.arange(SEQ)
    row = (t // GRID_W).astype(np.float32)
    col = (t % GRID_W).astype(np.float32)
    half = HEAD_DIM // 2
    freqs = (ROPE_THETA ** (-np.arange(0, half, 2, dtype=np.float32) / half)).astype(np.float32)
    ang = np.concatenate([row[:, None] * freqs, col[:, None] * freqs], axis=-1).astype(np.float32)
    cos = np.repeat(np.cos(ang), 2, axis=-1)
    sin = np.repeat(np.sin(ang), 2, axis=-1)
    lane_even = (np.arange(HEAD_DIM) % 2 == 0)
    s_even = np.where(lane_even, -sin, 0.0)
    s_odd = np.where(lane_even, 0.0, sin)

    def full(tab, ctx_val):
        tab = np.concatenate([np.full((CTX_LEN, HEAD_DIM), ctx_val, np.float32), tab.astype(np.float32)], axis=0)
        return jnp.asarray(np.tile(tab, (1, LANES // HEAD_DIM)))

    return full(cos, 1.0), full(s_even, 0.0), full(s_odd, 0.0)


def _inproj_even(xs, mods, layer, norm_g, w_in, q_gain, k_gain, tables, bd):
    cos, s_even, s_odd = tables
    tile_spec = pl.BlockSpec((ROW_TILE, LANES), lambda b, t: (t, 0))
    const = lambda shape: pl.BlockSpec(shape, lambda b, t: (0,) * len(shape))
    return pl.pallas_call(
        _inproj_even_kernel,
        grid=(BATCH, N_TILES),
        in_specs=[
            pl.BlockSpec((None, ROW_TILE, D_MODEL), lambda b, t: (b, t, 0)),
            _mod_spec(layer),
            const((1, D_MODEL)),
            const((D_MODEL, EVEN_IN)),
            const((1, LANES)),
            const((1, LANES)),
            tile_spec, tile_spec, tile_spec,
            const((LANES, LANES)),
        ],
        out_specs=pl.BlockSpec((None, ROW_TILE, EVEN_IN), lambda b, t: (b, t, 0)),
        out_shape=jax.ShapeDtypeStruct((BATCH, T_ALL, EVEN_IN), BF16),
        compiler_params=_cparams(2),
    )(xs, mods, norm_g.reshape(1, D_MODEL), w_in.astype(BF16),
      jnp.tile(q_gain, LANES // HEAD_DIM).reshape(1, LANES),
      jnp.tile(k_gain, LANES // HEAD_DIM).reshape(1, LANES),
      cos, s_even, s_odd, bd)


def _softmax_pv(s_list, v_list, sink=None):
    m = s_list[0].max(axis=-1, keepdims=True)
    for s in s_list[1:]:
        m = jnp.maximum(m, s.max(axis=-1, keepdims=True))
    if sink is not None:
        m = jnp.maximum(m, sink)
    den = jnp.exp(sink - m) if sink is not None else 0.0
    out = 0.0
    for s, v in zip(s_list, v_list):
        p = jnp.exp(s - m)
        den = den + p.sum(axis=-1, keepdims=True)
        out = out + jnp.dot(p.astype(BF16), v, preferred_element_type=F32)
    return out / den


def _attn_a_kernel(q_ref, g_ref, k_ref, v_ref, o_ref):
    group = A_HEADS // A_KV_HEADS

    def run(n_keys):
        for h in range(A_HEADS):
            kv = h // group
            q = q_ref[:, h * HEAD_DIM:(h + 1) * HEAD_DIM]
            k = k_ref[0:n_keys, kv * HEAD_DIM:(kv + 1) * HEAD_DIM]
            v = v_ref[0:n_keys, kv * HEAD_DIM:(kv + 1) * HEAD_DIM]
            s = lax.dot_general(q, k, NT_DIMS, preferred_element_type=F32)
            o = _softmax_pv([s], [v])
            gate = g_ref[:, h * HEAD_DIM:(h + 1) * HEAD_DIM].astype(F32)
            o_ref[:, h * HEAD_DIM:(h + 1) * HEAD_DIM] = (o * gate).astype(BF16)

    t = pl.program_id(1)
    pl.when(t == 0)(lambda: run(CTX_LEN))
    pl.when(t > 0)(lambda: run(T_ALL))


def _attn_a(h_even):
    w = A_HEADS * HEAD_DIM
    return pl.pallas_call(
        _attn_a_kernel,
        grid=(BATCH, N_TILES),
        in_specs=[
            pl.BlockSpec((None, ROW_TILE, w), lambda b, t: (b, t, 0)),
            pl.BlockSpec((None, ROW_TILE, w), lambda b, t: (b, t, 1)),
            pl.BlockSpec((None, T_ALL, LANES), lambda b, t: (b, 0, 2048 // LANES)),
            pl.BlockSpec((None, T_ALL, LANES), lambda b, t: (b, 0, 2176 // LANES)),
        ],
        out_specs=pl.BlockSpec((None, ROW_TILE, w), lambda b, t: (b, t, 0)),
        out_shape=jax.ShapeDtypeStruct((BATCH, T_ALL, w), BF16),
        compiler_params=_cparams(2),
    )(h_even, h_even, h_even, h_even)


def _attn_b_kernel(sink_ref, q_ref, g_ref, k_ref, v_ref, o_ref):
    group = B_HEADS // B_KV_HEADS
    t = pl.program_id(1)

    def head_io(h):
        kv = h // group
        q = q_ref[:, h * HEAD_DIM:(h + 1) * HEAD_DIM]
        lanes = slice(kv * HEAD_DIM, (kv + 1) * HEAD_DIM)
        gate = g_ref[:, h * HEAD_DIM:(h + 1) * HEAD_DIM].astype(F32)
        return q, lanes, gate

    def ctx_tile():
        for h in range(B_HEADS):
            q, lanes, gate = head_io(h)
            s = lax.dot_general(q, k_ref[0:CTX_LEN, lanes], NT_DIMS, preferred_element_type=F32)
            o = _softmax_pv([s], [v_ref[0:CTX_LEN, lanes]], sink=sink_ref[h])
            o_ref[:, h * HEAD_DIM:(h + 1) * HEAD_DIM] = (o * gate).astype(BF16)

    def lat_tile():
        start = pl.multiple_of(jnp.clip(t * ROW_TILE - B_WINDOW, CTX_LEN, T_ALL - BAND_B), LANES)
        q_pos = t * ROW_TILE + lax.broadcasted_iota(jnp.int32, (ROW_TILE, BAND_B), 0)
        k_pos = start + lax.broadcasted_iota(jnp.int32, (ROW_TILE, BAND_B), 1)
        bias = jnp.where(jnp.abs(k_pos - q_pos) <= B_WINDOW, 0.0, NEG_INF).astype(F32)
        for h in range(B_HEADS):
            q, lanes, gate = head_io(h)
            kb = k_ref[pl.ds(start, BAND_B), :][:, lanes]
            vb = v_ref[pl.ds(start, BAND_B), :][:, lanes]
            s_ctx = lax.dot_general(q, k_ref[0:CTX_LEN, lanes], NT_DIMS, preferred_element_type=F32)
            s_band = lax.dot_general(q, kb, NT_DIMS, preferred_element_type=F32) + bias
            o = _softmax_pv([s_ctx, s_band], [v_ref[0:CTX_LEN, lanes], vb], sink=sink_ref[h])
            o_ref[:, h * HEAD_DIM:(h + 1) * HEAD_DIM] = (o * gate).astype(BF16)

    pl.when(t == 0)(ctx_tile)
    pl.when(t > 0)(lat_tile)


def _attn_b(h_even, sink):
    w = B_HEADS * HEAD_DIM
    return pl.pallas_call(
        _attn_b_kernel,
        grid=(BATCH, N_TILES),
        in_specs=[
            pl.BlockSpec(memory_space=pltpu.SMEM),
            pl.BlockSpec((None, ROW_TILE, w), lambda b, t: (b, t, 2)),
            pl.BlockSpec((None, ROW_TILE, w), lambda b, t: (b, t, 3)),
            pl.BlockSpec((None, T_ALL, LANES), lambda b, t: (b, 0, 2304 // LANES)),
            pl.BlockSpec((None, T_ALL, LANES), lambda b, t: (b, 0, 2432 // LANES)),
        ],
        out_specs=pl.BlockSpec((None, ROW_TILE, w), lambda b, t: (b, t, 0)),
        out_shape=jax.ShapeDtypeStruct((BATCH, T_ALL, w), BF16),
        compiler_params=_cparams(2),
    )(sink, h_even, h_even, h_even, h_even)


def _inproj_odd_kernel(x_ref, mod_ref, g_ref, w_ref, o_ref, u_ref):
    h = _normed_input(x_ref, mod_ref, g_ref)

    def proj(col):
        return jnp.dot(h, w_ref[:, col:col + MXU_N], preferred_element_type=F32)

    for c in range(C_WIDTH // MXU_N):
        val = proj(c * MXU_N)
        glu = proj(C_WIDTH + c * MXU_N)
        u_ref[:, c * MXU_N:(c + 1) * MXU_N] = val * jax.nn.sigmoid(glu)
    plan = [(2 * C_WIDTH, 0, "gate"), (3 * C_WIDTH, 512, "q"), (4 * C_WIDTH, 1024, "copy"),
            (5 * C_WIDTH, 1536, "copy"), (6 * C_WIDTH, 2048, "gate")]
    for src, dst, kind in plan:
        for c in range(C_WIDTH // MXU_N):
            v = proj(src + c * MXU_N)
            if kind == "gate":
                v = _silu(v)
            elif kind == "q":
                v = v * ATTN_SCALE
            o_ref[:, dst + c * MXU_N:dst + (c + 1) * MXU_N] = v.astype(BF16)


ODD_OUT = 2560


def _inproj_odd(xs, mods, layer, norm_g, w_in):
    const = lambda shape: pl.BlockSpec(shape, lambda b, t: (0,) * len(shape))
    return pl.pallas_call(
        _inproj_odd_kernel,
        grid=(BATCH, N_TILES),
        in_specs=[
            pl.BlockSpec((None, ROW_TILE, D_MODEL), lambda b, t: (b, t, 0)),
            _mod_spec(layer),
            const((1, D_MODEL)),
            const((D_MODEL, ODD_IN)),
        ],
        out_specs=[
            pl.BlockSpec((None, ROW_TILE, ODD_OUT), lambda b, t: (b, t, 0)),
            pl.BlockSpec((None, ROW_TILE, C_WIDTH), lambda b, t: (b, t, 0)),
        ],
        out_shape=[
            jax.ShapeDtypeStruct((BATCH, T_ALL, ODD_OUT), BF16),
            jax.ShapeDtypeStruct((BATCH, T_ALL, C_WIDTH), F32),
        ],
        compiler_params=_cparams(2),
    )(xs, mods, norm_g.reshape(1, D_MODEL), w_in.astype(BF16))


def _conv_kernel(u_ref, lh_ref, rh_ref, sg_ref, w_ref, b_ref, lg_ref, lb_ref, o_ref):
    t = pl.program_id(1)
    left_ok = (t >= 2).astype(F32)
    right_ok = jnp.logical_and(t >= 1, t <= N_TILES - 2).astype(F32)
    win = jnp.concatenate([lh_ref[...] * left_ok, u_ref[...], rh_ref[...] * right_ok], axis=0)
    n_win = ROW_TILE + 2 * HALO
    sub = 8
    acc = jnp.zeros((ROW_TILE, C_WIDTH), F32) + b_ref[...]
    for r in range(sub):
        shifted = win if r == 0 else pltpu.roll(win, n_win - r, 0)
        for k in range(C_CONV):
            off = HALO - C_CONV // 2 + k
            if off % sub != r:
                continue
            base = off - r
            acc = acc + shifted[base:base + ROW_TILE, :] * w_ref[k:k + 1, :]
    mu = jnp.mean(acc, axis=-1, keepdims=True)
    ctr = acc - mu
    var = jnp.mean(ctr * ctr, axis=-1, keepdims=True)
    y = ctr * lax.rsqrt(var + EPS) * lg_ref[...] + lb_ref[...]
    o_ref[...] = (_silu(y) * sg_ref[...].astype(F32)).astype(BF16)


def _conv_c(u, h_odd, dw_w, dw_b, ln_g, ln_b):
    per_tile = ROW_TILE // HALO
    n_halo = T_ALL // HALO
    const = lambda shape: pl.BlockSpec(shape, lambda b, t: (0,) * len(shape))
    return pl.pallas_call(
        _conv_kernel,
        grid=(BATCH, N_TILES),
        in_specs=[
            pl.BlockSpec((None, ROW_TILE, C_WIDTH), lambda b, t: (b, t, 0)),
            pl.BlockSpec((None, HALO, C_WIDTH), lambda b, t: (b, jnp.maximum(t * per_tile - 1, 0), 0)),
            pl.BlockSpec((None, HALO, C_WIDTH), lambda b, t: (b, jnp.minimum((t + 1) * per_tile, n_halo - 1), 0)),
            pl.BlockSpec((None, ROW_TILE, C_WIDTH), lambda b, t: (b, t, 0)),
            const((C_CONV, C_WIDTH)),
            const((1, C_WIDTH)), const((1, C_WIDTH)), const((1, C_WIDTH)),
        ],
        out_specs=pl.BlockSpec((None, ROW_TILE, C_WIDTH), lambda b, t: (b, t, 0)),
        out_shape=jax.ShapeDtypeStruct((BATCH, T_ALL, C_WIDTH), BF16),
        compiler_params=_cparams(2),
    )(u, u, u, h_odd, dw_w, dw_b.reshape(1, C_WIDTH), ln_g.reshape(1, C_WIDTH), ln_b.reshape(1, C_WIDTH))


TILE_GRID_ROWS = ROW_TILE // GRID_W
BAND_GRID_ROWS = BAND_D // GRID_W


def _na_bias_tables(rpb):
    cols = np.arange(GRID_W)
    col_start = np.clip(cols - NA_KW // 2, 0, GRID_W - NA_KW)
    col_in = (cols[None, :] >= col_start[:, None]) & (cols[None, :] < col_start[:, None] + NA_KW)
    dc_idx = np.clip(cols[None, :] - cols[:, None] + NA_KW - 1, 0, 2 * NA_KW - 2)
    n_q_tiles = ROWS // TILE_GRID_ROWS
    tabs = []
    for tile in (0, 1, n_q_tiles - 1):
        r0 = tile * TILE_GRID_ROWS
        ks = int(np.clip(r0 - NA_KH // 2, 0, ROWS - BAND_GRID_ROWS))
        r = r0 + np.arange(TILE_GRID_ROWS)
        kr = ks + np.arange(BAND_GRID_ROWS)
        rs = np.clip(r - NA_KH // 2, 0, ROWS - NA_KH)
        row_in = (kr[None, :] >= rs[:, None]) & (kr[None, :] < rs[:, None] + NA_KH)
        dr_idx = np.clip(kr[None, :] - r[:, None] + NA_KH - 1, 0, 2 * NA_KH - 2)
        vals = rpb[:, dr_idx][:, :, :, dc_idx]
        ok = row_in[:, :, None, None] & col_in[None, None, :, :]
        vals = jnp.where(jnp.asarray(ok)[None], vals, NEG_INF)
        vals = jnp.transpose(vals, (0, 1, 3, 2, 4)).reshape(D_HEADS, ROW_TILE, BAND_D)
        tabs.append(vals)
    return jnp.stack(tabs).astype(F32)


def _attn_d_kernel(q_ref, g_ref, k_ref, v_ref, bias_ref, o_ref):
    t = pl.program_id(1)

    def ctx_tile():
        for h in range(D_HEADS):
            lanes = slice(h * HEAD_DIM, (h + 1) * HEAD_DIM)
            s = lax.dot_general(q_ref[:, lanes], k_ref[0:CTX_LEN, lanes], NT_DIMS, preferred_element_type=F32)
            o = _softmax_pv([s], [v_ref[0:CTX_LEN, lanes]])
            o_ref[:, lanes] = (o * g_ref[:, lanes].astype(F32)).astype(BF16)

    def lat_tile():
        start = pl.multiple_of(
            jnp.clip((t - 1) * ROW_TILE - (NA_KH // 2) * GRID_W, 0, SEQ - BAND_D) + CTX_LEN, LANES)
        for h in range(D_HEADS):
            lanes = slice(h * HEAD_DIM, (h + 1) * HEAD_DIM)
            q = q_ref[:, lanes]
            kb = k_ref[pl.ds(start, BAND_D), :][:, lanes]
            vb = v_ref[pl.ds(start, BAND_D), :][:, lanes]
            s_ctx = lax.dot_general(q, k_ref[0:CTX_LEN, lanes], NT_DIMS, preferred_element_type=F32)
            s_band = lax.dot_general(q, kb, NT_DIMS, preferred_element_type=F32) + bias_ref[h]
            o = _softmax_pv([s_ctx, s_band], [v_ref[0:CTX_LEN, lanes], vb])
            o_ref[:, lanes] = (o * g_ref[:, lanes].astype(F32)).astype(BF16)

    pl.when(t == 0)(ctx_tile)
    pl.when(t > 0)(lat_tile)


def _attn_d(h_odd, bias_tabs):
    w = D_HEADS * HEAD_DIM

    def variant(b, t):
        return (jnp.where(t <= 1, 0, jnp.where(t == N_TILES - 1, 2, 1)), 0, 0, 0)

    return pl.pallas_call(
        _attn_d_kernel,
        grid=(BATCH, N_TILES),
        in_specs=[
            pl.BlockSpec((None, ROW_TILE, w), lambda b, t: (b, t, 1)),
            pl.BlockSpec((None, ROW_TILE, w), lambda b, t: (b, t, 4)),
            pl.BlockSpec((None, T_ALL, w), lambda b, t: (b, 0, 2)),
            pl.BlockSpec((None, T_ALL, w), lambda b, t: (b, 0, 3)),
            pl.BlockSpec((None, D_HEADS, ROW_TILE, BAND_D), variant),
        ],
        out_specs=pl.BlockSpec((None, ROW_TILE, w), lambda b, t: (b, t, 0)),
        out_shape=jax.ShapeDtypeStruct((BATCH, T_ALL, w), BF16),
        compiler_params=_cparams(2),
    )(h_odd, h_odd, h_odd, h_odd, bias_tabs)


def _outproj_kernel(ya_ref, yb_ref, w_ref, x_ref, mod_ref, o_ref):
    half = w_ref.shape[0] // 2
    y = jnp.dot(ya_ref[...], w_ref[0:half, :], preferred_element_type=F32)
    y = y + jnp.dot(yb_ref[...], w_ref[half:, :], preferred_element_type=F32)
    o_ref[...] = x_ref[...] + mod_ref[:, 2 * D_MODEL:3 * D_MODEL] * y


def _outproj(ya, yb, w_out, xs, mods, layer):
    const = lambda shape: pl.BlockSpec(shape, lambda b, t: (0,) * len(shape))
    half = w_out.shape[0] // 2
    return pl.pallas_call(
        _outproj_kernel,
        grid=(BATCH, N_TILES),
        in_specs=[
            pl.BlockSpec((None, ROW_TILE, half), lambda b, t: (b, t, 0)),
            pl.BlockSpec((None, ROW_TILE, half), lambda b, t: (b, t, 0)),
            const((2 * half, D_MODEL)),
            pl.BlockSpec((None, ROW_TILE, D_MODEL), lambda b, t: (b, t, 0)),
            _mod_spec(layer),
        ],
        out_specs=pl.BlockSpec((None, ROW_TILE, D_MODEL), lambda b, t: (b, t, 0)),
        out_shape=jax.ShapeDtypeStruct((BATCH, T_ALL, D_MODEL), F32),
        input_output_aliases={3: 0},
        compiler_params=_cparams(2),
    )(ya, yb, w_out.astype(BF16), xs, mods)


def _outproj_final_kernel(ya_ref, yb_ref, w_ref, x_ref, mod_ref, fg_ref, o_ref):
    half = w_ref.shape[0] // 2
    y = jnp.dot(ya_ref[...], w_ref[0:half, :], preferred_element_type=F32)
    y = y + jnp.dot(yb_ref[...], w_ref[half:, :], preferred_element_type=F32)
    x = x_ref[...] + mod_ref[:, 2 * D_MODEL:3 * D_MODEL] * y
    ms = jnp.mean(x * x, axis=-1, keepdims=True)
    o_ref[...] = x * lax.rsqrt(ms + EPS) * fg_ref[...]


def _outproj_final(ya, yb, w_out, xs, mods, layer, final_g):
    half = w_out.shape[0] // 2
    n_lat = SEQ // ROW_TILE
    return pl.pallas_call(
        _outproj_final_kernel,
        grid=(BATCH, n_lat),
        in_specs=[
            pl.BlockSpec((None, ROW_TILE, half), lambda b, t: (b, t + 1, 0)),
            pl.BlockSpec((None, ROW_TILE, half), lambda b, t: (b, t + 1, 0)),
            pl.BlockSpec((2 * half, D_MODEL), lambda b, t: (0, 0)),
            pl.BlockSpec((None, ROW_TILE, D_MODEL), lambda b, t: (b, t + 1, 0)),
            pl.BlockSpec((None, 1, 3 * D_MODEL), lambda b, t: (layer * MOD_ROWS + b, 0, 0)),
            pl.BlockSpec((1, D_MODEL), lambda b, t: (0, 0)),
        ],
        out_specs=pl.BlockSpec((None, ROW_TILE, D_MODEL), lambda b, t: (b, t, 0)),
        out_shape=jax.ShapeDtypeStruct((BATCH, SEQ, D_MODEL), F32),
        compiler_params=_cparams(2),
    )(ya, yb, w_out.astype(BF16), xs, mods, final_g.reshape(1, D_MODEL))


def kernel(x, c, ctx, c_ctx, mod_w, mod_b, norm_g, ev_w_in, ev_w_out, a_q_gain, a_k_gain, b_sink,
           od_w_in, od_w_out, c_dw_w, c_dw_b, c_ln_g, c_ln_b, d_rpb, final_g):
    xs = jnp.concatenate([ctx, x], axis=1)
    cond = jnp.concatenate(
        [c, c_ctx[None, :], jnp.zeros((MOD_ROWS - BATCH - 1, D_MODEL), F32)], axis=0)
    mods = _modulation_all(cond, mod_w, mod_b).reshape(DEPTH * MOD_ROWS, 1, 3 * D_MODEL)
    tables = _rope_tables()
    head_of_lane = np.arange(LANES) // HEAD_DIM
    bd = jnp.asarray((head_of_lane[:, None] == head_of_lane[None, :]).astype(np.float32) / HEAD_DIM, BF16)

    out = None
    for i in range(DEPTH):
        j = i // 2
        last = i == DEPTH - 1
        if i % 2 == 0:
            h_even = _inproj_even(xs, mods, i, norm_g[i], ev_w_in[j], a_q_gain[j], a_k_gain[j], tables, bd)
            ya = _attn_a(h_even)
            yb = _attn_b(h_even, b_sink[j])
            w_out = ev_w_out[j]
        else:
            h_odd, u = _inproj_odd(xs, mods, i, norm_g[i], od_w_in[j])
            ya = _conv_c(u, h_odd, c_dw_w[j], c_dw_b[j], c_ln_g[j], c_ln_b[j])
            yb = _attn_d(h_odd, _na_bias_tables(d_rpb[j]))
            w_out = od_w_out[j]
        if last:
            out = _outproj_final(ya, yb, w_out, xs, mods, i, final_g)
        else:
            xs = _outproj(ya, yb, w_out, xs, mods, i)
    return out
```

```python
import numpy as np
import jax
import jax.numpy as jnp
from jax import lax
from jax.experimental import pallas as pl
from jax.experimental.pallas import tpu as pltpu

D_MODEL = 1024
BATCH = 8
SEQ = 2048
DEPTH = 4
GRID_W = 64
CTX_LEN = 256
HEAD_DIM = 64
ROPE_THETA = 10000.0
EPS = 1e-6
NEG_INF = -1e30
ATTN_SCALE = HEAD_DIM ** -0.5
A_HEADS = 8
A_KV_HEADS = 2
B_HEADS = 8
B_KV_HEADS = 2
B_WINDOW = 128
C_WIDTH = 512
C_CONV = 31
D_HEADS = 8
NA_KH = 8
NA_KW = 16
ROWS = SEQ // GRID_W

T_ALL = CTX_LEN + SEQ
ROW_TILE = 256
N_TILES = T_ALL // ROW_TILE
LANES = 128
MXU_N = 256
MOD_ROWS = 16
CTX_MOD_ROW = BATCH
VMEM_LIMIT = 48 * 1024 * 1024

EVEN_IN = 2560
ODD_IN = 3584
LOG2E = 1.4426950408889634
Q_SCALE = ATTN_SCALE * LOG2E
HALO = 16
BAND_B = 512
BAND_D = 768

F32 = jnp.float32
BF16 = jnp.bfloat16
NT_DIMS = (((1,), (1,)), ((), ()))


def _cparams(n_grid):
    return pltpu.CompilerParams(dimension_semantics=("arbitrary",) * n_grid,
                                vmem_limit_bytes=VMEM_LIMIT)


def _silu(v):
    return v * jax.nn.sigmoid(v)


def _mod_kernel(cond_ref, w_ref, b_ref, o_ref):
    cnd = cond_ref[...]
    act = _silu(cnd).astype(BF16)
    o_ref[...] = jnp.dot(act, w_ref[...].astype(BF16), preferred_element_type=F32) + b_ref[...]


def _modulation_all(cond, mod_w, mod_b):
    n_col = 3
    return pl.pallas_call(
        _mod_kernel,
        grid=(DEPTH, n_col),
        in_specs=[
            pl.BlockSpec((MOD_ROWS, D_MODEL), lambda i, j: (0, 0)),
            pl.BlockSpec((None, D_MODEL, D_MODEL), lambda i, j: (i, 0, j)),
            pl.BlockSpec((None, 1, D_MODEL), lambda i, j: (i, 0, j)),
        ],
        out_specs=pl.BlockSpec((None, MOD_ROWS, D_MODEL), lambda i, j: (i, 0, j)),
        out_shape=jax.ShapeDtypeStruct((DEPTH, MOD_ROWS, 3 * D_MODEL), F32),
        compiler_params=_cparams(2),
    )(cond, mod_w, mod_b.reshape(DEPTH, 1, 3 * D_MODEL))


def _mod_spec(layer):
    return pl.BlockSpec((None, 1, 3 * D_MODEL),
                        lambda b, t: (layer * MOD_ROWS + jnp.where(t == 0, CTX_MOD_ROW, b), 0, 0))


def _normed_input(x_ref, mod_ref, g_ref):
    x = x_ref[...]
    ms = jnp.mean(x * x, axis=-1, keepdims=True)
    y = x * lax.rsqrt(ms + EPS) * g_ref[...]
    shift = mod_ref[:, 0:D_MODEL]
    scale = mod_ref[:, D_MODEL:2 * D_MODEL]
    return (y * (1.0 + scale) + shift).astype(BF16)


EVEN_OUT = 2816
EV_AQ, EV_AG, EV_BQ, EV_BG, EV_AV, EV_BV, EV_AK, EV_BK = 0, 512, 1024, 1536, 2048, 2304, 2560, 2688


def _even_pieces():
    pieces = []
    for j in range(4):
        pieces.append((0 + LANES * j, "aq", EV_AQ + LANES * j))
    pieces.append((512, "ak", EV_AK))
    pieces.append((640, "v", EV_AV))
    for j in range(4):
        pieces.append((768 + LANES * j, "gate", EV_AG + LANES * j))
    for j in range(4):
        pieces.append((1280 + LANES * j, "bq", EV_BQ + LANES * j))
    pieces.append((1792, "bk", EV_BK))
    pieces.append((1920, "v", EV_BV))
    for j in range(4):
        pieces.append((2048 + LANES * j, "gate", EV_BG + LANES * j))
    return pieces


def _rope(v, cos, sin_even, sin_odd):
    nxt = pltpu.roll(v, LANES - 1, 1)
    prv = pltpu.roll(v, 1, 1)
    return v * cos + nxt * sin_even + prv * sin_odd


def _head_rms(v, gain, bd_ref):
    ms = jnp.dot((v * v).astype(BF16), bd_ref[...], preferred_element_type=F32)
    return v * lax.rsqrt(ms + EPS) * gain


def _inproj_even_kernel(x_ref, mod_ref, g_ref, w_ref, qg_ref, kg_ref, cos_ref, se_ref, so_ref, bd_ref, o_ref):
    h = _normed_input(x_ref, mod_ref, g_ref)
    cos, s_even, s_odd = cos_ref[...], se_ref[...], so_ref[...]
    pieces = _even_pieces()
    ones = jnp.ones((ROW_TILE, LANES), BF16)
    for c in range(EVEN_IN // MXU_N):
        acc = jnp.dot(h, w_ref[:, c * MXU_N:(c + 1) * MXU_N], preferred_element_type=F32)
        for half in range(MXU_N // LANES):
            src = c * MXU_N + half * LANES
            (_, kind, dst), = [p for p in pieces if p[0] == src]
            v = acc[:, half * LANES:(half + 1) * LANES]
            if kind == "aq":
                v = _rope(_head_rms(v, qg_ref[...], bd_ref), cos, s_even, s_odd) * Q_SCALE
            elif kind == "ak":
                v = _rope(_head_rms(v, kg_ref[...], bd_ref), cos, s_even, s_odd)
            elif kind == "bq":
                v = _rope(v, cos, s_even, s_odd) * Q_SCALE
            elif kind == "bk":
                v = _rope(v, cos, s_even, s_odd)
            elif kind == "gate":
                v = _silu(v)
            elif kind == "v":
                o_ref[:, dst + LANES:dst + 2 * LANES] = ones
            o_ref[:, dst:dst + LANES] = v.astype(BF16)


def _rope_tables():
    t = np.arange(SEQ)
    row = (t // GRID_W).astype(np.float64)
    col = (t % GRID_W).astype(np.float64)
    half = HEAD_DIM // 2
    freqs = (np.float32(ROPE_THETA) ** (-np.arange(0, half, 2, dtype=np.float32) / half)).astype(np.float64)
    ang = np.concatenate([row[:, None] * freqs, col[:, None] * freqs], axis=-1).astype(np.float32)
    cos = np.repeat(np.cos(ang.astype(np.float64)), 2, axis=-1)
    sin = np.repeat(np.sin(ang.astype(np.float64)), 2, axis=-1)
    lane_even = (np.arange(HEAD_DIM) % 2 == 0)
    s_even = np.where(lane_even, -sin, 0.0)
    s_odd = np.where(lane_even, 0.0, sin)

    def full(tab, ctx_val):
        tab = np.concatenate([np.full((CTX_LEN, HEAD_DIM), ctx_val), tab], axis=0).astype(np.float32)
        return jnp.asarray(np.tile(tab, (1, LANES // HEAD_DIM)))

    return full(cos, 1.0), full(s_even, 0.0), full(s_odd, 0.0)


def _inproj_even(xs, mods, layer, norm_g, w_in, q_gain, k_gain, tables, bd):
    cos, s_even, s_odd = tables
    tile_spec = pl.BlockSpec((ROW_TILE, LANES), lambda b, t: (t, 0))
    const = lambda shape: pl.BlockSpec(shape, lambda b, t: (0,) * len(shape))
    return pl.pallas_call(
        _inproj_even_kernel,
        grid=(BATCH, N_TILES),
        in_specs=[
            pl.BlockSpec((None, ROW_TILE, D_MODEL), lambda b, t: (b, t, 0)),
            _mod_spec(layer),
            const((1, D_MODEL)),
            const((D_MODEL, EVEN_IN)),
            const((1, LANES)),
            const((1, LANES)),
            tile_spec, tile_spec, tile_spec,
            const((LANES, LANES)),
        ],
        out_specs=pl.BlockSpec((None, ROW_TILE, EVEN_OUT), lambda b, t: (b, t, 0)),
        out_shape=jax.ShapeDtypeStruct((BATCH, T_ALL, EVEN_OUT), BF16),
        compiler_params=_cparams(2),
    )(xs, mods, norm_g.reshape(1, D_MODEL), w_in.astype(BF16),
      jnp.tile(q_gain, LANES // HEAD_DIM).reshape(1, LANES),
      jnp.tile(k_gain, LANES // HEAD_DIM).reshape(1, LANES),
      cos, s_even, s_odd, bd)


def _softmax_pv(s_list, v_list, sink=None):
    m = s_list[0].max(axis=-1, keepdims=True)
    for s in s_list[1:]:
        m = jnp.maximum(m, s.max(axis=-1, keepdims=True))
    if sink is not None:
        m = jnp.maximum(m, sink)
    acc = None
    for s, v in zip(s_list, v_list):
        p = jnp.exp2(s - m).astype(BF16)
        part = jnp.dot(p, v, preferred_element_type=F32)
        acc = part if acc is None else acc + part
    extra = jnp.exp2(sink - m) if sink is not None else None
    return acc, extra


def _pipelined(n_heads, scores_fn, finish_fn):
    s = scores_fn(0)
    for h in range(n_heads):
        s_next = scores_fn(h + 1) if h + 1 < n_heads else None
        finish_fn(h, s)
        s = s_next


class _PairStore:
    def __init__(self, o_ref, g_ref):
        self.o_ref, self.g_ref, self.held = o_ref, g_ref, None

    def put(self, h, o):
        if h % 2 == 0:
            self.held = o
            return
        lanes = slice((h - 1) * HEAD_DIM, (h + 1) * HEAD_DIM)
        pair = jnp.concatenate([self.held, o], axis=1)
        self.o_ref[:, lanes] = (pair * self.g_ref[:, lanes].astype(F32)).astype(BF16)


def _gqa_normalise(acc, extra, kv):
    den = acc[:, LANES:2 * LANES]
    if extra is not None:
        den = den + extra
    o = acc[:, 0:LANES] / den
    return o[:, kv * HEAD_DIM:(kv + 1) * HEAD_DIM]


def _attn_a_kernel(q_ref, g_ref, k_ref, v_ref, o_ref):
    group = A_HEADS // A_KV_HEADS

    def run(n_keys):
        out = _PairStore(o_ref, g_ref)

        def scores(h):
            kv = h // group
            q = q_ref[:, h * HEAD_DIM:(h + 1) * HEAD_DIM]
            k = k_ref[0:n_keys, kv * HEAD_DIM:(kv + 1) * HEAD_DIM]
            return [lax.dot_general(q, k, NT_DIMS, preferred_element_type=F32)]

        def finish(h, s_list):
            acc, extra = _softmax_pv(s_list, [v_ref[0:n_keys, :]])
            out.put(h, _gqa_normalise(acc, extra, h // group))

        _pipelined(A_HEADS, scores, finish)

    t = pl.program_id(1)
    pl.when(t == 0)(lambda: run(CTX_LEN))
    pl.when(t > 0)(lambda: run(T_ALL))


def _attn_a(h_even):
    w = A_HEADS * HEAD_DIM
    return pl.pallas_call(
        _attn_a_kernel,
        grid=(BATCH, N_TILES),
        in_specs=[
            pl.BlockSpec((None, ROW_TILE, w), lambda b, t: (b, t, EV_AQ // w)),
            pl.BlockSpec((None, ROW_TILE, w), lambda b, t: (b, t, EV_AG // w)),
            pl.BlockSpec((None, T_ALL, LANES), lambda b, t: (b, 0, EV_AK // LANES)),
            pl.BlockSpec((None, T_ALL, 2 * LANES), lambda b, t: (b, 0, EV_AV // (2 * LANES))),
        ],
        out_specs=pl.BlockSpec((None, ROW_TILE, w), lambda b, t: (b, t, 0)),
        out_shape=jax.ShapeDtypeStruct((BATCH, T_ALL, w), BF16),
        compiler_params=_cparams(2),
    )(h_even, h_even, h_even, h_even)


def _attn_b_kernel(sink_ref, q_ref, g_ref, k_ref, v_ref, o_ref):
    group = B_HEADS // B_KV_HEADS
    t = pl.program_id(1)

    def q_and_lanes(h):
        kv = h // group
        return q_ref[:, h * HEAD_DIM:(h + 1) * HEAD_DIM], slice(kv * HEAD_DIM, (kv + 1) * HEAD_DIM)

    def ctx_tile():
        out = _PairStore(o_ref, g_ref)

        def scores(h):
            q, lanes = q_and_lanes(h)
            return [lax.dot_general(q, k_ref[0:CTX_LEN, lanes], NT_DIMS, preferred_element_type=F32)]

        def finish(h, s_list):
            acc, extra = _softmax_pv(s_list, [v_ref[0:CTX_LEN, :]], sink=sink_ref[h] * LOG2E)
            out.put(h, _gqa_normalise(acc, extra, h // group))

        _pipelined(B_HEADS, scores, finish)

    def lat_tile():
        out = _PairStore(o_ref, g_ref)
        start = pl.multiple_of(jnp.clip(t * ROW_TILE - B_WINDOW, CTX_LEN, T_ALL - BAND_B), LANES)
        q_pos = t * ROW_TILE + lax.broadcasted_iota(jnp.int32, (ROW_TILE, BAND_B), 0)
        k_pos = start + lax.broadcasted_iota(jnp.int32, (ROW_TILE, BAND_B), 1)
        bias = jnp.where(jnp.abs(k_pos - q_pos) <= B_WINDOW, 0.0, NEG_INF).astype(F32)

        def scores(h):
            q, lanes = q_and_lanes(h)
            kb = k_ref[pl.ds(start, BAND_B), :][:, lanes]
            s_ctx = lax.dot_general(q, k_ref[0:CTX_LEN, lanes], NT_DIMS, preferred_element_type=F32)
            s_band = lax.dot_general(q, kb, NT_DIMS, preferred_element_type=F32) + bias
            return [s_ctx, s_band]

        def finish(h, s_list):
            v_list = [v_ref[0:CTX_LEN, :], v_ref[pl.ds(start, BAND_B), :]]
            acc, extra = _softmax_pv(s_list, v_list, sink=sink_ref[h] * LOG2E)
            out.put(h, _gqa_normalise(acc, extra, h // group))

        _pipelined(B_HEADS, scores, finish)

    pl.when(t == 0)(ctx_tile)
    pl.when(t > 0)(lat_tile)


def _attn_b(h_even, sink):
    w = B_HEADS * HEAD_DIM
    return pl.pallas_call(
        _attn_b_kernel,
        grid=(BATCH, N_TILES),
        in_specs=[
            pl.BlockSpec(memory_space=pltpu.SMEM),
            pl.BlockSpec((None, ROW_TILE, w), lambda b, t: (b, t, EV_BQ // w)),
            pl.BlockSpec((None, ROW_TILE, w), lambda b, t: (b, t, EV_BG // w)),
            pl.BlockSpec((None, T_ALL, LANES), lambda b, t: (b, 0, EV_BK // LANES)),
            pl.BlockSpec((None, T_ALL, 2 * LANES), lambda b, t: (b, 0, EV_BV // (2 * LANES))),
        ],
        out_specs=pl.BlockSpec((None, ROW_TILE, w), lambda b, t: (b, t, 0)),
        out_shape=jax.ShapeDtypeStruct((BATCH, T_ALL, w), BF16),
        compiler_params=_cparams(2),
    )(sink, h_even, h_even, h_even, h_even)


ODD_OUT = 3072
OD_CG, OD_DQ, OD_DK, OD_DG, OD_DV = 0, 512, 1024, 1536, 2048


def _inproj_odd_kernel(x_ref, mod_ref, g_ref, w_ref, o_ref, u_ref):
    h = _normed_input(x_ref, mod_ref, g_ref)

    def proj(col):
        return jnp.dot(h, w_ref[:, col:col + MXU_N], preferred_element_type=F32)

    for c in range(C_WIDTH // MXU_N):
        val = proj(c * MXU_N)
        glu = proj(C_WIDTH + c * MXU_N)
        u_ref[:, c * MXU_N:(c + 1) * MXU_N] = val * jax.nn.sigmoid(glu)
    plan = [(2 * C_WIDTH, OD_CG, "gate"), (3 * C_WIDTH, OD_DQ, "q"), (4 * C_WIDTH, OD_DK, "copy"),
            (6 * C_WIDTH, OD_DG, "gate")]
    for src, dst, kind in plan:
        for c in range(C_WIDTH // MXU_N):
            v = proj(src + c * MXU_N)
            if kind == "gate":
                v = _silu(v)
            elif kind == "q":
                v = v * Q_SCALE
            o_ref[:, dst + c * MXU_N:dst + (c + 1) * MXU_N] = v.astype(BF16)
    ones = jnp.ones((ROW_TILE, HEAD_DIM), F32)
    heads_per_chunk = MXU_N // HEAD_DIM
    for c in range(C_WIDTH // MXU_N):
        v = proj(5 * C_WIDTH + c * MXU_N)
        for i in range(heads_per_chunk):
            head = c * heads_per_chunk + i
            ext = jnp.concatenate([v[:, i * HEAD_DIM:(i + 1) * HEAD_DIM], ones], axis=1)
            o_ref[:, OD_DV + head * LANES:OD_DV + (head + 1) * LANES] = ext.astype(BF16)


def _inproj_odd(xs, mods, layer, norm_g, w_in):
    const = lambda shape: pl.BlockSpec(shape, lambda b, t: (0,) * len(shape))
    return pl.pallas_call(
        _inproj_odd_kernel,
        grid=(BATCH, N_TILES),
        in_specs=[
            pl.BlockSpec((None, ROW_TILE, D_MODEL), lambda b, t: (b, t, 0)),
            _mod_spec(layer),
            const((1, D_MODEL)),
            const((D_MODEL, ODD_IN)),
        ],
        out_specs=[
            pl.BlockSpec((None, ROW_TILE, ODD_OUT), lambda b, t: (b, t, 0)),
            pl.BlockSpec((None, ROW_TILE, C_WIDTH), lambda b, t: (b, t, 0)),
        ],
        out_shape=[
            jax.ShapeDtypeStruct((BATCH, T_ALL, ODD_OUT), BF16),
            jax.ShapeDtypeStruct((BATCH, T_ALL, C_WIDTH), F32),
        ],
        compiler_params=_cparams(2),
    )(xs, mods, norm_g.reshape(1, D_MODEL), w_in.astype(BF16))


def _conv_kernel(u_ref, lh_ref, rh_ref, sg_ref, w_ref, b_ref, lg_ref, lb_ref, o_ref):
    t = pl.program_id(1)
    left_ok = (t >= 2).astype(F32)
    right_ok = jnp.logical_and(t >= 1, t <= N_TILES - 2).astype(F32)
    win = jnp.concatenate([lh_ref[...] * left_ok, u_ref[...], rh_ref[...] * right_ok], axis=0)
    n_win = ROW_TILE + 2 * HALO
    sub = 8
    acc = jnp.zeros((ROW_TILE, C_WIDTH), F32) + b_ref[...]
    for r in range(sub):
        shifted = win if r == 0 else pltpu.roll(win, n_win - r, 0)
        for k in range(C_CONV):
            off = HALO - C_CONV // 2 + k
            if off % sub != r:
                continue
            base = off - r
            acc = acc + shifted[base:base + ROW_TILE, :] * w_ref[k:k + 1, :]
    mu = jnp.mean(acc, axis=-1, keepdims=True)
    ctr = acc - mu
    var = jnp.mean(ctr * ctr, axis=-1, keepdims=True)
    y = ctr * lax.rsqrt(var + EPS) * lg_ref[...] + lb_ref[...]
    o_ref[...] = (_silu(y) * sg_ref[...].astype(F32)).astype(BF16)


def _conv_c(u, h_odd, dw_w, dw_b, ln_g, ln_b):
    per_tile = ROW_TILE // HALO
    n_halo = T_ALL // HALO
    const = lambda shape: pl.BlockSpec(shape, lambda b, t: (0,) * len(shape))
    return pl.pallas_call(
        _conv_kernel,
        grid=(BATCH, N_TILES),
        in_specs=[
            pl.BlockSpec((None, ROW_TILE, C_WIDTH), lambda b, t: (b, t, 0)),
            pl.BlockSpec((None, HALO, C_WIDTH), lambda b, t: (b, jnp.maximum(t * per_tile - 1, 0), 0)),
            pl.BlockSpec((None, HALO, C_WIDTH), lambda b, t: (b, jnp.minimum((t + 1) * per_tile, n_halo - 1), 0)),
            pl.BlockSpec((None, ROW_TILE, C_WIDTH), lambda b, t: (b, t, OD_CG // C_WIDTH)),
            const((C_CONV, C_WIDTH)),
            const((1, C_WIDTH)), const((1, C_WIDTH)), const((1, C_WIDTH)),
        ],
        out_specs=pl.BlockSpec((None, ROW_TILE, C_WIDTH), lambda b, t: (b, t, 0)),
        out_shape=jax.ShapeDtypeStruct((BATCH, T_ALL, C_WIDTH), BF16),
        compiler_params=_cparams(2),
    )(u, u, u, h_odd, dw_w, dw_b.reshape(1, C_WIDTH), ln_g.reshape(1, C_WIDTH), ln_b.reshape(1, C_WIDTH))


TILE_GRID_ROWS = ROW_TILE // GRID_W
BAND_GRID_ROWS = BAND_D // GRID_W
N_DR = 2 * NA_KH - 1
PAIR_ENTRIES = 17
PAIR_RIGHT_ONLY, PAIR_LEFT_ONLY = 15, 16


def _na_pair_table(rpb):
    cols = np.arange(GRID_W)
    col_start = np.clip(cols - NA_KW // 2, 0, GRID_W - NA_KW)
    col_in = (cols[None, :] >= col_start[:, None]) & (cols[None, :] < col_start[:, None] + NA_KW)
    dc_idx = np.clip(cols[None, :] - cols[:, None] + NA_KW - 1, 0, 2 * NA_KW - 2)
    tab = jnp.where(jnp.asarray(col_in)[None, None], rpb.astype(F32)[:, :, dc_idx] * LOG2E, NEG_INF)
    masked = jnp.full((D_HEADS, 1, GRID_W, GRID_W), NEG_INF, F32)
    interior_first_dr = NA_KH // 2 - 1
    interior_last_dr = interior_first_dr + NA_KH - 1
    left = jnp.concatenate([masked, tab[:, 0:N_DR - 1], masked, tab[:, interior_last_dr:interior_last_dr + 1]], axis=1)
    right = jnp.concatenate([masked, tab[:, 1:N_DR], tab[:, interior_first_dr:interior_first_dr + 1], masked], axis=1)
    return jnp.concatenate([left, right], axis=-1)


def _na_pair_entry(variant, a, p):
    if variant == "first":
        return 2 * p - a + NA_KH if p < NA_KH // 2 else 0
    if variant == "last":
        return 2 * p - a if p >= (BAND_GRID_ROWS - NA_KH) // 2 else 0
    left_ok = a <= 2 * p < a + NA_KH
    right_ok = a <= 2 * p + 1 < a + NA_KH
    if left_ok and right_ok:
        return 2 * p - a + NA_KH // 2
    if right_ok:
        return PAIR_RIGHT_ONLY
    if left_ok:
        return PAIR_LEFT_ONLY
    return 0


def _attn_d_kernel(q_ref, g_ref, k_ref, v_ref, tab_ref, o_ref):
    t = pl.program_id(1)

    def normalise(acc):
        return (acc / pltpu.roll(acc, HEAD_DIM, 1))[:, 0:HEAD_DIM]

    def ctx_tile():
        out = _PairStore(o_ref, g_ref)

        def scores(h):
            lanes = slice(h * HEAD_DIM, (h + 1) * HEAD_DIM)
            return [lax.dot_general(q_ref[:, lanes], k_ref[0:CTX_LEN, lanes], NT_DIMS, preferred_element_type=F32)]

        def finish(h, s_list):
            acc, _ = _softmax_pv(s_list, [v_ref[0:CTX_LEN, h * LANES:(h + 1) * LANES]])
            out.put(h, normalise(acc))

        _pipelined(D_HEADS, scores, finish)

    def lat_tile():
        out = _PairStore(o_ref, g_ref)
        start = pl.multiple_of(
            jnp.clip((t - 1) * ROW_TILE - (NA_KH // 2) * GRID_W, 0, SEQ - BAND_D) + CTX_LEN, LANES)
        is_first = t == 1
        is_last = t == N_TILES - 1

        def bias(h):
            rows = []
            for a in range(TILE_GRID_ROWS):
                blocks = []
                for p in range(BAND_GRID_ROWS // 2):
                    entry = jnp.where(is_first, _na_pair_entry("first", a, p),
                                      jnp.where(is_last, _na_pair_entry("last", a, p),
                                                _na_pair_entry("interior", a, p)))
                    blocks.append(tab_ref[h, entry])
                rows.append(jnp.concatenate(blocks, axis=1))
            return jnp.concatenate(rows, axis=0)

        def scores(h):
            lanes = slice(h * HEAD_DIM, (h + 1) * HEAD_DIM)
            q = q_ref[:, lanes]
            kb = k_ref[pl.ds(start, BAND_D), :][:, lanes]
            s_ctx = lax.dot_general(q, k_ref[0:CTX_LEN, lanes], NT_DIMS, preferred_element_type=F32)
            s_band = lax.dot_general(q, kb, NT_DIMS, preferred_element_type=F32) + bias(h)
            return [s_ctx, s_band]

        def finish(h, s_list):
            v_lanes = slice(h * LANES, (h + 1) * LANES)
            v_list = [v_ref[0:CTX_LEN, v_lanes], v_ref[pl.ds(start, BAND_D), :][:, v_lanes]]
            acc, _ = _softmax_pv(s_list, v_list)
            out.put(h, normalise(acc))

        _pipelined(D_HEADS, scores, finish)

    pl.when(t == 0)(ctx_tile)
    pl.when(t > 0)(lat_tile)


def _attn_d(h_odd, pair_tab):
    w = D_HEADS * HEAD_DIM
    return pl.pallas_call(
        _attn_d_kernel,
        grid=(BATCH, N_TILES),
        in_specs=[
            pl.BlockSpec((None, ROW_TILE, w), lambda b, t: (b, t, OD_DQ // w)),
            pl.BlockSpec((None, ROW_TILE, w), lambda b, t: (b, t, OD_DG // w)),
            pl.BlockSpec((None, T_ALL, w), lambda b, t: (b, 0, OD_DK // w)),
            pl.BlockSpec((None, T_ALL, 2 * w), lambda b, t: (b, 0, OD_DV // (2 * w))),
            pl.BlockSpec((D_HEADS, PAIR_ENTRIES, GRID_W, LANES), lambda b, t: (0, 0, 0, 0)),
        ],
        out_specs=pl.BlockSpec((None, ROW_TILE, w), lambda b, t: (b, t, 0)),
        out_shape=jax.ShapeDtypeStruct((BATCH, T_ALL, w), BF16),
        compiler_params=_cparams(2),
    )(h_odd, h_odd, h_odd, h_odd, pair_tab)


def _outproj_kernel(ya_ref, yb_ref, w_ref, x_ref, mod_ref, o_ref):
    half = w_ref.shape[0] // 2
    y = jnp.dot(ya_ref[...], w_ref[0:half, :], preferred_element_type=F32)
    y = y + jnp.dot(yb_ref[...], w_ref[half:, :], preferred_element_type=F32)
    o_ref[...] = x_ref[...] + mod_ref[:, 2 * D_MODEL:3 * D_MODEL] * y


def _outproj(ya, yb, w_out, xs, mods, layer):
    const = lambda shape: pl.BlockSpec(shape, lambda b, t: (0,) * len(shape))
    half = w_out.shape[0] // 2
    return pl.pallas_call(
        _outproj_kernel,
        grid=(BATCH, N_TILES),
        in_specs=[
            pl.BlockSpec((None, ROW_TILE, half), lambda b, t: (b, t, 0)),
            pl.BlockSpec((None, ROW_TILE, half), lambda b, t: (b, t, 0)),
            const((2 * half, D_MODEL)),
            pl.BlockSpec((None, ROW_TILE, D_MODEL), lambda b, t: (b, t, 0)),
            _mod_spec(layer),
        ],
        out_specs=pl.BlockSpec((None, ROW_TILE, D_MODEL), lambda b, t: (b, t, 0)),
        out_shape=jax.ShapeDtypeStruct((BATCH, T_ALL, D_MODEL), F32),
        input_output_aliases={3: 0},
        compiler_params=_cparams(2),
    )(ya, yb, w_out.astype(BF16), xs, mods)


def _outproj_final_kernel(ya_ref, yb_ref, w_ref, x_ref, mod_ref, fg_ref, o_ref):
    half = w_ref.shape[0] // 2
    y = jnp.dot(ya_ref[...], w_ref[0:half, :], preferred_element_type=F32)
    y = y + jnp.dot(yb_ref[...], w_ref[half:, :], preferred_element_type=F32)
    x = x_ref[...] + mod_ref[:, 2 * D_MODEL:3 * D_MODEL] * y
    ms = jnp.mean(x * x, axis=-1, keepdims=True)
    o_ref[...] = x * lax.rsqrt(ms + EPS) * fg_ref[...]


def _outproj_final(ya, yb, w_out, xs, mods, layer, final_g):
    half = w_out.shape[0] // 2
    n_lat = SEQ // ROW_TILE
    return pl.pallas_call(
        _outproj_final_kernel,
        grid=(BATCH, n_lat),
        in_specs=[
            pl.BlockSpec((None, ROW_TILE, half), lambda b, t: (b, t + 1, 0)),
            pl.BlockSpec((None, ROW_TILE, half), lambda b, t: (b, t + 1, 0)),
            pl.BlockSpec((2 * half, D_MODEL), lambda b, t: (0, 0)),
            pl.BlockSpec((None, ROW_TILE, D_MODEL), lambda b, t: (b, t + 1, 0)),
            pl.BlockSpec((None, 1, 3 * D_MODEL), lambda b, t: (layer * MOD_ROWS + b, 0, 0)),
            pl.BlockSpec((1, D_MODEL), lambda b, t: (0, 0)),
        ],
        out_specs=pl.BlockSpec((None, ROW_TILE, D_MODEL), lambda b, t: (b, t, 0)),
        out_shape=jax.ShapeDtypeStruct((BATCH, SEQ, D_MODEL), F32),
        compiler_params=_cparams(2),
    )(ya, yb, w_out.astype(BF16), xs, mods, final_g.reshape(1, D_MODEL))


def kernel(x, c, ctx, c_ctx, mod_w, mod_b, norm_g, ev_w_in, ev_w_out, a_q_gain, a_k_gain, b_sink,
           od_w_in, od_w_out, c_dw_w, c_dw_b, c_ln_g, c_ln_b, d_rpb, final_g):
    xs = jnp.concatenate([ctx, x], axis=1)
    cond = jnp.concatenate(
        [c, c_ctx[None, :], jnp.zeros((MOD_ROWS - BATCH - 1, D_MODEL), F32)], axis=0)
    mods = _modulation_all(cond, mod_w, mod_b).reshape(DEPTH * MOD_ROWS, 1, 3 * D_MODEL)
    tables = _rope_tables()
    head_of_lane = np.arange(LANES) // HEAD_DIM
    bd = jnp.asarray((head_of_lane[:, None] == head_of_lane[None, :]).astype(np.float32) / HEAD_DIM, BF16)

    out = None
    for i in range(DEPTH):
        j = i // 2
        last = i == DEPTH - 1
        if i % 2 == 0:
            h_even = _inproj_even(xs, mods, i, norm_g[i], ev_w_in[j], a_q_gain[j], a_k_gain[j], tables, bd)
            ya = _attn_a(h_even)
            yb = _attn_b(h_even, b_sink[j])
            w_out = ev_w_out[j]
        else:
            h_odd, u = _inproj_odd(xs, mods, i, norm_g[i], od_w_in[j])
            ya = _conv_c(u, h_odd, c_dw_w[j], c_dw_b[j], c_ln_g[j], c_ln_b[j])
            yb = _attn_d(h_odd, _na_pair_table(d_rpb[j]))
            w_out = od_w_out[j]
        if last:
            out = _outproj_final(ya, yb, w_out, xs, mods, i, final_g)
        else:
            xs = _outproj(ya, yb, w_out, xs, mods, i)
    return out
```

```python
import functools

import numpy as np
import jax
import jax.numpy as jnp
from jax import lax
from jax.experimental import pallas as pl
from jax.experimental.pallas import tpu as pltpu

D_MODEL = 1024
BATCH = 8
SEQ = 2048
DEPTH = 4
GRID_W = 64
CTX_LEN = 256
HEAD_DIM = 64
ROPE_THETA = 10000.0
EPS = 1e-6
NEG_INF = -1e30
ATTN_SCALE = HEAD_DIM ** -0.5
A_HEADS = 8
A_KV_HEADS = 2
B_HEADS = 8
B_KV_HEADS = 2
B_WINDOW = 128
C_WIDTH = 512
C_CONV = 31
D_HEADS = 8
NA_KH = 8
NA_KW = 16
ROWS = SEQ // GRID_W

T_ALL = CTX_LEN + SEQ
ROW_TILE = 256
N_TILES = T_ALL // ROW_TILE
PROJ_TILE = 3 * ROW_TILE
PROJ_TILES = T_ALL // PROJ_TILE
LANES = 128
MXU_N = 256
MOD_ROWS = 16
CTX_MOD_ROW = BATCH
VMEM_LIMIT = 48 * 1024 * 1024
PROJ_VMEM_LIMIT = 56 * 1024 * 1024

EVEN_IN = 2560
ODD_IN = 3584
LOG2E = 1.4426950408889634
Q_SCALE = ATTN_SCALE * LOG2E
PIPELINE_DEPTH = 2
HALO = 16
BAND_B = 512
BAND_D = 768

F32 = jnp.float32
BF16 = jnp.bfloat16
NT_DIMS = (((1,), (1,)), ((), ()))


def _cparams(n_grid, vmem=VMEM_LIMIT):
    return pltpu.CompilerParams(dimension_semantics=("arbitrary",) * n_grid, vmem_limit_bytes=vmem)


def _silu(v):
    return v * jax.nn.sigmoid(v)


def _mod_kernel(cond_ref, w_ref, b_ref, o_ref):
    cnd = cond_ref[...]
    act = _silu(cnd).astype(BF16)
    o_ref[...] = jnp.dot(act, w_ref[...].astype(BF16), preferred_element_type=F32) + b_ref[...]


def _modulation_all(cond, mod_w, mod_b):
    n_col = 3
    return pl.pallas_call(
        _mod_kernel,
        grid=(DEPTH, n_col),
        in_specs=[
            pl.BlockSpec((MOD_ROWS, D_MODEL), lambda i, j: (0, 0)),
            pl.BlockSpec((None, D_MODEL, D_MODEL), lambda i, j: (i, 0, j)),
            pl.BlockSpec((None, 1, D_MODEL), lambda i, j: (i, 0, j)),
        ],
        out_specs=pl.BlockSpec((None, MOD_ROWS, D_MODEL), lambda i, j: (i, 0, j)),
        out_shape=jax.ShapeDtypeStruct((DEPTH, MOD_ROWS, 3 * D_MODEL), F32),
        compiler_params=_cparams(2),
    )(cond, mod_w, mod_b.reshape(DEPTH, 1, 3 * D_MODEL))


EVEN_OUT = 2816
EV_AQ, EV_AG, EV_BQ, EV_BG, EV_AV, EV_BV, EV_AK, EV_BK = 0, 512, 1024, 1536, 2048, 2304, 2560, 2688

ODD_OUT = 3072
OD_CG, OD_DQ, OD_DK, OD_DG, OD_DV = 0, 512, 1024, 1536, 2048


def _even_pieces():
    pieces = []
    for j in range(4):
        pieces.append((0 + LANES * j, "aq", EV_AQ + LANES * j))
    pieces.append((512, "ak", EV_AK))
    pieces.append((640, "v", EV_AV))
    for j in range(4):
        pieces.append((768 + LANES * j, "gate", EV_AG + LANES * j))
    for j in range(4):
        pieces.append((1280 + LANES * j, "bq", EV_BQ + LANES * j))
    pieces.append((1792, "bk", EV_BK))
    pieces.append((1920, "v", EV_BV))
    for j in range(4):
        pieces.append((2048 + LANES * j, "gate", EV_BG + LANES * j))
    return pieces


def _rope(v, cos, sin_even, sin_odd):
    nxt = pltpu.roll(v, LANES - 1, 1)
    prv = pltpu.roll(v, 1, 1)
    return v * cos + nxt * sin_even + prv * sin_odd


def _head_rms(v, gain, bd_ref):
    ms = jnp.dot((v * v).astype(BF16), bd_ref[...], preferred_element_type=F32)
    return v * lax.rsqrt(ms + EPS) * gain


def _even_epilogue(h, w_ref, qg_ref, kg_ref, cos_ref, se_ref, so_ref, bd_ref, o_ref):
    cos, s_even, s_odd = cos_ref[...], se_ref[...], so_ref[...]
    pieces = _even_pieces()
    ones = jnp.ones((PROJ_TILE, LANES), BF16)
    n_chunks = EVEN_IN // MXU_N

    def proj(c):
        return jnp.dot(h, w_ref[:, c * MXU_N:(c + 1) * MXU_N], preferred_element_type=F32)

    acc_next = proj(0)
    for c in range(n_chunks):
        acc = acc_next
        if c + 1 < n_chunks:
            acc_next = proj(c + 1)
        for half in range(MXU_N // LANES):
            src = c * MXU_N + half * LANES
            (_, kind, dst), = [p for p in pieces if p[0] == src]
            v = acc[:, half * LANES:(half + 1) * LANES]
            if kind == "aq":
                v = _rope(_head_rms(v, qg_ref[...], bd_ref), cos, s_even, s_odd) * Q_SCALE
            elif kind == "ak":
                v = _rope(_head_rms(v, kg_ref[...], bd_ref), cos, s_even, s_odd)
            elif kind == "bq":
                v = _rope(v, cos, s_even, s_odd) * Q_SCALE
            elif kind == "bk":
                v = _rope(v, cos, s_even, s_odd)
            elif kind == "gate":
                v = _silu(v)
            elif kind == "v":
                o_ref[:, dst + LANES:dst + 2 * LANES] = ones
            o_ref[:, dst:dst + LANES] = v.astype(BF16)


def _odd_epilogue(h, w_ref, o_ref, u_ref):
    def proj(col):
        return jnp.dot(h, w_ref[:, col:col + MXU_N], preferred_element_type=F32)

    ones = jnp.ones((PROJ_TILE, HEAD_DIM), F32)
    heads_per_chunk = MXU_N // HEAD_DIM
    held = {}

    def finish(kind, c, v):
        lanes = slice(c * MXU_N, (c + 1) * MXU_N)
        if kind == "val":
            held[c] = v
        elif kind == "glu":
            u_ref[:, lanes] = held.pop(c) * jax.nn.sigmoid(v)
        elif kind == "v":
            for i in range(heads_per_chunk):
                head = c * heads_per_chunk + i
                ext = jnp.concatenate([v[:, i * HEAD_DIM:(i + 1) * HEAD_DIM], ones], axis=1)
                o_ref[:, OD_DV + head * LANES:OD_DV + (head + 1) * LANES] = ext.astype(BF16)
        else:
            dst = {"cg": OD_CG, "q": OD_DQ, "k": OD_DK, "dg": OD_DG}[kind]
            if kind in ("cg", "dg"):
                v = _silu(v)
            elif kind == "q":
                v = v * Q_SCALE
            o_ref[:, dst + c * MXU_N:dst + (c + 1) * MXU_N] = v.astype(BF16)

    order = [("val", 0), ("glu", 1), ("cg", 2), ("q", 3), ("k", 4), ("v", 5), ("dg", 6)]
    steps = [(kind, c, g * C_WIDTH + c * MXU_N) for c in range(C_WIDTH // MXU_N) for kind, g in order[:2]]
    steps += [(kind, c, g * C_WIDTH + c * MXU_N) for kind, g in order[2:] for c in range(C_WIDTH // MXU_N)]
    acc_next = proj(steps[0][2])
    for i, (kind, c, _) in enumerate(steps):
        acc = acc_next
        if i + 1 < len(steps):
            acc_next = proj(steps[i + 1][2])
        finish(kind, c, acc)


def _per_slab(t, sample_row, ctx_row, fn):
    first_row = jnp.where(t == 0, ctx_row, sample_row)
    return jnp.concatenate([fn(slice(0, ROW_TILE), first_row), fn(slice(ROW_TILE, PROJ_TILE), sample_row)], axis=0)


def _make_proj_kernel(first, has_out, in_kind):
    def kern(*refs):
        it = iter(refs)
        t = pl.program_id(1)
        if first:
            ctx_ref, xa_ref, xb_ref, xc_ref = next(it), next(it), next(it), next(it)
            x = jnp.concatenate(
                [jnp.where(t == 0, ctx_ref[...], xa_ref[...]), xb_ref[...], xc_ref[...]], axis=0)
        else:
            x = next(it)[...]
        if has_out:
            ya_ref, yb_ref, wo_ref, cur_ref, cur_ctx_ref = (next(it) for _ in range(5))
        norm_ref, nxt_ref, nxt_ctx_ref, wi_ref = (next(it) for _ in range(4))
        if in_kind == "even":
            extra = [next(it) for _ in range(6)]
        if has_out:
            xo_ref = next(it)
        o_ref = next(it)
        if in_kind == "odd":
            u_ref = next(it)

        if has_out:
            half = wo_ref.shape[0] // 2
            y = jnp.dot(ya_ref[...], wo_ref[0:half, :], preferred_element_type=F32)
            y = y + jnp.dot(yb_ref[...], wo_ref[half:, :], preferred_element_type=F32)
            x = _per_slab(t, cur_ref[...], cur_ctx_ref[...],
                          lambda rows, m: x[rows] + m[:, 2 * D_MODEL:3 * D_MODEL] * y[rows])
            xo_ref[...] = x

        ms = jnp.mean(x * x, axis=-1, keepdims=True)
        normed = x * lax.rsqrt(ms + EPS) * norm_ref[...]
        h = _per_slab(t, nxt_ref[...], nxt_ctx_ref[...],
                      lambda rows, m: normed[rows] * (1.0 + m[:, D_MODEL:2 * D_MODEL]) + m[:, 0:D_MODEL])
        h = h.astype(BF16)
        if in_kind == "even":
            _even_epilogue(h, wi_ref, *extra, o_ref)
        else:
            _odd_epilogue(h, wi_ref, o_ref, u_ref)

    return kern


def _proj(in_kind, layer_in, norm_g, w_in, mods, *, x=None, ctx=None, xs=None, out=None, even_extra=None):
    first = xs is None
    has_out = out is not None
    const = lambda shape: pl.BlockSpec(shape, lambda b, t: (0,) * len(shape), pipeline_mode=pl.Buffered(1))
    tile = lambda width: pl.BlockSpec((None, PROJ_TILE, width), lambda b, t: (b, t, 0))
    mod_sample = lambda layer: pl.BlockSpec((None, 1, 3 * D_MODEL), lambda b, t: (layer * MOD_ROWS + b, 0, 0))
    mod_ctx = lambda layer: pl.BlockSpec((None, 1, 3 * D_MODEL),
                                         lambda b, t: (layer * MOD_ROWS + CTX_MOD_ROW, 0, 0))
    args, specs = [], []
    if first:
        per = PROJ_TILE // ROW_TILE
        blk = lambda off: pl.BlockSpec((None, ROW_TILE, D_MODEL),
                                       lambda b, t: (b, jnp.maximum(per * t + off, 0), 0))
        args += [ctx, x, x, x]
        specs += [pl.BlockSpec((None, CTX_LEN, D_MODEL), lambda b, t: (b, 0, 0)), blk(-1), blk(0), blk(1)]
    else:
        args.append(xs)
        specs.append(tile(D_MODEL))
    aliases = {}
    if has_out:
        ya, yb, w_out, layer_out = out
        half = w_out.shape[0] // 2
        args += [ya, yb, w_out.astype(BF16), mods, mods]
        specs += [tile(half), tile(half), const((2 * half, D_MODEL)), mod_sample(layer_out), mod_ctx(layer_out)]
        if not first:
            aliases = {0: 0}
    n_in = EVEN_IN if in_kind == "even" else ODD_IN
    args += [norm_g.reshape(1, D_MODEL), mods, mods, w_in.astype(BF16)]
    specs += [const((1, D_MODEL)), mod_sample(layer_in), mod_ctx(layer_in), const((D_MODEL, n_in))]
    if in_kind == "even":
        q_gain, k_gain, (cos, s_even, s_odd), bd = even_extra
        tab = pl.BlockSpec((PROJ_TILE, LANES), lambda b, t: (t, 0))
        args += [jnp.tile(q_gain, LANES // HEAD_DIM).reshape(1, LANES),
                 jnp.tile(k_gain, LANES // HEAD_DIM).reshape(1, LANES), cos, s_even, s_odd, bd]
        specs += [const((1, LANES)), const((1, LANES)), tab, tab, tab, const((LANES, LANES))]
    out_shapes, out_specs = [], []
    if has_out:
        out_shapes.append(jax.ShapeDtypeStruct((BATCH, T_ALL, D_MODEL), F32))
        out_specs.append(tile(D_MODEL))
    if in_kind == "even":
        out_shapes.append(jax.ShapeDtypeStruct((BATCH, T_ALL, EVEN_OUT), BF16))
        out_specs.append(tile(EVEN_OUT))
    else:
        out_shapes += [jax.ShapeDtypeStruct((BATCH, T_ALL, ODD_OUT), BF16),
                       jax.ShapeDtypeStruct((BATCH, T_ALL, C_WIDTH), F32)]
        out_specs += [tile(ODD_OUT), tile(C_WIDTH)]
    res = pl.pallas_call(
        _make_proj_kernel(first, has_out, in_kind),
        grid=(BATCH, PROJ_TILES),
        in_specs=specs,
        out_specs=out_specs,
        out_shape=out_shapes,
        input_output_aliases=aliases,
        compiler_params=_cparams(2, PROJ_VMEM_LIMIT),
    )(*args)
    res = list(res)
    stream = res.pop(0) if has_out else None
    return (stream, *res)


def _rope_tables():
    t = np.arange(SEQ)
    row = (t // GRID_W).astype(np.float64)
    col = (t % GRID_W).astype(np.float64)
    half = HEAD_DIM // 2
    freqs = (np.float32(ROPE_THETA) ** (-np.arange(0, half, 2, dtype=np.float32) / half)).astype(np.float64)
    ang = np.concatenate([row[:, None] * freqs, col[:, None] * freqs], axis=-1).astype(np.float32)
    cos = np.repeat(np.cos(ang.astype(np.float64)), 2, axis=-1)
    sin = np.repeat(np.sin(ang.astype(np.float64)), 2, axis=-1)
    lane_even = (np.arange(HEAD_DIM) % 2 == 0)
    s_even = np.where(lane_even, -sin, 0.0)
    s_odd = np.where(lane_even, 0.0, sin)

    def full(tab, ctx_val):
        tab = np.concatenate([np.full((CTX_LEN, HEAD_DIM), ctx_val), tab], axis=0).astype(np.float32)
        return jnp.asarray(np.tile(tab, (1, LANES // HEAD_DIM)))

    return full(cos, 1.0), full(s_even, 0.0), full(s_odd, 0.0)


def _softmax_pv(s_list, v_list, sink=None):
    m = s_list[0].max(axis=-1, keepdims=True)
    for s in s_list[1:]:
        m = jnp.maximum(m, s.max(axis=-1, keepdims=True))
    if sink is not None:
        m = jnp.maximum(m, sink)
    acc = None
    for s, v in zip(s_list, v_list):
        p = jnp.exp2(s - m).astype(BF16)
        part = jnp.dot(p, v, preferred_element_type=F32)
        acc = part if acc is None else acc + part
    extra = jnp.exp2(sink - m) if sink is not None else None
    return acc, extra


def _pipelined(n_heads, scores_fn, finish_fn):
    ahead = [scores_fn(h) for h in range(min(PIPELINE_DEPTH, n_heads))]
    for h in range(n_heads):
        if h + PIPELINE_DEPTH < n_heads:
            ahead.append(scores_fn(h + PIPELINE_DEPTH))
        finish_fn(h, ahead.pop(0))


class _PairStore:
    def __init__(self, o_ref, g_ref):
        self.o_ref, self.g_ref, self.held = o_ref, g_ref, None

    def put(self, h, o):
        if h % 2 == 0:
            self.held = o
            return
        lanes = slice((h - 1) * HEAD_DIM, (h + 1) * HEAD_DIM)
        pair = jnp.concatenate([self.held, o], axis=1)
        self.o_ref[:, lanes] = (pair * self.g_ref[:, lanes].astype(F32)).astype(BF16)


def _gqa_normalise(acc, extra, kv):
    den = acc[:, LANES:2 * LANES]
    if extra is not None:
        den = den + extra
    o = acc[:, 0:LANES] / den
    return o[:, kv * HEAD_DIM:(kv + 1) * HEAD_DIM]


def _attn_a_kernel(q_ref, g_ref, k_ref, v_ref, o_ref):
    group = A_HEADS // A_KV_HEADS

    def run(n_keys):
        out = _PairStore(o_ref, g_ref)

        def scores(h):
            kv = h // group
            q = q_ref[:, h * HEAD_DIM:(h + 1) * HEAD_DIM]
            k = k_ref[0:n_keys, kv * HEAD_DIM:(kv + 1) * HEAD_DIM]
            return [lax.dot_general(q, k, NT_DIMS, preferred_element_type=F32)]

        def finish(h, s_list):
            acc, extra = _softmax_pv(s_list, [v_ref[0:n_keys, :]])
            out.put(h, _gqa_normalise(acc, extra, h // group))

        _pipelined(A_HEADS, scores, finish)

    t = pl.program_id(1)
    pl.when(t == 0)(lambda: run(CTX_LEN))
    pl.when(t > 0)(lambda: run(T_ALL))


def _attn_a(h_even):
    w = A_HEADS * HEAD_DIM
    return pl.pallas_call(
        _attn_a_kernel,
        grid=(BATCH, N_TILES),
        in_specs=[
            pl.BlockSpec((None, ROW_TILE, w), lambda b, t: (b, t, EV_AQ // w)),
            pl.BlockSpec((None, ROW_TILE, w), lambda b, t: (b, t, EV_AG // w)),
            pl.BlockSpec((None, T_ALL, LANES), lambda b, t: (b, 0, EV_AK // LANES)),
            pl.BlockSpec((None, T_ALL, 2 * LANES), lambda b, t: (b, 0, EV_AV // (2 * LANES))),
        ],
        out_specs=pl.BlockSpec((None, ROW_TILE, w), lambda b, t: (b, t, 0)),
        out_shape=jax.ShapeDtypeStruct((BATCH, T_ALL, w), BF16),
        compiler_params=_cparams(2),
    )(h_even, h_even, h_even, h_even)


def _attn_b_kernel(sink_ref, q_ref, g_ref, k_ref, v_ref, o_ref):
    group = B_HEADS // B_KV_HEADS
    t = pl.program_id(1)

    def q_and_lanes(h):
        kv = h // group
        return q_ref[:, h * HEAD_DIM:(h + 1) * HEAD_DIM], slice(kv * HEAD_DIM, (kv + 1) * HEAD_DIM)

    def ctx_tile():
        out = _PairStore(o_ref, g_ref)

        def scores(h):
            q, lanes = q_and_lanes(h)
            return [lax.dot_general(q, k_ref[0:CTX_LEN, lanes], NT_DIMS, preferred_element_type=F32)]

        def finish(h, s_list):
            acc, extra = _softmax_pv(s_list, [v_ref[0:CTX_LEN, :]], sink=sink_ref[h] * LOG2E)
            out.put(h, _gqa_normalise(acc, extra, h // group))

        _pipelined(B_HEADS, scores, finish)

    def lat_tile():
        out = _PairStore(o_ref, g_ref)
        start = pl.multiple_of(jnp.clip(t * ROW_TILE - B_WINDOW, CTX_LEN, T_ALL - BAND_B), LANES)
        q_pos = t * ROW_TILE + lax.broadcasted_iota(jnp.int32, (ROW_TILE, BAND_B), 0)
        k_pos = start + lax.broadcasted_iota(jnp.int32, (ROW_TILE, BAND_B), 1)
        bias = jnp.where(jnp.abs(k_pos - q_pos) <= B_WINDOW, 0.0, NEG_INF).astype(F32)

        def scores(h):
            q, lanes = q_and_lanes(h)
            kb = k_ref[pl.ds(start, BAND_B), :][:, lanes]
            s_ctx = lax.dot_general(q, k_ref[0:CTX_LEN, lanes], NT_DIMS, preferred_element_type=F32)
            s_band = lax.dot_general(q, kb, NT_DIMS, preferred_element_type=F32) + bias
            return [s_ctx, s_band]

        def finish(h, s_list):
            v_list = [v_ref[0:CTX_LEN, :], v_ref[pl.ds(start, BAND_B), :]]
            acc, extra = _softmax_pv(s_list, v_list, sink=sink_ref[h] * LOG2E)
            out.put(h, _gqa_normalise(acc, extra, h // group))

        _pipelined(B_HEADS, scores, finish)

    pl.when(t == 0)(ctx_tile)
    pl.when(t > 0)(lat_tile)


def _attn_b(h_even, sink):
    w = B_HEADS * HEAD_DIM
    return pl.pallas_call(
        _attn_b_kernel,
        grid=(BATCH, N_TILES),
        in_specs=[
            pl.BlockSpec(memory_space=pltpu.SMEM),
            pl.BlockSpec((None, ROW_TILE, w), lambda b, t: (b, t, EV_BQ // w)),
            pl.BlockSpec((None, ROW_TILE, w), lambda b, t: (b, t, EV_BG // w)),
            pl.BlockSpec((None, T_ALL, LANES), lambda b, t: (b, 0, EV_BK // LANES)),
            pl.BlockSpec((None, T_ALL, 2 * LANES), lambda b, t: (b, 0, EV_BV // (2 * LANES))),
        ],
        out_specs=pl.BlockSpec((None, ROW_TILE, w), lambda b, t: (b, t, 0)),
        out_shape=jax.ShapeDtypeStruct((BATCH, T_ALL, w), BF16),
        compiler_params=_cparams(2),
    )(sink, h_even, h_even, h_even, h_even)


def _conv_kernel(first_tile, u_ref, lh_ref, rh_ref, sg_ref, w_ref, b_ref, lg_ref, lb_ref, o_ref):
    t = pl.program_id(1) + first_tile
    left_ok = (t >= 2).astype(F32)
    right_ok = jnp.logical_and(t >= 1, t <= N_TILES - 2).astype(F32)
    win = jnp.concatenate([lh_ref[...] * left_ok, u_ref[...], rh_ref[...] * right_ok], axis=0)
    n_win = ROW_TILE + 2 * HALO
    sub = 8
    acc = jnp.zeros((ROW_TILE, C_WIDTH), F32) + b_ref[...]
    for r in range(sub):
        shifted = win if r == 0 else pltpu.roll(win, n_win - r, 0)
        for k in range(C_CONV):
            off = HALO - C_CONV // 2 + k
            if off % sub != r:
                continue
            base = off - r
            acc = acc + shifted[base:base + ROW_TILE, :] * w_ref[k:k + 1, :]
    mu = jnp.mean(acc, axis=-1, keepdims=True)
    ctr = acc - mu
    var = jnp.mean(ctr * ctr, axis=-1, keepdims=True)
    y = ctr * lax.rsqrt(var + EPS) * lg_ref[...] + lb_ref[...]
    o_ref[...] = (_silu(y) * sg_ref[...].astype(F32)).astype(BF16)


def _conv_c(u, h_odd, dw_w, dw_b, ln_g, ln_b, lat_only):
    per_tile = ROW_TILE // HALO
    n_halo = T_ALL // HALO
    f = 1 if lat_only else 0
    const = lambda shape: pl.BlockSpec(shape, lambda b, t: (0,) * len(shape))
    return pl.pallas_call(
        functools.partial(_conv_kernel, f),
        grid=(BATCH, N_TILES - f),
        in_specs=[
            pl.BlockSpec((None, ROW_TILE, C_WIDTH), lambda b, t: (b, t + f, 0)),
            pl.BlockSpec((None, HALO, C_WIDTH), lambda b, t: (b, jnp.maximum((t + f) * per_tile - 1, 0), 0)),
            pl.BlockSpec((None, HALO, C_WIDTH),
                         lambda b, t: (b, jnp.minimum((t + f + 1) * per_tile, n_halo - 1), 0)),
            pl.BlockSpec((None, ROW_TILE, C_WIDTH), lambda b, t: (b, t + f, OD_CG // C_WIDTH)),
            const((C_CONV, C_WIDTH)),
            const((1, C_WIDTH)), const((1, C_WIDTH)), const((1, C_WIDTH)),
        ],
        out_specs=pl.BlockSpec((None, ROW_TILE, C_WIDTH), lambda b, t: (b, t, 0)),
        out_shape=jax.ShapeDtypeStruct((BATCH, T_ALL - f * CTX_LEN, C_WIDTH), BF16),
        compiler_params=_cparams(2),
    )(u, u, u, h_odd, dw_w, dw_b.reshape(1, C_WIDTH), ln_g.reshape(1, C_WIDTH), ln_b.reshape(1, C_WIDTH))


TILE_GRID_ROWS = ROW_TILE // GRID_W
BAND_GRID_ROWS = BAND_D // GRID_W
N_DR = 2 * NA_KH - 1
N_DC = 2 * NA_KW - 1
PAIR_ENTRIES = 17
PAIR_RIGHT_ONLY, PAIR_LEFT_ONLY = 15, 16


def _na_pair_table(rpb):
    period = 2 * GRID_W
    lead = GRID_W - NA_KW
    ext = jnp.pad(rpb.astype(F32) * LOG2E, ((0, 0), (0, 0), (lead, period - N_DC - lead)), mode="edge")
    skew = jnp.tile(ext, (1, 1, GRID_W))[..., :GRID_W * (period - 1)].reshape(D_HEADS, N_DR, GRID_W, period - 1)
    tab = skew[..., GRID_W - 1:period - 1]
    cols = np.arange(GRID_W)
    col_start = np.clip(cols - NA_KW // 2, 0, GRID_W - NA_KW)
    col_in = (cols[None, :] >= col_start[:, None]) & (cols[None, :] < col_start[:, None] + NA_KW)
    tab = jnp.where(jnp.asarray(col_in)[None, None], tab, NEG_INF)
    masked = jnp.full((D_HEADS, 1, GRID_W, GRID_W), NEG_INF, F32)
    interior_first_dr = NA_KH // 2 - 1
    interior_last_dr = interior_first_dr + NA_KH - 1
    left = jnp.concatenate([masked, tab[:, 0:N_DR - 1], masked, tab[:, interior_last_dr:interior_last_dr + 1]], axis=1)
    right = jnp.concatenate([masked, tab[:, 1:N_DR], tab[:, interior_first_dr:interior_first_dr + 1], masked], axis=1)
    return jnp.concatenate([left, right], axis=-1)


def _na_pair_entry(variant, a, p):
    if variant == "first":
        return 2 * p - a + NA_KH if p < NA_KH // 2 else 0
    if variant == "last":
        return 2 * p - a if p >= (BAND_GRID_ROWS - NA_KH) // 2 else 0
    left_ok = a <= 2 * p < a + NA_KH
    right_ok = a <= 2 * p + 1 < a + NA_KH
    if left_ok and right_ok:
        return 2 * p - a + NA_KH // 2
    if right_ok:
        return PAIR_RIGHT_ONLY
    if left_ok:
        return PAIR_LEFT_ONLY
    return 0


def _attn_d_kernel(first_tile, q_ref, g_ref, k_ref, v_ref, tab_ref, o_ref):
    t = pl.program_id(1) + first_tile

    def normalise(acc):
        return (acc / pltpu.roll(acc, HEAD_DIM, 1))[:, 0:HEAD_DIM]

    def ctx_tile():
        out = _PairStore(o_ref, g_ref)

        def scores(h):
            lanes = slice(h * HEAD_DIM, (h + 1) * HEAD_DIM)
            return [lax.dot_general(q_ref[:, lanes], k_ref[0:CTX_LEN, lanes], NT_DIMS, preferred_element_type=F32)]

        def finish(h, s_list):
            acc, _ = _softmax_pv(s_list, [v_ref[0:CTX_LEN, h * LANES:(h + 1) * LANES]])
            out.put(h, normalise(acc))

        _pipelined(D_HEADS, scores, finish)

    def lat_tile():
        out = _PairStore(o_ref, g_ref)
        start = pl.multiple_of(
            jnp.clip((t - 1) * ROW_TILE - (NA_KH // 2) * GRID_W, 0, SEQ - BAND_D) + CTX_LEN, LANES)
        is_first = t == 1
        is_last = t == N_TILES - 1

        def bias(h):
            rows = []
            for a in range(TILE_GRID_ROWS):
                blocks = []
                for p in range(BAND_GRID_ROWS // 2):
                    entry = jnp.where(is_first, _na_pair_entry("first", a, p),
                                      jnp.where(is_last, _na_pair_entry("last", a, p),
                                                _na_pair_entry("interior", a, p)))
                    blocks.append(tab_ref[h, entry])
                rows.append(jnp.concatenate(blocks, axis=1))
            return jnp.concatenate(rows, axis=0)

        def scores(h):
            lanes = slice(h * HEAD_DIM, (h + 1) * HEAD_DIM)
            q = q_ref[:, lanes]
            kb = k_ref[pl.ds(start, BAND_D), :][:, lanes]
            s_ctx = lax.dot_general(q, k_ref[0:CTX_LEN, lanes], NT_DIMS, preferred_element_type=F32)
            s_band = lax.dot_general(q, kb, NT_DIMS, preferred_element_type=F32) + bias(h)
            return [s_ctx, s_band]

        def finish(h, s_list):
            v_lanes = slice(h * LANES, (h + 1) * LANES)
            v_list = [v_ref[0:CTX_LEN, v_lanes], v_ref[pl.ds(start, BAND_D), :][:, v_lanes]]
            acc, _ = _softmax_pv(s_list, v_list)
            out.put(h, normalise(acc))

        _pipelined(D_HEADS, scores, finish)

    if first_tile == 0:
        pl.when(t == 0)(ctx_tile)
        pl.when(t > 0)(lat_tile)
    else:
        lat_tile()


def _attn_d(h_odd, pair_tab, lat_only):
    w = D_HEADS * HEAD_DIM
    f = 1 if lat_only else 0
    return pl.pallas_call(
        functools.partial(_attn_d_kernel, f),
        grid=(BATCH, N_TILES - f),
        in_specs=[
            pl.BlockSpec((None, ROW_TILE, w), lambda b, t: (b, t + f, OD_DQ // w)),
            pl.BlockSpec((None, ROW_TILE, w), lambda b, t: (b, t + f, OD_DG // w)),
            pl.BlockSpec((None, T_ALL, w), lambda b, t: (b, 0, OD_DK // w)),
            pl.BlockSpec((None, T_ALL, 2 * w), lambda b, t: (b, 0, OD_DV // (2 * w))),
            pl.BlockSpec((D_HEADS, PAIR_ENTRIES, GRID_W, LANES), lambda b, t: (0, 0, 0, 0)),
        ],
        out_specs=pl.BlockSpec((None, ROW_TILE, w), lambda b, t: (b, t, 0)),
        out_shape=jax.ShapeDtypeStruct((BATCH, T_ALL - f * CTX_LEN, w), BF16),
        compiler_params=_cparams(2),
    )(h_odd, h_odd, h_odd, h_odd, pair_tab)


FINAL_TILE = 2 * ROW_TILE


def _outproj_final_kernel(ya_ref, yb_ref, w_ref, x_lo_ref, x_hi_ref, mod_ref, fg_ref, o_ref):
    half = w_ref.shape[0] // 2
    y = jnp.dot(ya_ref[...], w_ref[0:half, :], preferred_element_type=F32)
    y = y + jnp.dot(yb_ref[...], w_ref[half:, :], preferred_element_type=F32)
    x = jnp.concatenate([x_lo_ref[...], x_hi_ref[...]], axis=0) + mod_ref[:, 2 * D_MODEL:3 * D_MODEL] * y
    ms = jnp.mean(x * x, axis=-1, keepdims=True)
    o_ref[...] = x * lax.rsqrt(ms + EPS) * fg_ref[...]


def _outproj_final(ya, yb, w_out, xs, mods, layer, final_g):
    half = w_out.shape[0] // 2
    per = FINAL_TILE // ROW_TILE
    x_blk = lambda off: pl.BlockSpec((None, ROW_TILE, D_MODEL), lambda b, t: (b, per * t + 1 + off, 0))
    return pl.pallas_call(
        _outproj_final_kernel,
        grid=(BATCH, SEQ // FINAL_TILE),
        in_specs=[
            pl.BlockSpec((None, FINAL_TILE, half), lambda b, t: (b, t, 0)),
            pl.BlockSpec((None, FINAL_TILE, half), lambda b, t: (b, t, 0)),
            pl.BlockSpec((2 * half, D_MODEL), lambda b, t: (0, 0)),
            x_blk(0), x_blk(1),
            pl.BlockSpec((None, 1, 3 * D_MODEL), lambda b, t: (layer * MOD_ROWS + b, 0, 0)),
            pl.BlockSpec((1, D_MODEL), lambda b, t: (0, 0)),
        ],
        out_specs=pl.BlockSpec((None, FINAL_TILE, D_MODEL), lambda b, t: (b, t, 0)),
        out_shape=jax.ShapeDtypeStruct((BATCH, SEQ, D_MODEL), F32),
        compiler_params=_cparams(2),
    )(ya, yb, w_out.astype(BF16), xs, xs, mods, final_g.reshape(1, D_MODEL))


def kernel(x, c, ctx, c_ctx, mod_w, mod_b, norm_g, ev_w_in, ev_w_out, a_q_gain, a_k_gain, b_sink,
           od_w_in, od_w_out, c_dw_w, c_dw_b, c_ln_g, c_ln_b, d_rpb, final_g):
    cond = jnp.concatenate(
        [c, c_ctx[None, :], jnp.zeros((MOD_ROWS - BATCH - 1, D_MODEL), F32)], axis=0)
    mods = _modulation_all(cond, mod_w, mod_b).reshape(DEPTH * MOD_ROWS, 1, 3 * D_MODEL)
    tables = _rope_tables()
    head_of_lane = np.arange(LANES) // HEAD_DIM
    bd = jnp.asarray((head_of_lane[:, None] == head_of_lane[None, :]).astype(np.float32) / HEAD_DIM, BF16)

    xs = None
    pending = None
    for i in range(DEPTH):
        j = i // 2
        src = dict(x=x, ctx=ctx) if xs is None else dict(xs=xs)
        if i % 2 == 0:
            extra = (a_q_gain[j], a_k_gain[j], tables, bd)
            stream, h_even = _proj("even", i, norm_g[i], ev_w_in[j], mods, out=pending, even_extra=extra, **src)
            pending = (_attn_a(h_even), _attn_b(h_even, b_sink[j]), ev_w_out[j], i)
        else:
            stream, h_odd, u = _proj("odd", i, norm_g[i], od_w_in[j], mods, out=pending, **src)
            last = i == DEPTH - 1
            ya = _conv_c(u, h_odd, c_dw_w[j], c_dw_b[j], c_ln_g[j], c_ln_b[j], lat_only=last)
            pending = (ya, _attn_d(h_odd, _na_pair_table(d_rpb[j]), lat_only=last), od_w_out[j], i)
        if stream is not None:
            xs = stream
    ya, yb, w_out, layer = pending
    return _outproj_final(ya, yb, w_out, xs, mods, layer, final_g)
```

```python
import functools

import numpy as np
import jax
import jax.numpy as jnp
from jax import lax
from jax.experimental import pallas as pl
from jax.experimental.pallas import tpu as pltpu

D_MODEL = 1024
BATCH = 8
SEQ = 2048
DEPTH = 4
GRID_W = 64
CTX_LEN = 256
HEAD_DIM = 64
ROPE_THETA = 10000.0
EPS = 1e-6
NEG_INF = -1e30
ATTN_SCALE = HEAD_DIM ** -0.5
A_HEADS = 8
A_KV_HEADS = 2
B_HEADS = 8
B_KV_HEADS = 2
B_WINDOW = 128
C_WIDTH = 512
C_CONV = 31
D_HEADS = 8
NA_KH = 8
NA_KW = 16
ROWS = SEQ // GRID_W

T_ALL = CTX_LEN + SEQ
ROW_TILE = 256
N_TILES = T_ALL // ROW_TILE
PROJ_TILE = 3 * ROW_TILE
PROJ_TILES = T_ALL // PROJ_TILE
LANES = 128
MXU_N = 256
MOD_ROWS = 16
CTX_MOD_ROW = BATCH
VMEM_LIMIT = 48 * 1024 * 1024
PROJ_VMEM_LIMIT = 56 * 1024 * 1024

EVEN_IN = 2560
ODD_IN = 3584
LOG2E = 1.4426950408889634
Q_SCALE = ATTN_SCALE * LOG2E
STACK = 1
CTX_PIPELINE_DEPTH = 8
PIPELINE_DEPTH = 2
HALO = 16
BAND_B = 512
BAND_D = 768

F32 = jnp.float32
BF16 = jnp.bfloat16
NT_DIMS = (((1,), (1,)), ((), ()))


def _cparams(n_grid, vmem=VMEM_LIMIT):
    return pltpu.CompilerParams(dimension_semantics=("arbitrary",) * n_grid, vmem_limit_bytes=vmem)


def _silu(v):
    return v * jax.nn.sigmoid(v)


def _mod_kernel(cond_ref, w_ref, b_ref, o_ref):
    cnd = cond_ref[...]
    act = _silu(cnd).astype(BF16)
    o_ref[...] = jnp.dot(act, w_ref[...].astype(BF16), preferred_element_type=F32) + b_ref[...]


def _modulation_all(cond, mod_w, mod_b):
    n_col = 3
    return pl.pallas_call(
        _mod_kernel,
        grid=(DEPTH, n_col),
        in_specs=[
            pl.BlockSpec((MOD_ROWS, D_MODEL), lambda i, j: (0, 0)),
            pl.BlockSpec((None, D_MODEL, D_MODEL), lambda i, j: (i, 0, j)),
            pl.BlockSpec((None, 1, D_MODEL), lambda i, j: (i, 0, j)),
        ],
        out_specs=pl.BlockSpec((None, MOD_ROWS, D_MODEL), lambda i, j: (i, 0, j)),
        out_shape=jax.ShapeDtypeStruct((DEPTH, MOD_ROWS, 3 * D_MODEL), F32),
        compiler_params=_cparams(2),
    )(cond, mod_w, mod_b.reshape(DEPTH, 1, 3 * D_MODEL))


EVEN_OUT = 2816
EV_AQ, EV_AG, EV_BQ, EV_BG, EV_AV, EV_BV, EV_AK, EV_BK = 0, 512, 1024, 1536, 2048, 2304, 2560, 2688

ODD_OUT = 3072
OD_CG, OD_DQ, OD_DK, OD_DG, OD_DV = 0, 512, 1024, 1536, 2048


def _even_pieces():
    pieces = []
    for j in range(4):
        pieces.append((0 + LANES * j, "aq", EV_AQ + LANES * j))
    pieces.append((512, "ak", EV_AK))
    pieces.append((640, "v", EV_AV))
    for j in range(4):
        pieces.append((768 + LANES * j, "gate", EV_AG + LANES * j))
    for j in range(4):
        pieces.append((1280 + LANES * j, "bq", EV_BQ + LANES * j))
    pieces.append((1792, "bk", EV_BK))
    pieces.append((1920, "v", EV_BV))
    for j in range(4):
        pieces.append((2048 + LANES * j, "gate", EV_BG + LANES * j))
    return pieces


def _rope(v, cos, sin_even, sin_odd):
    nxt = pltpu.roll(v, LANES - 1, 1)
    prv = pltpu.roll(v, 1, 1)
    return v * cos + nxt * sin_even + prv * sin_odd


def _head_rms(v, gain, bd_ref):
    ms = jnp.dot((v * v).astype(BF16), bd_ref[...], preferred_element_type=F32)
    return v * lax.rsqrt(ms + EPS) * gain


def _even_epilogue(h_slabs, w_ref, qg_ref, kg_ref, cos_ref, se_ref, so_ref, bd_ref, o_ref):
    cos, s_even, s_odd = cos_ref[...], se_ref[...], so_ref[...]
    pieces = _even_pieces()
    ones = jnp.ones((PROJ_TILE, LANES), BF16)
    n_chunks = EVEN_IN // MXU_N

    h = jnp.concatenate(h_slabs, axis=0)

    def proj(c):
        return jnp.dot(h, w_ref[:, c * MXU_N:(c + 1) * MXU_N], preferred_element_type=F32)

    acc_next = proj(0)
    for c in range(n_chunks):
        acc = acc_next
        if c + 1 < n_chunks:
            acc_next = proj(c + 1)
        for half in range(MXU_N // LANES):
            src = c * MXU_N + half * LANES
            (_, kind, dst), = [p for p in pieces if p[0] == src]
            v = acc[:, half * LANES:(half + 1) * LANES]
            if kind == "aq":
                v = _rope(_head_rms(v, qg_ref[...], bd_ref), cos, s_even, s_odd) * Q_SCALE
            elif kind == "ak":
                v = _rope(_head_rms(v, kg_ref[...], bd_ref), cos, s_even, s_odd)
            elif kind == "bq":
                v = _rope(v, cos, s_even, s_odd) * Q_SCALE
            elif kind == "bk":
                v = _rope(v, cos, s_even, s_odd)
            elif kind == "gate":
                v = _silu(v)
            elif kind == "v":
                o_ref[:, dst + LANES:dst + 2 * LANES] = ones
            o_ref[:, dst:dst + LANES] = v.astype(BF16)


def _odd_epilogue(h_slabs, w_ref, o_ref, u_ref):
    h = jnp.concatenate(h_slabs, axis=0)

    def proj(col):
        return jnp.dot(h, w_ref[:, col:col + MXU_N], preferred_element_type=F32)

    ones = jnp.ones((PROJ_TILE, HEAD_DIM), F32)
    heads_per_chunk = MXU_N // HEAD_DIM
    held = {}

    def finish(kind, c, v):
        lanes = slice(c * MXU_N, (c + 1) * MXU_N)
        if kind == "val":
            held[c] = v
        elif kind == "glu":
            u_ref[:, lanes] = held.pop(c) * jax.nn.sigmoid(v)
        elif kind == "v":
            for i in range(heads_per_chunk):
                head = c * heads_per_chunk + i
                ext = jnp.concatenate([v[:, i * HEAD_DIM:(i + 1) * HEAD_DIM], ones], axis=1)
                o_ref[:, OD_DV + head * LANES:OD_DV + (head + 1) * LANES] = ext.astype(BF16)
        else:
            dst = {"cg": OD_CG, "q": OD_DQ, "k": OD_DK, "dg": OD_DG}[kind]
            if kind in ("cg", "dg"):
                v = _silu(v)
            elif kind == "q":
                v = v * Q_SCALE
            o_ref[:, dst + c * MXU_N:dst + (c + 1) * MXU_N] = v.astype(BF16)

    order = [("val", 0), ("glu", 1), ("cg", 2), ("q", 3), ("k", 4), ("v", 5), ("dg", 6)]
    steps = [(kind, c, g * C_WIDTH + c * MXU_N) for c in range(C_WIDTH // MXU_N) for kind, g in order[:2]]
    steps += [(kind, c, g * C_WIDTH + c * MXU_N) for kind, g in order[2:] for c in range(C_WIDTH // MXU_N)]
    acc_next = proj(steps[0][2])
    for i, (kind, c, _) in enumerate(steps):
        acc = acc_next
        if i + 1 < len(steps):
            acc_next = proj(steps[i + 1][2])
        finish(kind, c, acc)


def _make_proj_kernel(first, has_out, in_kind):
    def kern(*refs):
        it = iter(refs)
        t = pl.program_id(1)
        if first:
            ctx_ref, xa_ref, xb_ref, xc_ref = next(it), next(it), next(it), next(it)
        else:
            x_ref = next(it)
        if has_out:
            ya_ref, yb_ref, wo_ref, cur_ref, cur_ctx_ref = (next(it) for _ in range(5))
        norm_ref, nxt_ref, nxt_ctx_ref, wi_ref = (next(it) for _ in range(4))
        if in_kind == "even":
            extra = [next(it) for _ in range(6)]
        if has_out:
            xo_ref = next(it)
        o_ref = next(it)
        if in_kind == "odd":
            u_ref = next(it)

        h_slabs = []
        for s in range(PROJ_TILE // ROW_TILE):
            rows = slice(s * ROW_TILE, (s + 1) * ROW_TILE)
            is_ctx = jnp.logical_and(t == 0, s == 0)
            if first:
                x = jnp.where(t == 0, ctx_ref[...], xa_ref[...]) if s == 0 else (xb_ref, xc_ref)[s - 1][...]
            else:
                x = x_ref[rows, :]
            if has_out:
                half = wo_ref.shape[0] // 2
                y = jnp.dot(ya_ref[rows, :], wo_ref[0:half, :], preferred_element_type=F32)
                y = y + jnp.dot(yb_ref[rows, :], wo_ref[half:, :], preferred_element_type=F32)
                cur = jnp.where(is_ctx, cur_ctx_ref[...], cur_ref[...]) if s == 0 else cur_ref[...]
                x = x + cur[:, 2 * D_MODEL:3 * D_MODEL] * y
                xo_ref[rows, :] = x
            nxt = jnp.where(is_ctx, nxt_ctx_ref[...], nxt_ref[...]) if s == 0 else nxt_ref[...]
            ms = jnp.mean(x * x, axis=-1, keepdims=True)
            normed = x * lax.rsqrt(ms + EPS) * norm_ref[...]
            h_slabs.append((normed * (1.0 + nxt[:, D_MODEL:2 * D_MODEL]) + nxt[:, 0:D_MODEL]).astype(BF16))
        if in_kind == "even":
            _even_epilogue(h_slabs, wi_ref, *extra, o_ref)
        else:
            _odd_epilogue(h_slabs, wi_ref, o_ref, u_ref)

    return kern


def _proj(in_kind, layer_in, norm_g, w_in, mods, *, x=None, ctx=None, xs=None, out=None, even_extra=None):
    first = xs is None
    has_out = out is not None
    const = lambda shape: pl.BlockSpec(shape, lambda b, t: (0,) * len(shape), pipeline_mode=pl.Buffered(1))
    tile = lambda width: pl.BlockSpec((None, PROJ_TILE, width), lambda b, t: (b, t, 0))
    mod_sample = lambda layer: pl.BlockSpec((None, 1, 3 * D_MODEL), lambda b, t: (layer * MOD_ROWS + b, 0, 0))
    mod_ctx = lambda layer: pl.BlockSpec((None, 1, 3 * D_MODEL),
                                         lambda b, t: (layer * MOD_ROWS + CTX_MOD_ROW, 0, 0))
    args, specs = [], []
    if first:
        per = PROJ_TILE // ROW_TILE
        blk = lambda off: pl.BlockSpec((None, ROW_TILE, D_MODEL),
                                       lambda b, t: (b, jnp.maximum(per * t + off, 0), 0))
        args += [ctx, x, x, x]
        specs += [pl.BlockSpec((None, CTX_LEN, D_MODEL), lambda b, t: (b, 0, 0)), blk(-1), blk(0), blk(1)]
    else:
        args.append(xs)
        specs.append(tile(D_MODEL))
    aliases = {}
    if has_out:
        ya, yb, w_out, layer_out = out
        half = w_out.shape[0] // 2
        args += [ya, yb, w_out.astype(BF16), mods, mods]
        specs += [tile(half), tile(half), const((2 * half, D_MODEL)), mod_sample(layer_out), mod_ctx(layer_out)]
        if not first:
            aliases = {0: 0}
    n_in = EVEN_IN if in_kind == "even" else ODD_IN
    args += [norm_g.reshape(1, D_MODEL), mods, mods, w_in.astype(BF16)]
    specs += [const((1, D_MODEL)), mod_sample(layer_in), mod_ctx(layer_in), const((D_MODEL, n_in))]
    if in_kind == "even":
        q_gain, k_gain, (cos, s_even, s_odd), bd = even_extra
        tab = pl.BlockSpec((PROJ_TILE, LANES), lambda b, t: (t, 0))
        args += [jnp.tile(q_gain, LANES // HEAD_DIM).reshape(1, LANES),
                 jnp.tile(k_gain, LANES // HEAD_DIM).reshape(1, LANES), cos, s_even, s_odd, bd]
        specs += [const((1, LANES)), const((1, LANES)), tab, tab, tab, const((LANES, LANES))]
    out_shapes, out_specs = [], []
    if has_out:
        out_shapes.append(jax.ShapeDtypeStruct((BATCH, T_ALL, D_MODEL), F32))
        out_specs.append(tile(D_MODEL))
    if in_kind == "even":
        out_shapes.append(jax.ShapeDtypeStruct((BATCH, T_ALL, EVEN_OUT), BF16))
        out_specs.append(tile(EVEN_OUT))
    else:
        out_shapes += [jax.ShapeDtypeStruct((BATCH, T_ALL, ODD_OUT), BF16),
                       jax.ShapeDtypeStruct((BATCH, T_ALL, C_WIDTH), F32)]
        out_specs += [tile(ODD_OUT), tile(C_WIDTH)]
    res = pl.pallas_call(
        _make_proj_kernel(first, has_out, in_kind),
        grid=(BATCH, PROJ_TILES),
        in_specs=specs,
        out_specs=out_specs,
        out_shape=out_shapes,
        input_output_aliases=aliases,
        compiler_params=_cparams(2, PROJ_VMEM_LIMIT),
    )(*args)
    res = list(res)
    stream = res.pop(0) if has_out else None
    return (stream, *res)


def _rope_tables():
    t = np.arange(SEQ)
    row = (t // GRID_W).astype(np.float64)
    col = (t % GRID_W).astype(np.float64)
    half = HEAD_DIM // 2
    freqs = (np.float32(ROPE_THETA) ** (-np.arange(0, half, 2, dtype=np.float32) / half)).astype(np.float64)
    ang = np.concatenate([row[:, None] * freqs, col[:, None] * freqs], axis=-1).astype(np.float32)
    cos = np.repeat(np.cos(ang.astype(np.float64)), 2, axis=-1)
    sin = np.repeat(np.sin(ang.astype(np.float64)), 2, axis=-1)
    lane_even = (np.arange(HEAD_DIM) % 2 == 0)
    s_even = np.where(lane_even, -sin, 0.0)
    s_odd = np.where(lane_even, 0.0, sin)

    def full(tab, ctx_val):
        tab = np.concatenate([np.full((CTX_LEN, HEAD_DIM), ctx_val), tab], axis=0).astype(np.float32)
        return jnp.asarray(np.tile(tab, (1, LANES // HEAD_DIM)))

    return full(cos, 1.0), full(s_even, 0.0), full(s_odd, 0.0)


def _softmax_pv(s_list, v_list, sink=None):
    m = s_list[0].max(axis=-1, keepdims=True)
    for s in s_list[1:]:
        m = jnp.maximum(m, s.max(axis=-1, keepdims=True))
    if sink is not None:
        m = jnp.maximum(m, sink)
    acc = None
    for s, v in zip(s_list, v_list):
        p = jnp.exp2(s - m).astype(BF16)
        part = jnp.dot(p, v, preferred_element_type=F32)
        acc = part if acc is None else acc + part
    extra = jnp.exp2(sink - m) if sink is not None else None
    return acc, extra


def _pipelined(n_heads, scores_fn, finish_fn, depth=PIPELINE_DEPTH):
    ahead = [scores_fn(h) for h in range(min(depth, n_heads))]
    for h in range(n_heads):
        if h + depth < n_heads:
            ahead.append(scores_fn(h + depth))
        finish_fn(h, ahead.pop(0))


class _PairStore:
    def __init__(self, o_ref, g_ref):
        self.o_ref, self.g_ref, self.held = o_ref, g_ref, None

    def put(self, h, o):
        if h % 2 == 0:
            self.held = o
            return
        lanes = slice((h - 1) * HEAD_DIM, (h + 1) * HEAD_DIM)
        pair = jnp.concatenate([self.held, o], axis=1)
        self.o_ref[:, lanes] = (pair * self.g_ref[:, lanes].astype(F32)).astype(BF16)


def _gqa_normalise(acc, extra, kv):
    den = acc[:, LANES:2 * LANES]
    if extra is not None:
        den = den + extra
    o = acc[:, 0:LANES] / den
    return o[:, kv * HEAD_DIM:(kv + 1) * HEAD_DIM]


def _attn_a_kernel(q_ref, g_ref, k_ref, v_ref, o_ref):
    group = A_HEADS // A_KV_HEADS

    def run(n_keys):
        out = _PairStore(o_ref, g_ref)

        def scores(h):
            kv = h // group
            q = q_ref[:, h * HEAD_DIM:(h + 1) * HEAD_DIM]
            k = k_ref[0:n_keys, kv * HEAD_DIM:(kv + 1) * HEAD_DIM]
            return [lax.dot_general(q, k, NT_DIMS, preferred_element_type=F32)]

        def finish(h, s_list):
            acc, extra = _softmax_pv(s_list, [v_ref[0:n_keys, :]])
            out.put(h, _gqa_normalise(acc, extra, h // group))

        _pipelined(A_HEADS, scores, finish, depth=CTX_PIPELINE_DEPTH if n_keys == CTX_LEN else PIPELINE_DEPTH)

    t = pl.program_id(1)
    pl.when(t == 0)(lambda: run(CTX_LEN))
    pl.when(t > 0)(lambda: run(T_ALL))


def _attn_a(h_even):
    w = A_HEADS * HEAD_DIM
    return pl.pallas_call(
        _attn_a_kernel,
        grid=(BATCH, N_TILES),
        in_specs=[
            pl.BlockSpec((None, ROW_TILE, w), lambda b, t: (b, t, EV_AQ // w)),
            pl.BlockSpec((None, ROW_TILE, w), lambda b, t: (b, t, EV_AG // w)),
            pl.BlockSpec((None, T_ALL, LANES), lambda b, t: (b, 0, EV_AK // LANES)),
            pl.BlockSpec((None, T_ALL, 2 * LANES), lambda b, t: (b, 0, EV_AV // (2 * LANES))),
        ],
        out_specs=pl.BlockSpec((None, ROW_TILE, w), lambda b, t: (b, t, 0)),
        out_shape=jax.ShapeDtypeStruct((BATCH, T_ALL, w), BF16),
        compiler_params=_cparams(2),
    )(h_even, h_even, h_even, h_even)


def _attn_b_kernel(sink_ref, q_ref, g_ref, k_ref, v_ref, o_ref):
    group = B_HEADS // B_KV_HEADS
    t = pl.program_id(1)

    def run(row_spans, band_bias):
        out = _PairStore(o_ref, g_ref)
        k_all = jnp.concatenate([k_ref[pl.ds(s, n), :] for s, n in row_spans], axis=0)
        v_all = jnp.concatenate([v_ref[pl.ds(s, n), :] for s, n in row_spans], axis=0)
        k_kv = [k_all[:, kv * HEAD_DIM:(kv + 1) * HEAD_DIM] for kv in range(B_KV_HEADS)]

        def heads_of(u):
            return list(range(u * STACK, (u + 1) * STACK))

        def scores(u):
            heads = heads_of(u)
            q = jnp.concatenate([q_ref[:, h * HEAD_DIM:(h + 1) * HEAD_DIM] for h in heads], axis=0)
            s = lax.dot_general(q, k_kv[heads[0] // group], NT_DIMS, preferred_element_type=F32)
            if band_bias is None:
                return s
            slabs = []
            for i in range(STACK):
                slab = s[i * ROW_TILE:(i + 1) * ROW_TILE]
                slabs.append(jnp.concatenate([slab[:, 0:CTX_LEN], slab[:, CTX_LEN:] + band_bias], axis=1))
            return jnp.concatenate(slabs, axis=0)

        def finish(u, s):
            heads = heads_of(u)
            kv = heads[0] // group
            sink = jnp.concatenate([jnp.full((ROW_TILE, 1), sink_ref[h] * LOG2E, F32) for h in heads], axis=0)
            m = jnp.maximum(s.max(axis=-1, keepdims=True), sink)
            p = jnp.exp2(s - m).astype(BF16)
            acc = jnp.dot(p, v_all, preferred_element_type=F32)
            den = acc[:, LANES:2 * LANES] + jnp.exp2(sink - m)
            o = (acc[:, 0:LANES] / den)[:, kv * HEAD_DIM:(kv + 1) * HEAD_DIM]
            for i, h in enumerate(heads):
                out.put(h, o[i * ROW_TILE:(i + 1) * ROW_TILE])

        _pipelined(B_HEADS // STACK, scores, finish,
                   depth=CTX_PIPELINE_DEPTH if band_bias is None else PIPELINE_DEPTH)

    def lat_tile():
        start = pl.multiple_of(jnp.clip(t * ROW_TILE - B_WINDOW, CTX_LEN, T_ALL - BAND_B), LANES)
        q_pos = t * ROW_TILE + lax.broadcasted_iota(jnp.int32, (ROW_TILE, BAND_B), 0)
        k_pos = start + lax.broadcasted_iota(jnp.int32, (ROW_TILE, BAND_B), 1)
        bias = jnp.where(jnp.abs(k_pos - q_pos) <= B_WINDOW, 0.0, NEG_INF).astype(F32)
        run([(0, CTX_LEN), (start, BAND_B)], bias)

    pl.when(t == 0)(lambda: run([(0, CTX_LEN)], None))
    pl.when(t > 0)(lat_tile)


def _attn_b(h_even, sink):
    w = B_HEADS * HEAD_DIM
    return pl.pallas_call(
        _attn_b_kernel,
        grid=(BATCH, N_TILES),
        in_specs=[
            pl.BlockSpec(memory_space=pltpu.SMEM),
            pl.BlockSpec((None, ROW_TILE, w), lambda b, t: (b, t, EV_BQ // w)),
            pl.BlockSpec((None, ROW_TILE, w), lambda b, t: (b, t, EV_BG // w)),
            pl.BlockSpec((None, T_ALL, LANES), lambda b, t: (b, 0, EV_BK // LANES)),
            pl.BlockSpec((None, T_ALL, 2 * LANES), lambda b, t: (b, 0, EV_BV // (2 * LANES))),
        ],
        out_specs=pl.BlockSpec((None, ROW_TILE, w), lambda b, t: (b, t, 0)),
        out_shape=jax.ShapeDtypeStruct((BATCH, T_ALL, w), BF16),
        compiler_params=_cparams(2),
    )(sink, h_even, h_even, h_even, h_even)


CONV_GROUPS = 8


def _conv_kernel(first_tile, u_ref, lh_ref, rh_ref, sg_ref, w_ref, b_ref, lg_ref, lb_ref, o_ref):
    t = pl.program_id(1) + first_tile
    left_ok = (t >= 2).astype(F32)
    right_ok = jnp.logical_and(t >= 1, t <= N_TILES - 2).astype(F32)
    win = jnp.concatenate([lh_ref[...] * left_ok, u_ref[...], rh_ref[...] * right_ok], axis=0)
    n_win = ROW_TILE + 2 * HALO
    acc = jnp.zeros((ROW_TILE, C_WIDTH), F32) + b_ref[...]
    for r in range(CONV_GROUPS):
        shifted = win if r == 0 else pltpu.roll(win, n_win - r, 0)
        for k in range(C_CONV):
            off = HALO - C_CONV // 2 + k
            if off % CONV_GROUPS == r:
                base = off - r
                acc = acc + shifted[base:base + ROW_TILE, :] * w_ref[k:k + 1, :]
    mu = jnp.mean(acc, axis=-1, keepdims=True)
    ctr = acc - mu
    var = jnp.mean(ctr * ctr, axis=-1, keepdims=True)
    y = ctr * lax.rsqrt(var + EPS) * lg_ref[...] + lb_ref[...]
    o_ref[...] = (_silu(y) * sg_ref[...].astype(F32)).astype(BF16)


def _conv_c(u, h_odd, dw_w, dw_b, ln_g, ln_b, lat_only):
    per_tile = ROW_TILE // HALO
    n_halo = T_ALL // HALO
    f = 1 if lat_only else 0
    const = lambda shape: pl.BlockSpec(shape, lambda b, t: (0,) * len(shape))
    return pl.pallas_call(
        functools.partial(_conv_kernel, f),
        grid=(BATCH, N_TILES - f),
        in_specs=[
            pl.BlockSpec((None, ROW_TILE, C_WIDTH), lambda b, t: (b, t + f, 0)),
            pl.BlockSpec((None, HALO, C_WIDTH), lambda b, t: (b, jnp.maximum((t + f) * per_tile - 1, 0), 0)),
            pl.BlockSpec((None, HALO, C_WIDTH),
                         lambda b, t: (b, jnp.minimum((t + f + 1) * per_tile, n_halo - 1), 0)),
            pl.BlockSpec((None, ROW_TILE, C_WIDTH), lambda b, t: (b, t + f, OD_CG // C_WIDTH)),
            const((C_CONV, C_WIDTH)),
            const((1, C_WIDTH)), const((1, C_WIDTH)), const((1, C_WIDTH)),
        ],
        out_specs=pl.BlockSpec((None, ROW_TILE, C_WIDTH), lambda b, t: (b, t, 0)),
        out_shape=jax.ShapeDtypeStruct((BATCH, T_ALL - f * CTX_LEN, C_WIDTH), BF16),
        compiler_params=_cparams(2),
    )(u, u, u, h_odd, dw_w, dw_b.reshape(1, C_WIDTH), ln_g.reshape(1, C_WIDTH), ln_b.reshape(1, C_WIDTH))


TILE_GRID_ROWS = ROW_TILE // GRID_W
BAND_GRID_ROWS = BAND_D // GRID_W
N_DR = 2 * NA_KH - 1
N_DC = 2 * NA_KW - 1
PAIR_ENTRIES = 17
PAIR_RIGHT_ONLY, PAIR_LEFT_ONLY = 15, 16


INTERIOR_FIRST_DR = NA_KH // 2 - 1
INTERIOR_LAST_DR = INTERIOR_FIRST_DR + NA_KH - 1


def _pair_halves(entry):
    if entry == 0:
        return None, None
    if entry == PAIR_RIGHT_ONLY:
        return None, INTERIOR_FIRST_DR
    if entry == PAIR_LEFT_ONLY:
        return INTERIOR_LAST_DR, None
    return entry - 1, entry


def _pair_table_kernel(ext_ref, o_ref):
    qc = lax.broadcasted_iota(jnp.int32, (GRID_W, LANES), 0)
    lane = lax.broadcasted_iota(jnp.int32, (GRID_W, LANES), 1)
    kc = jnp.bitwise_and(lane, GRID_W - 1)
    col_start = jnp.clip(qc - NA_KW // 2, 0, GRID_W - NA_KW)
    col_in = jnp.logical_and(kc >= col_start, kc < col_start + NA_KW)
    is_left = lane < GRID_W

    def half(dr, right):
        rows = jnp.broadcast_to(ext_ref[dr:dr + 1, :], (GRID_W, LANES))
        shift = 1 if right else LANES - GRID_W + 1
        return pltpu.roll(rows, shift, 1, stride=1, stride_axis=0)

    for entry in range(PAIR_ENTRIES):
        left_dr, right_dr = _pair_halves(entry)
        block = jnp.full((GRID_W, LANES), NEG_INF, F32)
        if left_dr is not None:
            block = jnp.where(jnp.logical_and(is_left, col_in), half(left_dr, False), block)
        if right_dr is not None:
            block = jnp.where(jnp.logical_and(jnp.logical_not(is_left), col_in), half(right_dr, True), block)
        o_ref[entry] = block


def _na_pair_table(rpb):
    lead = GRID_W - NA_KW
    ext = jnp.pad(rpb.astype(F32) * LOG2E, ((0, 0), (0, 0), (lead, LANES - N_DC - lead)), mode="edge")
    return pl.pallas_call(
        _pair_table_kernel,
        grid=(D_HEADS,),
        in_specs=[pl.BlockSpec((None, N_DR, LANES), lambda h: (h, 0, 0))],
        out_specs=pl.BlockSpec((None, PAIR_ENTRIES, GRID_W, LANES), lambda h: (h, 0, 0, 0)),
        out_shape=jax.ShapeDtypeStruct((D_HEADS, PAIR_ENTRIES, GRID_W, LANES), F32),
        compiler_params=_cparams(1),
    )(ext)


def _na_pair_entry(variant, a, p):
    if variant == "first":
        return 2 * p - a + NA_KH if p < NA_KH // 2 else 0
    if variant == "last":
        return 2 * p - a if p >= (BAND_GRID_ROWS - NA_KH) // 2 else 0
    left_ok = a <= 2 * p < a + NA_KH
    right_ok = a <= 2 * p + 1 < a + NA_KH
    if left_ok and right_ok:
        return 2 * p - a + NA_KH // 2
    if right_ok:
        return PAIR_RIGHT_ONLY
    if left_ok:
        return PAIR_LEFT_ONLY
    return 0


def _attn_d_kernel(first_tile, q_ref, g_ref, k_ref, v_ref, tab_ref, o_ref):
    t = pl.program_id(1) + first_tile

    def normalise(acc):
        return (acc / pltpu.roll(acc, HEAD_DIM, 1))[:, 0:HEAD_DIM]

    def ctx_tile():
        out = _PairStore(o_ref, g_ref)

        def scores(h):
            lanes = slice(h * HEAD_DIM, (h + 1) * HEAD_DIM)
            return [lax.dot_general(q_ref[:, lanes], k_ref[0:CTX_LEN, lanes], NT_DIMS, preferred_element_type=F32)]

        def finish(h, s_list):
            acc, _ = _softmax_pv(s_list, [v_ref[0:CTX_LEN, h * LANES:(h + 1) * LANES]])
            out.put(h, normalise(acc))

        _pipelined(D_HEADS, scores, finish, depth=CTX_PIPELINE_DEPTH)

    def lat_tile():
        out = _PairStore(o_ref, g_ref)
        start = pl.multiple_of(
            jnp.clip((t - 1) * ROW_TILE - (NA_KH // 2) * GRID_W, 0, SEQ - BAND_D) + CTX_LEN, LANES)
        is_first = t == 1
        is_last = t == N_TILES - 1

        def bias(h):
            rows = []
            for a in range(TILE_GRID_ROWS):
                blocks = []
                for p in range(BAND_GRID_ROWS // 2):
                    entry = jnp.where(is_first, _na_pair_entry("first", a, p),
                                      jnp.where(is_last, _na_pair_entry("last", a, p),
                                                _na_pair_entry("interior", a, p)))
                    blocks.append(tab_ref[h, entry])
                rows.append(jnp.concatenate(blocks, axis=1))
            return jnp.concatenate(rows, axis=0)

        def scores(h):
            lanes = slice(h * HEAD_DIM, (h + 1) * HEAD_DIM)
            q = q_ref[:, lanes]
            kb = k_ref[pl.ds(start, BAND_D), :][:, lanes]
            s_ctx = lax.dot_general(q, k_ref[0:CTX_LEN, lanes], NT_DIMS, preferred_element_type=F32)
            s_band = lax.dot_general(q, kb, NT_DIMS, preferred_element_type=F32) + bias(h)
            return [s_ctx, s_band]

        def finish(h, s_list):
            v_lanes = slice(h * LANES, (h + 1) * LANES)
            v_list = [v_ref[0:CTX_LEN, v_lanes], v_ref[pl.ds(start, BAND_D), :][:, v_lanes]]
            acc, _ = _softmax_pv(s_list, v_list)
            out.put(h, normalise(acc))

        _pipelined(D_HEADS, scores, finish)

    if first_tile == 0:
        pl.when(t == 0)(ctx_tile)
        pl.when(t > 0)(lat_tile)
    else:
        lat_tile()


def _attn_d(h_odd, pair_tab, lat_only):
    w = D_HEADS * HEAD_DIM
    f = 1 if lat_only else 0
    return pl.pallas_call(
        functools.partial(_attn_d_kernel, f),
        grid=(BATCH, N_TILES - f),
        in_specs=[
            pl.BlockSpec((None, ROW_TILE, w), lambda b, t: (b, t + f, OD_DQ // w)),
            pl.BlockSpec((None, ROW_TILE, w), lambda b, t: (b, t + f, OD_DG // w)),
            pl.BlockSpec((None, T_ALL, w), lambda b, t: (b, 0, OD_DK // w)),
            pl.BlockSpec((None, T_ALL, 2 * w), lambda b, t: (b, 0, OD_DV // (2 * w))),
            pl.BlockSpec((D_HEADS, PAIR_ENTRIES, GRID_W, LANES), lambda b, t: (0, 0, 0, 0)),
        ],
        out_specs=pl.BlockSpec((None, ROW_TILE, w), lambda b, t: (b, t, 0)),
        out_shape=jax.ShapeDtypeStruct((BATCH, T_ALL - f * CTX_LEN, w), BF16),
        compiler_params=_cparams(2),
    )(h_odd, h_odd, h_odd, h_odd, pair_tab)


FINAL_TILE = 2 * ROW_TILE


def _outproj_final_kernel(ya_ref, yb_ref, w_ref, x_lo_ref, x_hi_ref, mod_ref, fg_ref, o_ref):
    half = w_ref.shape[0] // 2
    y = jnp.dot(ya_ref[...], w_ref[0:half, :], preferred_element_type=F32)
    y = y + jnp.dot(yb_ref[...], w_ref[half:, :], preferred_element_type=F32)
    x = jnp.concatenate([x_lo_ref[...], x_hi_ref[...]], axis=0) + mod_ref[:, 2 * D_MODEL:3 * D_MODEL] * y
    ms = jnp.mean(x * x, axis=-1, keepdims=True)
    o_ref[...] = x * lax.rsqrt(ms + EPS) * fg_ref[...]


def _outproj_final(ya, yb, w_out, xs, mods, layer, final_g):
    half = w_out.shape[0] // 2
    per = FINAL_TILE // ROW_TILE
    x_blk = lambda off: pl.BlockSpec((None, ROW_TILE, D_MODEL), lambda b, t: (b, per * t + 1 + off, 0))
    return pl.pallas_call(
        _outproj_final_kernel,
        grid=(BATCH, SEQ // FINAL_TILE),
        in_specs=[
            pl.BlockSpec((None, FINAL_TILE, half), lambda b, t: (b, t, 0)),
            pl.BlockSpec((None, FINAL_TILE, half), lambda b, t: (b, t, 0)),
            pl.BlockSpec((2 * half, D_MODEL), lambda b, t: (0, 0)),
            x_blk(0), x_blk(1),
            pl.BlockSpec((None, 1, 3 * D_MODEL), lambda b, t: (layer * MOD_ROWS + b, 0, 0)),
            pl.BlockSpec((1, D_MODEL), lambda b, t: (0, 0)),
        ],
        out_specs=pl.BlockSpec((None, FINAL_TILE, D_MODEL), lambda b, t: (b, t, 0)),
        out_shape=jax.ShapeDtypeStruct((BATCH, SEQ, D_MODEL), F32),
        compiler_params=_cparams(2),
    )(ya, yb, w_out.astype(BF16), xs, xs, mods, final_g.reshape(1, D_MODEL))


def kernel(x, c, ctx, c_ctx, mod_w, mod_b, norm_g, ev_w_in, ev_w_out, a_q_gain, a_k_gain, b_sink,
           od_w_in, od_w_out, c_dw_w, c_dw_b, c_ln_g, c_ln_b, d_rpb, final_g):
    cond = jnp.concatenate(
        [c, c_ctx[None, :], jnp.zeros((MOD_ROWS - BATCH - 1, D_MODEL), F32)], axis=0)
    mods = _modulation_all(cond, mod_w, mod_b).reshape(DEPTH * MOD_ROWS, 1, 3 * D_MODEL)
    tables = _rope_tables()
    head_of_lane = np.arange(LANES) // HEAD_DIM
    bd = jnp.asarray((head_of_lane[:, None] == head_of_lane[None, :]).astype(np.float32) / HEAD_DIM, BF16)

    xs = None
    pending = None
    for i in range(DEPTH):
        j = i // 2
        src = dict(x=x, ctx=ctx) if xs is None else dict(xs=xs)
        if i % 2 == 0:
            extra = (a_q_gain[j], a_k_gain[j], tables, bd)
            stream, h_even = _proj("even", i, norm_g[i], ev_w_in[j], mods, out=pending, even_extra=extra, **src)
            pending = (_attn_a(h_even), _attn_b(h_even, b_sink[j]), ev_w_out[j], i)
        else:
            stream, h_odd, u = _proj("odd", i, norm_g[i], od_w_in[j], mods, out=pending, **src)
            last = i == DEPTH - 1
            ya = _conv_c(u, h_odd, c_dw_w[j], c_dw_b[j], c_ln_g[j], c_ln_b[j], lat_only=last)
            pending = (ya, _attn_d(h_odd, _na_pair_table(d_rpb[j]), lat_only=last), od_w_out[j], i)
        if stream is not None:
            xs = stream
    ya, yb, w_out, layer = pending
    return _outproj_final(ya, yb, w_out, xs, mods, layer, final_g)
```

```python
import functools

import numpy as np
import jax
import jax.numpy as jnp
from jax import lax
from jax.experimental import pallas as pl
from jax.experimental.pallas import tpu as pltpu

D_MODEL = 1024
BATCH = 8
SEQ = 2048
DEPTH = 4
GRID_W = 64
CTX_LEN = 256
HEAD_DIM = 64
ROPE_THETA = 10000.0
EPS = 1e-6
NEG_INF = -1e30
ATTN_SCALE = HEAD_DIM ** -0.5
A_HEADS = 8
A_KV_HEADS = 2
B_HEADS = 8
B_KV_HEADS = 2
B_WINDOW = 128
C_WIDTH = 512
C_CONV = 31
D_HEADS = 8
NA_KH = 8
NA_KW = 16
ROWS = SEQ // GRID_W

T_ALL = CTX_LEN + SEQ
ROW_TILE = 256
N_TILES = T_ALL // ROW_TILE
PROJ_TILE = 3 * ROW_TILE
PROJ_TILES = T_ALL // PROJ_TILE
LANES = 128
MXU_N = 256
MOD_ROWS = 16
CTX_MOD_ROW = BATCH
VMEM_LIMIT = 48 * 1024 * 1024
PROJ_VMEM_LIMIT = 56 * 1024 * 1024

EVEN_IN = 2560
ODD_IN = 3584
LOG2E = 1.4426950408889634
Q_SCALE = ATTN_SCALE * LOG2E
STACK = 1
CTX_PIPELINE_DEPTH = 8
PIPELINE_DEPTH = 2
HALO = 16
BAND_B = 512
BAND_D = 768

F32 = jnp.float32
BF16 = jnp.bfloat16
NT_DIMS = (((1,), (1,)), ((), ()))


def _cparams(n_grid, vmem=VMEM_LIMIT):
    return pltpu.CompilerParams(dimension_semantics=("arbitrary",) * n_grid, vmem_limit_bytes=vmem)


def _silu(v):
    return v * jax.nn.sigmoid(v)


def _mod_kernel(cond_ref, w_ref, b_ref, o_ref):
    cnd = cond_ref[...]
    act = _silu(cnd).astype(BF16)
    o_ref[...] = jnp.dot(act, w_ref[...].astype(BF16), preferred_element_type=F32) + b_ref[...]


def _modulation_all(cond, mod_w, mod_b):
    n_col = 3
    return pl.pallas_call(
        _mod_kernel,
        grid=(DEPTH, n_col),
        in_specs=[
            pl.BlockSpec((MOD_ROWS, D_MODEL), lambda i, j: (0, 0)),
            pl.BlockSpec((None, D_MODEL, D_MODEL), lambda i, j: (i, 0, j)),
            pl.BlockSpec((None, 1, D_MODEL), lambda i, j: (i, 0, j)),
        ],
        out_specs=pl.BlockSpec((None, MOD_ROWS, D_MODEL), lambda i, j: (i, 0, j)),
        out_shape=jax.ShapeDtypeStruct((DEPTH, MOD_ROWS, 3 * D_MODEL), F32),
        compiler_params=_cparams(2),
    )(cond, mod_w, mod_b.reshape(DEPTH, 1, 3 * D_MODEL))


EVEN_OUT = 2816
EV_AQ, EV_AG, EV_BQ, EV_BG, EV_AV, EV_BV, EV_AK, EV_BK = 0, 512, 1024, 1536, 2048, 2304, 2560, 2688

ODD_OUT = 3072
OD_CG, OD_DQ, OD_DK, OD_DG, OD_DV = 0, 512, 1024, 1536, 2048


def _even_pieces():
    pieces = []
    for j in range(4):
        pieces.append((0 + LANES * j, "aq", EV_AQ + LANES * j))
    pieces.append((512, "ak", EV_AK))
    pieces.append((640, "v", EV_AV))
    for j in range(4):
        pieces.append((768 + LANES * j, "gate", EV_AG + LANES * j))
    for j in range(4):
        pieces.append((1280 + LANES * j, "bq", EV_BQ + LANES * j))
    pieces.append((1792, "bk", EV_BK))
    pieces.append((1920, "v", EV_BV))
    for j in range(4):
        pieces.append((2048 + LANES * j, "gate", EV_BG + LANES * j))
    return pieces


def _rope(v, cos, sin_even, sin_odd):
    nxt = pltpu.roll(v, LANES - 1, 1)
    prv = pltpu.roll(v, 1, 1)
    return v * cos + nxt * sin_even + prv * sin_odd


def _head_rms(v, gain, bd_ref):
    ms = jnp.dot((v * v).astype(BF16), bd_ref[...], preferred_element_type=F32)
    return v * lax.rsqrt(ms + EPS) * gain


def _even_epilogue(h_slabs, w_ref, qg_ref, kg_ref, cos_ref, se_ref, so_ref, bd_ref, o_ref):
    cos, s_even, s_odd = cos_ref[...], se_ref[...], so_ref[...]
    pieces = _even_pieces()
    ones = jnp.ones((PROJ_TILE, LANES), BF16)
    n_chunks = EVEN_IN // MXU_N

    h = jnp.concatenate(h_slabs, axis=0)

    def proj(c):
        return jnp.dot(h, w_ref[:, c * MXU_N:(c + 1) * MXU_N], preferred_element_type=F32)

    acc_next = proj(0)
    for c in range(n_chunks):
        acc = acc_next
        if c + 1 < n_chunks:
            acc_next = proj(c + 1)
        for half in range(MXU_N // LANES):
            src = c * MXU_N + half * LANES
            (_, kind, dst), = [p for p in pieces if p[0] == src]
            v = acc[:, half * LANES:(half + 1) * LANES]
            if kind == "aq":
                v = _rope(_head_rms(v, qg_ref[...], bd_ref), cos, s_even, s_odd) * Q_SCALE
            elif kind == "ak":
                v = _rope(_head_rms(v, kg_ref[...], bd_ref), cos, s_even, s_odd)
            elif kind == "bq":
                v = _rope(v, cos, s_even, s_odd) * Q_SCALE
            elif kind == "bk":
                v = _rope(v, cos, s_even, s_odd)
            elif kind == "gate":
                v = _silu(v)
            elif kind == "v":
                o_ref[:, dst + LANES:dst + 2 * LANES] = ones
            o_ref[:, dst:dst + LANES] = v.astype(BF16)


def _odd_epilogue(t, h_slabs, w_ref, conv_refs, o_ref, u_ref, yc_ref):
    h = jnp.concatenate(h_slabs, axis=0)
    dw_ref, db_ref, lg_ref, lb_ref = conv_refs
    n_slabs = PROJ_TILE // ROW_TILE
    n_cc = C_WIDTH // MXU_N

    def proj(col):
        return jnp.dot(h, w_ref[:, col:col + MXU_N], preferred_element_type=F32)

    ones = jnp.ones((PROJ_TILE, HEAD_DIM), F32)
    heads_per_chunk = MXU_N // HEAD_DIM
    held, u_val, sg_val, conv_acc, windows = {}, {}, {}, {}, {}

    def window(s, c):
        zeros = jnp.zeros((HALO, MXU_N), F32)
        lo, hi = s * ROW_TILE, (s + 1) * ROW_TILE
        left = u_val[c][lo - HALO:lo] if s > 0 else zeros
        right = u_val[c][hi:hi + HALO] if s < n_slabs - 1 else zeros
        if s == 1:
            left = jnp.where(t == 0, zeros, left)
        if s == 0:
            right = jnp.where(t == 0, zeros, right)
        return jnp.concatenate([left, u_val[c][lo:hi], right], axis=0)

    def conv_piece(s, c, r):
        lanes = slice(c * MXU_N, (c + 1) * MXU_N)
        n_win = ROW_TILE + 2 * HALO
        if (s, c) not in windows:
            windows[(s, c)] = window(s, c)
        win = windows[(s, c)]
        shifted = win if r == 0 else pltpu.roll(win, n_win - r, 0)
        acc = conv_acc.get((s, c))
        if acc is None:
            acc = jnp.zeros((ROW_TILE, MXU_N), F32) + db_ref[:, lanes]
        for k in range(C_CONV):
            off = HALO - C_CONV // 2 + k
            if off % CONV_GROUPS == r:
                base = off - r
                acc = acc + shifted[base:base + ROW_TILE, :] * dw_ref[k:k + 1, lanes]
        conv_acc[(s, c)] = acc

    def conv_finish(s):
        acc = jnp.concatenate([conv_acc.pop((s, c)) for c in range(n_cc)], axis=1)
        sg = jnp.concatenate([sg_val[c][s * ROW_TILE:(s + 1) * ROW_TILE] for c in range(n_cc)], axis=1)
        yc_ref[s * ROW_TILE:(s + 1) * ROW_TILE, :] = _conv_tail(acc, sg, lg_ref, lb_ref)

    conv_todo = [("piece", s, c, r) for s in range(n_slabs) for c in range(n_cc) for r in range(CONV_GROUPS)]
    for s in range(n_slabs):
        conv_todo.insert((s + 1) * n_cc * CONV_GROUPS + s, ("finish", s))

    def finish(kind, c, v):
        lanes = slice(c * MXU_N, (c + 1) * MXU_N)
        if kind == "val":
            held[c] = v
        elif kind == "glu":
            u_val[c] = held.pop(c) * jax.nn.sigmoid(v)
            u_ref[:, lanes] = u_val[c]
        elif kind == "v":
            for i in range(heads_per_chunk):
                head = c * heads_per_chunk + i
                ext = jnp.concatenate([v[:, i * HEAD_DIM:(i + 1) * HEAD_DIM], ones], axis=1)
                o_ref[:, OD_DV + head * LANES:OD_DV + (head + 1) * LANES] = ext.astype(BF16)
        else:
            dst = {"cg": OD_CG, "q": OD_DQ, "k": OD_DK, "dg": OD_DG}[kind]
            if kind in ("cg", "dg"):
                v = _silu(v)
                if kind == "cg":
                    sg_val[c] = v
            elif kind == "q":
                v = v * Q_SCALE
            o_ref[:, dst + c * MXU_N:dst + (c + 1) * MXU_N] = v.astype(BF16)

    order = [("val", 0), ("glu", 1), ("cg", 2), ("q", 3), ("k", 4), ("v", 5), ("dg", 6)]
    steps = [(kind, c, g * C_WIDTH + c * MXU_N) for c in range(C_WIDTH // MXU_N) for kind, g in order[:2]]
    steps += [(kind, c, g * C_WIDTH + c * MXU_N) for kind, g in order[2:] for c in range(C_WIDTH // MXU_N)]
    conv_from = 4
    per_step = -(-len(conv_todo) // (len(steps) - conv_from))
    acc_next = proj(steps[0][2])
    for i, (kind, c, _) in enumerate(steps):
        acc = acc_next
        if i + 1 < len(steps):
            acc_next = proj(steps[i + 1][2])
        finish(kind, c, acc)
        if i >= conv_from:
            for item in conv_todo[(i - conv_from) * per_step:(i - conv_from + 1) * per_step]:
                if item[0] == "piece":
                    conv_piece(*item[1:])
                else:
                    conv_finish(item[1])


def _make_proj_kernel(first, has_out, in_kind):
    def kern(*refs):
        it = iter(refs)
        t = pl.program_id(1)
        if first:
            ctx_ref, xa_ref, xb_ref, xc_ref = next(it), next(it), next(it), next(it)
        else:
            x_ref = next(it)
        if has_out:
            ya_ref, yb_ref, wo_ref, cur_ref, cur_ctx_ref = (next(it) for _ in range(5))
        norm_ref, nxt_ref, nxt_ctx_ref, wi_ref = (next(it) for _ in range(4))
        if in_kind == "even":
            extra = [next(it) for _ in range(6)]
        else:
            conv_refs = [next(it) for _ in range(4)]
        if has_out:
            xo_ref = next(it)
        o_ref = next(it)
        if in_kind == "odd":
            u_ref, yc_ref = next(it), next(it)

        h_slabs = []
        for s in range(PROJ_TILE // ROW_TILE):
            rows = slice(s * ROW_TILE, (s + 1) * ROW_TILE)
            is_ctx = jnp.logical_and(t == 0, s == 0)
            if first:
                x = jnp.where(t == 0, ctx_ref[...], xa_ref[...]) if s == 0 else (xb_ref, xc_ref)[s - 1][...]
            else:
                x = x_ref[rows, :]
            if has_out:
                half = wo_ref.shape[0] // 2
                y = jnp.dot(ya_ref[rows, :], wo_ref[0:half, :], preferred_element_type=F32)
                y = y + jnp.dot(yb_ref[rows, :], wo_ref[half:, :], preferred_element_type=F32)
                cur = jnp.where(is_ctx, cur_ctx_ref[...], cur_ref[...]) if s == 0 else cur_ref[...]
                x = x + cur[:, 2 * D_MODEL:3 * D_MODEL] * y
                xo_ref[rows, :] = x
            nxt = jnp.where(is_ctx, nxt_ctx_ref[...], nxt_ref[...]) if s == 0 else nxt_ref[...]
            ms = jnp.mean(x * x, axis=-1, keepdims=True)
            normed = x * lax.rsqrt(ms + EPS) * norm_ref[...]
            h_slabs.append((normed * (1.0 + nxt[:, D_MODEL:2 * D_MODEL]) + nxt[:, 0:D_MODEL]).astype(BF16))
        if in_kind == "even":
            _even_epilogue(h_slabs, wi_ref, *extra, o_ref)
        else:
            _odd_epilogue(t, h_slabs, wi_ref, conv_refs, o_ref, u_ref, yc_ref)

    return kern


def _proj(in_kind, layer_in, norm_g, w_in, mods, *, x=None, ctx=None, xs=None, out=None, even_extra=None,
          conv=None):
    first = xs is None
    has_out = out is not None
    const = lambda shape: pl.BlockSpec(shape, lambda b, t: (0,) * len(shape), pipeline_mode=pl.Buffered(1))
    tile = lambda width: pl.BlockSpec((None, PROJ_TILE, width), lambda b, t: (b, t, 0))
    mod_sample = lambda layer: pl.BlockSpec((None, 1, 3 * D_MODEL), lambda b, t: (layer * MOD_ROWS + b, 0, 0))
    mod_ctx = lambda layer: pl.BlockSpec((None, 1, 3 * D_MODEL),
                                         lambda b, t: (layer * MOD_ROWS + CTX_MOD_ROW, 0, 0))
    args, specs = [], []
    if first:
        per = PROJ_TILE // ROW_TILE
        blk = lambda off: pl.BlockSpec((None, ROW_TILE, D_MODEL),
                                       lambda b, t: (b, jnp.maximum(per * t + off, 0), 0))
        args += [ctx, x, x, x]
        specs += [pl.BlockSpec((None, CTX_LEN, D_MODEL), lambda b, t: (b, 0, 0)), blk(-1), blk(0), blk(1)]
    else:
        args.append(xs)
        specs.append(tile(D_MODEL))
    aliases = {}
    if has_out:
        ya, yb, w_out, layer_out = out
        half = w_out.shape[0] // 2
        args += [ya, yb, w_out.astype(BF16), mods, mods]
        specs += [tile(half), tile(half), const((2 * half, D_MODEL)), mod_sample(layer_out), mod_ctx(layer_out)]
        if not first:
            aliases = {0: 0}
    n_in = EVEN_IN if in_kind == "even" else ODD_IN
    args += [norm_g.reshape(1, D_MODEL), mods, mods, w_in.astype(BF16)]
    specs += [const((1, D_MODEL)), mod_sample(layer_in), mod_ctx(layer_in), const((D_MODEL, n_in))]
    if in_kind == "even":
        q_gain, k_gain, (cos, s_even, s_odd), bd = even_extra
        tab = pl.BlockSpec((PROJ_TILE, LANES), lambda b, t: (t, 0))
        args += [jnp.tile(q_gain, LANES // HEAD_DIM).reshape(1, LANES),
                 jnp.tile(k_gain, LANES // HEAD_DIM).reshape(1, LANES), cos, s_even, s_odd, bd]
        specs += [const((1, LANES)), const((1, LANES)), tab, tab, tab, const((LANES, LANES))]
    else:
        dw_w, dw_b, ln_g, ln_b = conv
        args += [dw_w, dw_b.reshape(1, C_WIDTH), ln_g.reshape(1, C_WIDTH), ln_b.reshape(1, C_WIDTH)]
        specs += [const((C_CONV, C_WIDTH)), const((1, C_WIDTH)), const((1, C_WIDTH)), const((1, C_WIDTH))]
    out_shapes, out_specs = [], []
    if has_out:
        out_shapes.append(jax.ShapeDtypeStruct((BATCH, T_ALL, D_MODEL), F32))
        out_specs.append(tile(D_MODEL))
    if in_kind == "even":
        out_shapes.append(jax.ShapeDtypeStruct((BATCH, T_ALL, EVEN_OUT), BF16))
        out_specs.append(tile(EVEN_OUT))
    else:
        out_shapes += [jax.ShapeDtypeStruct((BATCH, T_ALL, ODD_OUT), BF16),
                       jax.ShapeDtypeStruct((BATCH, T_ALL, C_WIDTH), F32),
                       jax.ShapeDtypeStruct((BATCH, T_ALL, C_WIDTH), BF16)]
        out_specs += [tile(ODD_OUT), tile(C_WIDTH), tile(C_WIDTH)]
    res = pl.pallas_call(
        _make_proj_kernel(first, has_out, in_kind),
        grid=(BATCH, PROJ_TILES),
        in_specs=specs,
        out_specs=out_specs,
        out_shape=out_shapes,
        input_output_aliases=aliases,
        compiler_params=_cparams(2, PROJ_VMEM_LIMIT),
    )(*args)
    res = list(res)
    stream = res.pop(0) if has_out else None
    return (stream, *res)


def _rope_tables():
    t = np.arange(SEQ)
    row = (t // GRID_W).astype(np.float64)
    col = (t % GRID_W).astype(np.float64)
    half = HEAD_DIM // 2
    freqs = (np.float32(ROPE_THETA) ** (-np.arange(0, half, 2, dtype=np.float32) / half)).astype(np.float64)
    ang = np.concatenate([row[:, None] * freqs, col[:, None] * freqs], axis=-1).astype(np.float32)
    cos = np.repeat(np.cos(ang.astype(np.float64)), 2, axis=-1)
    sin = np.repeat(np.sin(ang.astype(np.float64)), 2, axis=-1)
    lane_even = (np.arange(HEAD_DIM) % 2 == 0)
    s_even = np.where(lane_even, -sin, 0.0)
    s_odd = np.where(lane_even, 0.0, sin)

    def full(tab, ctx_val):
        tab = np.concatenate([np.full((CTX_LEN, HEAD_DIM), ctx_val), tab], axis=0).astype(np.float32)
        return jnp.asarray(np.tile(tab, (1, LANES // HEAD_DIM)))

    return full(cos, 1.0), full(s_even, 0.0), full(s_odd, 0.0)


def _softmax_pv(s_list, v_list, sink=None):
    m = s_list[0].max(axis=-1, keepdims=True)
    for s in s_list[1:]:
        m = jnp.maximum(m, s.max(axis=-1, keepdims=True))
    if sink is not None:
        m = jnp.maximum(m, sink)
    acc = None
    for s, v in zip(s_list, v_list):
        p = jnp.exp2(s - m).astype(BF16)
        part = jnp.dot(p, v, preferred_element_type=F32)
        acc = part if acc is None else acc + part
    extra = jnp.exp2(sink - m) if sink is not None else None
    return acc, extra


def _pipelined(n_heads, scores_fn, finish_fn, depth=PIPELINE_DEPTH):
    ahead = [scores_fn(h) for h in range(min(depth, n_heads))]
    for h in range(n_heads):
        if h + depth < n_heads:
            ahead.append(scores_fn(h + depth))
        finish_fn(h, ahead.pop(0))


class _PairStore:
    def __init__(self, o_ref, g_ref):
        self.o_ref, self.g_ref, self.held = o_ref, g_ref, None

    def put(self, h, o):
        if h % 2 == 0:
            self.held = o
            return
        lanes = slice((h - 1) * HEAD_DIM, (h + 1) * HEAD_DIM)
        pair = jnp.concatenate([self.held, o], axis=1)
        self.o_ref[:, lanes] = (pair * self.g_ref[:, lanes].astype(F32)).astype(BF16)


def _gqa_normalise(acc, extra, kv):
    den = acc[:, LANES:2 * LANES]
    if extra is not None:
        den = den + extra
    o = acc[:, 0:LANES] / den
    return o[:, kv * HEAD_DIM:(kv + 1) * HEAD_DIM]


def _attn_a_kernel(q_ref, g_ref, k_ref, v_ref, o_ref):
    group = A_HEADS // A_KV_HEADS

    def run(n_keys):
        out = _PairStore(o_ref, g_ref)

        def scores(h):
            kv = h // group
            q = q_ref[:, h * HEAD_DIM:(h + 1) * HEAD_DIM]
            k = k_ref[0:n_keys, kv * HEAD_DIM:(kv + 1) * HEAD_DIM]
            return [lax.dot_general(q, k, NT_DIMS, preferred_element_type=F32)]

        def finish(h, s_list):
            acc, extra = _softmax_pv(s_list, [v_ref[0:n_keys, :]])
            out.put(h, _gqa_normalise(acc, extra, h // group))

        _pipelined(A_HEADS, scores, finish, depth=CTX_PIPELINE_DEPTH if n_keys == CTX_LEN else PIPELINE_DEPTH)

    t = pl.program_id(1)
    pl.when(t == 0)(lambda: run(CTX_LEN))
    pl.when(t > 0)(lambda: run(T_ALL))


def _attn_a(h_even):
    w = A_HEADS * HEAD_DIM
    return pl.pallas_call(
        _attn_a_kernel,
        grid=(BATCH, N_TILES),
        in_specs=[
            pl.BlockSpec((None, ROW_TILE, w), lambda b, t: (b, t, EV_AQ // w)),
            pl.BlockSpec((None, ROW_TILE, w), lambda b, t: (b, t, EV_AG // w)),
            pl.BlockSpec((None, T_ALL, LANES), lambda b, t: (b, 0, EV_AK // LANES)),
            pl.BlockSpec((None, T_ALL, 2 * LANES), lambda b, t: (b, 0, EV_AV // (2 * LANES))),
        ],
        out_specs=pl.BlockSpec((None, ROW_TILE, w), lambda b, t: (b, t, 0)),
        out_shape=jax.ShapeDtypeStruct((BATCH, T_ALL, w), BF16),
        compiler_params=_cparams(2),
    )(h_even, h_even, h_even, h_even)


def _attn_b_kernel(sink_ref, q_ref, g_ref, k_ref, v_ref, o_ref):
    group = B_HEADS // B_KV_HEADS
    t = pl.program_id(1)

    def run(row_spans, band_bias):
        out = _PairStore(o_ref, g_ref)
        k_all = jnp.concatenate([k_ref[pl.ds(s, n), :] for s, n in row_spans], axis=0)
        v_all = jnp.concatenate([v_ref[pl.ds(s, n), :] for s, n in row_spans], axis=0)
        k_kv = [k_all[:, kv * HEAD_DIM:(kv + 1) * HEAD_DIM] for kv in range(B_KV_HEADS)]

        def heads_of(u):
            return list(range(u * STACK, (u + 1) * STACK))

        def scores(u):
            heads = heads_of(u)
            q = jnp.concatenate([q_ref[:, h * HEAD_DIM:(h + 1) * HEAD_DIM] for h in heads], axis=0)
            s = lax.dot_general(q, k_kv[heads[0] // group], NT_DIMS, preferred_element_type=F32)
            if band_bias is None:
                return s
            slabs = []
            for i in range(STACK):
                slab = s[i * ROW_TILE:(i + 1) * ROW_TILE]
                slabs.append(jnp.concatenate([slab[:, 0:CTX_LEN], slab[:, CTX_LEN:] + band_bias], axis=1))
            return jnp.concatenate(slabs, axis=0)

        def finish(u, s):
            heads = heads_of(u)
            kv = heads[0] // group
            sink = jnp.concatenate([jnp.full((ROW_TILE, 1), sink_ref[h] * LOG2E, F32) for h in heads], axis=0)
            m = jnp.maximum(s.max(axis=-1, keepdims=True), sink)
            p = jnp.exp2(s - m).astype(BF16)
            acc = jnp.dot(p, v_all, preferred_element_type=F32)
            den = acc[:, LANES:2 * LANES] + jnp.exp2(sink - m)
            o = (acc[:, 0:LANES] / den)[:, kv * HEAD_DIM:(kv + 1) * HEAD_DIM]
            for i, h in enumerate(heads):
                out.put(h, o[i * ROW_TILE:(i + 1) * ROW_TILE])

        _pipelined(B_HEADS // STACK, scores, finish,
                   depth=CTX_PIPELINE_DEPTH if band_bias is None else PIPELINE_DEPTH)

    def lat_tile():
        start = pl.multiple_of(jnp.clip(t * ROW_TILE - B_WINDOW, CTX_LEN, T_ALL - BAND_B), LANES)
        q_pos = t * ROW_TILE + lax.broadcasted_iota(jnp.int32, (ROW_TILE, BAND_B), 0)
        k_pos = start + lax.broadcasted_iota(jnp.int32, (ROW_TILE, BAND_B), 1)
        bias = jnp.where(jnp.abs(k_pos - q_pos) <= B_WINDOW, 0.0, NEG_INF).astype(F32)
        run([(0, CTX_LEN), (start, BAND_B)], bias)

    pl.when(t == 0)(lambda: run([(0, CTX_LEN)], None))
    pl.when(t > 0)(lat_tile)


def _attn_b(h_even, sink):
    w = B_HEADS * HEAD_DIM
    return pl.pallas_call(
        _attn_b_kernel,
        grid=(BATCH, N_TILES),
        in_specs=[
            pl.BlockSpec(memory_space=pltpu.SMEM),
            pl.BlockSpec((None, ROW_TILE, w), lambda b, t: (b, t, EV_BQ // w)),
            pl.BlockSpec((None, ROW_TILE, w), lambda b, t: (b, t, EV_BG // w)),
            pl.BlockSpec((None, T_ALL, LANES), lambda b, t: (b, 0, EV_BK // LANES)),
            pl.BlockSpec((None, T_ALL, 2 * LANES), lambda b, t: (b, 0, EV_BV // (2 * LANES))),
        ],
        out_specs=pl.BlockSpec((None, ROW_TILE, w), lambda b, t: (b, t, 0)),
        out_shape=jax.ShapeDtypeStruct((BATCH, T_ALL, w), BF16),
        compiler_params=_cparams(2),
    )(sink, h_even, h_even, h_even, h_even)


CONV_GROUPS = 8


def _conv_tail(acc, sg, lg_ref, lb_ref):
    mu = jnp.mean(acc, axis=-1, keepdims=True)
    ctr = acc - mu
    var = jnp.mean(ctr * ctr, axis=-1, keepdims=True)
    y = ctr * lax.rsqrt(var + EPS) * lg_ref[...] + lb_ref[...]
    return (_silu(y) * sg).astype(BF16)


def _conv_fixup_kernel(ul_ref, uc_ref, ur_ref, sg_ref, w_ref, b_ref, lg_ref, lb_ref, ya_ref, o_ref):
    del ya_ref
    win = jnp.concatenate([ul_ref[...], uc_ref[...], ur_ref[...]], axis=0)
    n_win = 3 * HALO
    acc = jnp.zeros((HALO, C_WIDTH), F32) + b_ref[...]
    for r in range(CONV_GROUPS):
        shifted = win if r == 0 else pltpu.roll(win, n_win - r, 0)
        for k in range(C_CONV):
            off = HALO - C_CONV // 2 + k
            if off % CONV_GROUPS == r:
                base = off - r
                acc = acc + shifted[base:base + HALO, :] * w_ref[k:k + 1, :]
    o_ref[...] = _conv_tail(acc, sg_ref[...].astype(F32), lg_ref, lb_ref)


def _conv_fixup(ya, u, h_odd, dw_w, dw_b, ln_g, ln_b):
    per_tile = PROJ_TILE // HALO
    edge_block = lambda k: (k // 2 + 1) * per_tile - 1 + k % 2
    blk = lambda off, col=0: pl.BlockSpec((None, HALO, C_WIDTH), lambda b, k: (b, edge_block(k) + off, col))
    const = lambda shape: pl.BlockSpec(shape, lambda b, k: (0,) * len(shape))
    return pl.pallas_call(
        _conv_fixup_kernel,
        grid=(BATCH, 2 * (PROJ_TILES - 1)),
        in_specs=[
            blk(-1), blk(0), blk(1), blk(0, OD_CG // C_WIDTH),
            const((C_CONV, C_WIDTH)), const((1, C_WIDTH)), const((1, C_WIDTH)), const((1, C_WIDTH)),
            pl.BlockSpec(memory_space=pl.ANY),
        ],
        out_specs=blk(0),
        out_shape=jax.ShapeDtypeStruct((BATCH, T_ALL, C_WIDTH), BF16),
        input_output_aliases={8: 0},
        compiler_params=_cparams(2),
    )(u, u, u, h_odd, dw_w, dw_b.reshape(1, C_WIDTH), ln_g.reshape(1, C_WIDTH), ln_b.reshape(1, C_WIDTH), ya)


TILE_GRID_ROWS = ROW_TILE // GRID_W
BAND_GRID_ROWS = BAND_D // GRID_W
N_DR = 2 * NA_KH - 1
N_DC = 2 * NA_KW - 1
PAIR_ENTRIES = 17
PAIR_RIGHT_ONLY, PAIR_LEFT_ONLY = 15, 16


INTERIOR_FIRST_DR = NA_KH // 2 - 1
INTERIOR_LAST_DR = INTERIOR_FIRST_DR + NA_KH - 1


def _pair_halves(entry):
    if entry == 0:
        return None, None
    if entry == PAIR_RIGHT_ONLY:
        return None, INTERIOR_FIRST_DR
    if entry == PAIR_LEFT_ONLY:
        return INTERIOR_LAST_DR, None
    return entry - 1, entry


def _pair_table_kernel(ext_ref, o_ref):
    qc = lax.broadcasted_iota(jnp.int32, (GRID_W, LANES), 0)
    lane = lax.broadcasted_iota(jnp.int32, (GRID_W, LANES), 1)
    kc = jnp.bitwise_and(lane, GRID_W - 1)
    col_start = jnp.clip(qc - NA_KW // 2, 0, GRID_W - NA_KW)
    col_in = jnp.logical_and(kc >= col_start, kc < col_start + NA_KW)
    is_left = lane < GRID_W

    def half(dr, right):
        rows = jnp.broadcast_to(ext_ref[dr:dr + 1, :], (GRID_W, LANES))
        shift = 1 if right else LANES - GRID_W + 1
        return pltpu.roll(rows, shift, 1, stride=1, stride_axis=0)

    for entry in range(PAIR_ENTRIES):
        left_dr, right_dr = _pair_halves(entry)
        block = jnp.full((GRID_W, LANES), NEG_INF, F32)
        if left_dr is not None:
            block = jnp.where(jnp.logical_and(is_left, col_in), half(left_dr, False), block)
        if right_dr is not None:
            block = jnp.where(jnp.logical_and(jnp.logical_not(is_left), col_in), half(right_dr, True), block)
        o_ref[entry] = block


def _na_pair_table(rpb):
    lead = GRID_W - NA_KW
    ext = jnp.pad(rpb.astype(F32) * LOG2E, ((0, 0), (0, 0), (lead, LANES - N_DC - lead)), mode="edge")
    return pl.pallas_call(
        _pair_table_kernel,
        grid=(D_HEADS,),
        in_specs=[pl.BlockSpec((None, N_DR, LANES), lambda h: (h, 0, 0))],
        out_specs=pl.BlockSpec((None, PAIR_ENTRIES, GRID_W, LANES), lambda h: (h, 0, 0, 0)),
        out_shape=jax.ShapeDtypeStruct((D_HEADS, PAIR_ENTRIES, GRID_W, LANES), F32),
        compiler_params=_cparams(1),
    )(ext)


def _na_pair_entry(variant, a, p):
    if variant == "first":
        return 2 * p - a + NA_KH if p < NA_KH // 2 else 0
    if variant == "last":
        return 2 * p - a if p >= (BAND_GRID_ROWS - NA_KH) // 2 else 0
    left_ok = a <= 2 * p < a + NA_KH
    right_ok = a <= 2 * p + 1 < a + NA_KH
    if left_ok and right_ok:
        return 2 * p - a + NA_KH // 2
    if right_ok:
        return PAIR_RIGHT_ONLY
    if left_ok:
        return PAIR_LEFT_ONLY
    return 0


def _attn_d_kernel(first_tile, q_ref, g_ref, k_ref, v_ref, tab_ref, o_ref):
    t = pl.program_id(1) + first_tile

    def normalise(acc):
        return (acc / pltpu.roll(acc, HEAD_DIM, 1))[:, 0:HEAD_DIM]

    def ctx_tile():
        out = _PairStore(o_ref, g_ref)

        def scores(h):
            lanes = slice(h * HEAD_DIM, (h + 1) * HEAD_DIM)
            return [lax.dot_general(q_ref[:, lanes], k_ref[0:CTX_LEN, lanes], NT_DIMS, preferred_element_type=F32)]

        def finish(h, s_list):
            acc, _ = _softmax_pv(s_list, [v_ref[0:CTX_LEN, h * LANES:(h + 1) * LANES]])
            out.put(h, normalise(acc))

        _pipelined(D_HEADS, scores, finish, depth=CTX_PIPELINE_DEPTH)

    def lat_tile():
        out = _PairStore(o_ref, g_ref)
        start = pl.multiple_of(
            jnp.clip((t - 1) * ROW_TILE - (NA_KH // 2) * GRID_W, 0, SEQ - BAND_D) + CTX_LEN, LANES)
        is_first = t == 1
        is_last = t == N_TILES - 1

        def bias(h):
            rows = []
            for a in range(TILE_GRID_ROWS):
                blocks = []
                for p in range(BAND_GRID_ROWS // 2):
                    entry = jnp.where(is_first, _na_pair_entry("first", a, p),
                                      jnp.where(is_last, _na_pair_entry("last", a, p),
                                                _na_pair_entry("interior", a, p)))
                    blocks.append(tab_ref[h, entry])
                rows.append(jnp.concatenate(blocks, axis=1))
            return jnp.concatenate(rows, axis=0)

        def scores(h):
            lanes = slice(h * HEAD_DIM, (h + 1) * HEAD_DIM)
            q = q_ref[:, lanes]
            kb = k_ref[pl.ds(start, BAND_D), :][:, lanes]
            s_ctx = lax.dot_general(q, k_ref[0:CTX_LEN, lanes], NT_DIMS, preferred_element_type=F32)
            s_band = lax.dot_general(q, kb, NT_DIMS, preferred_element_type=F32) + bias(h)
            return [s_ctx, s_band]

        def finish(h, s_list):
            v_lanes = slice(h * LANES, (h + 1) * LANES)
            v_list = [v_ref[0:CTX_LEN, v_lanes], v_ref[pl.ds(start, BAND_D), :][:, v_lanes]]
            acc, _ = _softmax_pv(s_list, v_list)
            out.put(h, normalise(acc))

        _pipelined(D_HEADS, scores, finish)

    if first_tile == 0:
        pl.when(t == 0)(ctx_tile)
        pl.when(t > 0)(lat_tile)
    else:
        lat_tile()


def _attn_d(h_odd, pair_tab, lat_only):
    w = D_HEADS * HEAD_DIM
    f = 1 if lat_only else 0
    return pl.pallas_call(
        functools.partial(_attn_d_kernel, f),
        grid=(BATCH, N_TILES - f),
        in_specs=[
            pl.BlockSpec((None, ROW_TILE, w), lambda b, t: (b, t + f, OD_DQ // w)),
            pl.BlockSpec((None, ROW_TILE, w), lambda b, t: (b, t + f, OD_DG // w)),
            pl.BlockSpec((None, T_ALL, w), lambda b, t: (b, 0, OD_DK // w)),
            pl.BlockSpec((None, T_ALL, 2 * w), lambda b, t: (b, 0, OD_DV // (2 * w))),
            pl.BlockSpec((D_HEADS, PAIR_ENTRIES, GRID_W, LANES), lambda b, t: (0, 0, 0, 0)),
        ],
        out_specs=pl.BlockSpec((None, ROW_TILE, w), lambda b, t: (b, t, 0)),
        out_shape=jax.ShapeDtypeStruct((BATCH, T_ALL - f * CTX_LEN, w), BF16),
        compiler_params=_cparams(2),
    )(h_odd, h_odd, h_odd, h_odd, pair_tab)


FINAL_TILE = 2 * ROW_TILE


def _outproj_final_kernel(ya_lo_ref, ya_hi_ref, yb_ref, w_ref, x_lo_ref, x_hi_ref, mod_ref, fg_ref, o_ref):
    half = w_ref.shape[0] // 2
    ya = jnp.concatenate([ya_lo_ref[...], ya_hi_ref[...]], axis=0)
    y = jnp.dot(ya, w_ref[0:half, :], preferred_element_type=F32)
    y = y + jnp.dot(yb_ref[...], w_ref[half:, :], preferred_element_type=F32)
    x = jnp.concatenate([x_lo_ref[...], x_hi_ref[...]], axis=0) + mod_ref[:, 2 * D_MODEL:3 * D_MODEL] * y
    ms = jnp.mean(x * x, axis=-1, keepdims=True)
    o_ref[...] = x * lax.rsqrt(ms + EPS) * fg_ref[...]


def _outproj_final(ya, yb, w_out, xs, mods, layer, final_g):
    half = w_out.shape[0] // 2
    per = FINAL_TILE // ROW_TILE
    blk = lambda width, off: pl.BlockSpec((None, ROW_TILE, width), lambda b, t: (b, per * t + 1 + off, 0))
    return pl.pallas_call(
        _outproj_final_kernel,
        grid=(BATCH, SEQ // FINAL_TILE),
        in_specs=[
            blk(half, 0), blk(half, 1),
            pl.BlockSpec((None, FINAL_TILE, half), lambda b, t: (b, t, 0)),
            pl.BlockSpec((2 * half, D_MODEL), lambda b, t: (0, 0)),
            blk(D_MODEL, 0), blk(D_MODEL, 1),
            pl.BlockSpec((None, 1, 3 * D_MODEL), lambda b, t: (layer * MOD_ROWS + b, 0, 0)),
            pl.BlockSpec((1, D_MODEL), lambda b, t: (0, 0)),
        ],
        out_specs=pl.BlockSpec((None, FINAL_TILE, D_MODEL), lambda b, t: (b, t, 0)),
        out_shape=jax.ShapeDtypeStruct((BATCH, SEQ, D_MODEL), F32),
        compiler_params=_cparams(2),
    )(ya, ya, yb, w_out.astype(BF16), xs, xs, mods, final_g.reshape(1, D_MODEL))


def kernel(x, c, ctx, c_ctx, mod_w, mod_b, norm_g, ev_w_in, ev_w_out, a_q_gain, a_k_gain, b_sink,
           od_w_in, od_w_out, c_dw_w, c_dw_b, c_ln_g, c_ln_b, d_rpb, final_g):
    cond = jnp.concatenate(
        [c, c_ctx[None, :], jnp.zeros((MOD_ROWS - BATCH - 1, D_MODEL), F32)], axis=0)
    mods = _modulation_all(cond, mod_w, mod_b).reshape(DEPTH * MOD_ROWS, 1, 3 * D_MODEL)
    tables = _rope_tables()
    head_of_lane = np.arange(LANES) // HEAD_DIM
    bd = jnp.asarray((head_of_lane[:, None] == head_of_lane[None, :]).astype(np.float32) / HEAD_DIM, BF16)

    xs = None
    pending = None
    for i in range(DEPTH):
        j = i // 2
        src = dict(x=x, ctx=ctx) if xs is None else dict(xs=xs)
        if i % 2 == 0:
            extra = (a_q_gain[j], a_k_gain[j], tables, bd)
            stream, h_even = _proj("even", i, norm_g[i], ev_w_in[j], mods, out=pending, even_extra=extra, **src)
            pending = (_attn_a(h_even), _attn_b(h_even, b_sink[j]), ev_w_out[j], i)
        else:
            conv = (c_dw_w[j], c_dw_b[j], c_ln_g[j], c_ln_b[j])
            stream, h_odd, u, ya = _proj("odd", i, norm_g[i], od_w_in[j], mods, out=pending, conv=conv, **src)
            ya = _conv_fixup(ya, u, h_odd, *conv)
            last = i == DEPTH - 1
            pending = (ya, _attn_d(h_odd, _na_pair_table(d_rpb[j]), lat_only=last), od_w_out[j], i)
        if stream is not None:
            xs = stream
    ya, yb, w_out, layer = pending
    return _outproj_final(ya, yb, w_out, xs, mods, layer, final_g)
```

```python
import functools

import numpy as np
import jax
import jax.numpy as jnp
from jax import lax
from jax.experimental import pallas as pl
from jax.experimental.pallas import tpu as pltpu

D_MODEL = 1024
BATCH = 8
SEQ = 2048
DEPTH = 4
GRID_W = 64
CTX_LEN = 256
HEAD_DIM = 64
ROPE_THETA = 10000.0
EPS = 1e-6
NEG_INF = -1e30
ATTN_SCALE = HEAD_DIM ** -0.5
A_HEADS = 8
A_KV_HEADS = 2
B_HEADS = 8
B_KV_HEADS = 2
B_WINDOW = 128
C_WIDTH = 512
C_CONV = 31
D_HEADS = 8
NA_KH = 8
NA_KW = 16
ROWS = SEQ // GRID_W

T_ALL = CTX_LEN + SEQ
ROW_TILE = 256
N_TILES = T_ALL // ROW_TILE
PROJ_TILE = 3 * ROW_TILE
PROJ_TILES = T_ALL // PROJ_TILE
LANES = 128
MXU_N = 256
MOD_ROWS = 16
CTX_MOD_ROW = BATCH
VMEM_LIMIT = 48 * 1024 * 1024
PROJ_VMEM_LIMIT = 56 * 1024 * 1024

EVEN_IN = 2560
ODD_IN = 3584
LOG2E = 1.4426950408889634
Q_SCALE = ATTN_SCALE * LOG2E
STACK = 1
CTX_PIPELINE_DEPTH = 8
PIPELINE_DEPTH = 2
HALO = 16
BAND_B = 512
BAND_D = 768

F32 = jnp.float32
BF16 = jnp.bfloat16
NT_DIMS = (((1,), (1,)), ((), ()))


def _cparams(n_grid, vmem=VMEM_LIMIT):
    return pltpu.CompilerParams(dimension_semantics=("arbitrary",) * n_grid, vmem_limit_bytes=vmem)


def _silu(v):
    return v * jax.nn.sigmoid(v)


def _mod_kernel(cond_ref, w_ref, b_ref, o_ref):
    cnd = cond_ref[...]
    act = _silu(cnd).astype(BF16)
    o_ref[...] = jnp.dot(act, w_ref[...].astype(BF16), preferred_element_type=F32) + b_ref[...]


def _modulation_all(cond, mod_w, mod_b):
    n_col = 3
    return pl.pallas_call(
        _mod_kernel,
        grid=(DEPTH, n_col),
        in_specs=[
            pl.BlockSpec((MOD_ROWS, D_MODEL), lambda i, j: (0, 0)),
            pl.BlockSpec((None, D_MODEL, D_MODEL), lambda i, j: (i, 0, j)),
            pl.BlockSpec((None, 1, D_MODEL), lambda i, j: (i, 0, j)),
        ],
        out_specs=pl.BlockSpec((None, MOD_ROWS, D_MODEL), lambda i, j: (i, 0, j)),
        out_shape=jax.ShapeDtypeStruct((DEPTH, MOD_ROWS, 3 * D_MODEL), F32),
        compiler_params=_cparams(2),
    )(cond, mod_w, mod_b.reshape(DEPTH, 1, 3 * D_MODEL))


EVEN_OUT = 2816
EV_AQ, EV_AG, EV_BQ, EV_BG, EV_AV, EV_BV, EV_AK, EV_BK = 0, 512, 1024, 1536, 2048, 2304, 2560, 2688

ODD_OUT = 3072
OD_CG, OD_DQ, OD_DK, OD_DG, OD_DV = 0, 512, 1024, 1536, 2048


def _even_pieces():
    pieces = []
    for j in range(4):
        pieces.append((0 + LANES * j, "aq", EV_AQ + LANES * j))
    pieces.append((512, "ak", EV_AK))
    pieces.append((640, "v", EV_AV))
    for j in range(4):
        pieces.append((768 + LANES * j, "gate", EV_AG + LANES * j))
    for j in range(4):
        pieces.append((1280 + LANES * j, "bq", EV_BQ + LANES * j))
    pieces.append((1792, "bk", EV_BK))
    pieces.append((1920, "v", EV_BV))
    for j in range(4):
        pieces.append((2048 + LANES * j, "gate", EV_BG + LANES * j))
    return pieces


def _rope(v, cos, sin_even, sin_odd):
    nxt = pltpu.roll(v, LANES - 1, 1)
    prv = pltpu.roll(v, 1, 1)
    return v * cos + nxt * sin_even + prv * sin_odd


def _head_rms(v, gain, bd_ref):
    ms = jnp.dot((v * v).astype(BF16), bd_ref[...], preferred_element_type=F32)
    return v * lax.rsqrt(ms + EPS) * gain


def _even_epilogue(h_slabs, w_ref, qg_ref, kg_ref, cos_ref, se_ref, so_ref, bd_ref, o_ref):
    cos, s_even, s_odd = cos_ref[...], se_ref[...], so_ref[...]
    pieces = _even_pieces()
    ones = jnp.ones((PROJ_TILE, LANES), BF16)
    n_chunks = EVEN_IN // MXU_N

    h = jnp.concatenate(h_slabs, axis=0)

    def proj(c):
        return jnp.dot(h, w_ref[:, c * MXU_N:(c + 1) * MXU_N], preferred_element_type=F32)

    acc_next = proj(0)
    for c in range(n_chunks):
        acc = acc_next
        if c + 1 < n_chunks:
            acc_next = proj(c + 1)
        for half in range(MXU_N // LANES):
            src = c * MXU_N + half * LANES
            (_, kind, dst), = [p for p in pieces if p[0] == src]
            v = acc[:, half * LANES:(half + 1) * LANES]
            if kind == "aq":
                v = _rope(_head_rms(v, qg_ref[...], bd_ref), cos, s_even, s_odd) * Q_SCALE
            elif kind == "ak":
                v = _rope(_head_rms(v, kg_ref[...], bd_ref), cos, s_even, s_odd)
            elif kind == "bq":
                v = _rope(v, cos, s_even, s_odd) * Q_SCALE
            elif kind == "bk":
                v = _rope(v, cos, s_even, s_odd)
            elif kind == "gate":
                v = _silu(v)
            elif kind == "v":
                o_ref[:, dst + LANES:dst + 2 * LANES] = ones
            o_ref[:, dst:dst + LANES] = v.astype(BF16)


def _odd_epilogue(t, h_slabs, w_ref, conv_refs, o_ref, u_ref, yc_ref):
    h = jnp.concatenate(h_slabs, axis=0)
    dw_ref, db_ref, lg_ref, lb_ref = conv_refs
    n_slabs = PROJ_TILE // ROW_TILE
    n_cc = C_WIDTH // MXU_N

    def proj(col):
        return jnp.dot(h, w_ref[:, col:col + MXU_N], preferred_element_type=F32)

    ones = jnp.ones((PROJ_TILE, HEAD_DIM), F32)
    heads_per_chunk = MXU_N // HEAD_DIM
    held, u_val, sg_val, conv_acc, windows = {}, {}, {}, {}, {}

    def window(s, c):
        zeros = jnp.zeros((HALO, MXU_N), F32)
        lo, hi = s * ROW_TILE, (s + 1) * ROW_TILE
        left = u_val[c][lo - HALO:lo] if s > 0 else zeros
        right = u_val[c][hi:hi + HALO] if s < n_slabs - 1 else zeros
        if s == 1:
            left = jnp.where(t == 0, zeros, left)
        if s == 0:
            right = jnp.where(t == 0, zeros, right)
        return jnp.concatenate([left, u_val[c][lo:hi], right], axis=0)

    def conv_piece(s, c, r):
        lanes = slice(c * MXU_N, (c + 1) * MXU_N)
        n_win = ROW_TILE + 2 * HALO
        if (s, c) not in windows:
            windows[(s, c)] = window(s, c)
        win = windows[(s, c)]
        shifted = win if r == 0 else pltpu.roll(win, n_win - r, 0)
        acc = conv_acc.get((s, c))
        if acc is None:
            acc = jnp.zeros((ROW_TILE, MXU_N), F32) + db_ref[:, lanes]
        for k in range(C_CONV):
            off = HALO - C_CONV // 2 + k
            if off % CONV_GROUPS == r:
                base = off - r
                acc = acc + shifted[base:base + ROW_TILE, :] * dw_ref[k:k + 1, lanes]
        conv_acc[(s, c)] = acc

    def conv_finish(s):
        acc = jnp.concatenate([conv_acc.pop((s, c)) for c in range(n_cc)], axis=1)
        sg = jnp.concatenate([sg_val[c][s * ROW_TILE:(s + 1) * ROW_TILE] for c in range(n_cc)], axis=1)
        yc_ref[s * ROW_TILE:(s + 1) * ROW_TILE, :] = _conv_tail(acc, sg, lg_ref, lb_ref)

    conv_todo = [("piece", s, c, r) for s in range(n_slabs) for c in range(n_cc) for r in range(CONV_GROUPS)]
    for s in range(n_slabs):
        conv_todo.insert((s + 1) * n_cc * CONV_GROUPS + s, ("finish", s))

    def finish(kind, c, v):
        lanes = slice(c * MXU_N, (c + 1) * MXU_N)
        if kind == "val":
            held[c] = v
        elif kind == "glu":
            u_val[c] = held.pop(c) * jax.nn.sigmoid(v)
            u_ref[:, lanes] = u_val[c]
        elif kind == "v":
            for i in range(heads_per_chunk):
                head = c * heads_per_chunk + i
                ext = jnp.concatenate([v[:, i * HEAD_DIM:(i + 1) * HEAD_DIM], ones], axis=1)
                o_ref[:, OD_DV + head * LANES:OD_DV + (head + 1) * LANES] = ext.astype(BF16)
        else:
            dst = {"cg": OD_CG, "q": OD_DQ, "k": OD_DK, "dg": OD_DG}[kind]
            if kind in ("cg", "dg"):
                v = _silu(v)
                if kind == "cg":
                    sg_val[c] = v
            elif kind == "q":
                v = v * Q_SCALE
            o_ref[:, dst + c * MXU_N:dst + (c + 1) * MXU_N] = v.astype(BF16)

    order = [("val", 0), ("glu", 1), ("cg", 2), ("q", 3), ("k", 4), ("v", 5), ("dg", 6)]
    steps = [(kind, c, g * C_WIDTH + c * MXU_N) for c in range(C_WIDTH // MXU_N) for kind, g in order[:2]]
    steps += [(kind, c, g * C_WIDTH + c * MXU_N) for kind, g in order[2:] for c in range(C_WIDTH // MXU_N)]
    conv_from = 4
    per_step = -(-len(conv_todo) // (len(steps) - conv_from))
    acc_next = proj(steps[0][2])
    for i, (kind, c, _) in enumerate(steps):
        acc = acc_next
        if i + 1 < len(steps):
            acc_next = proj(steps[i + 1][2])
        finish(kind, c, acc)
        if i >= conv_from:
            for item in conv_todo[(i - conv_from) * per_step:(i - conv_from + 1) * per_step]:
                if item[0] == "piece":
                    conv_piece(*item[1:])
                else:
                    conv_finish(item[1])


def _make_proj_kernel(first, has_out, in_kind):
    def kern(*refs):
        it = iter(refs)
        t = pl.program_id(1)
        if first:
            ctx_ref, xa_ref, xb_ref, xc_ref = next(it), next(it), next(it), next(it)
        else:
            x_ref = next(it)
        if has_out:
            ya_ref, yb_ref, wo_ref, cur_ref, cur_ctx_ref = (next(it) for _ in range(5))
        norm_ref, nxt_ref, nxt_ctx_ref, wi_ref = (next(it) for _ in range(4))
        if in_kind == "even":
            extra = [next(it) for _ in range(6)]
        else:
            conv_refs = [next(it) for _ in range(4)]
        if has_out:
            xo_ref = next(it)
        o_ref = next(it)
        if in_kind == "odd":
            u_ref, yc_ref = next(it), next(it)

        h_slabs = []
        for s in range(PROJ_TILE // ROW_TILE):
            rows = slice(s * ROW_TILE, (s + 1) * ROW_TILE)
            is_ctx = jnp.logical_and(t == 0, s == 0)
            if first:
                x = jnp.where(t == 0, ctx_ref[...], xa_ref[...]) if s == 0 else (xb_ref, xc_ref)[s - 1][...]
            else:
                x = x_ref[rows, :]
            if has_out:
                half = wo_ref.shape[0] // 2
                y = jnp.dot(ya_ref[rows, :], wo_ref[0:half, :], preferred_element_type=F32)
                y = y + jnp.dot(yb_ref[rows, :], wo_ref[half:, :], preferred_element_type=F32)
                cur = jnp.where(is_ctx, cur_ctx_ref[...], cur_ref[...]) if s == 0 else cur_ref[...]
                x = x + cur[:, 2 * D_MODEL:3 * D_MODEL] * y
                xo_ref[rows, :] = x
            nxt = jnp.where(is_ctx, nxt_ctx_ref[...], nxt_ref[...]) if s == 0 else nxt_ref[...]
            ms = jnp.mean(x * x, axis=-1, keepdims=True)
            normed = x * lax.rsqrt(ms + EPS) * norm_ref[...]
            h_slabs.append((normed * (1.0 + nxt[:, D_MODEL:2 * D_MODEL]) + nxt[:, 0:D_MODEL]).astype(BF16))
        if in_kind == "even":
            _even_epilogue(h_slabs, wi_ref, *extra, o_ref)
        else:
            _odd_epilogue(t, h_slabs, wi_ref, conv_refs, o_ref, u_ref, yc_ref)

    return kern


def _proj(in_kind, layer_in, norm_g, w_in, mods, *, x=None, ctx=None, xs=None, out=None, even_extra=None,
          conv=None):
    first = xs is None
    has_out = out is not None
    const = lambda shape: pl.BlockSpec(shape, lambda b, t: (0,) * len(shape), pipeline_mode=pl.Buffered(1))
    layer_of = lambda stack, j: pl.BlockSpec((None,) + stack.shape[1:], lambda b, t: (j, 0, 0),
                                             pipeline_mode=pl.Buffered(1))
    tile = lambda width: pl.BlockSpec((None, PROJ_TILE, width), lambda b, t: (b, t, 0))
    mod_sample = lambda layer: pl.BlockSpec((None, 1, 3 * D_MODEL), lambda b, t: (layer * MOD_ROWS + b, 0, 0))
    mod_ctx = lambda layer: pl.BlockSpec((None, 1, 3 * D_MODEL),
                                         lambda b, t: (layer * MOD_ROWS + CTX_MOD_ROW, 0, 0))
    args, specs = [], []
    if first:
        per = PROJ_TILE // ROW_TILE
        blk = lambda off: pl.BlockSpec((None, ROW_TILE, D_MODEL),
                                       lambda b, t: (b, jnp.maximum(per * t + off, 0), 0))
        args += [ctx, x, x, x]
        specs += [pl.BlockSpec((None, CTX_LEN, D_MODEL), lambda b, t: (b, 0, 0)), blk(-1), blk(0), blk(1)]
    else:
        args.append(xs)
        specs.append(tile(D_MODEL))
    aliases = {}
    if has_out:
        ya, yb, (w_out_stack, j_out), layer_out = out
        half = w_out_stack.shape[1] // 2
        args += [ya, yb, w_out_stack, mods, mods]
        specs += [tile(half), tile(half), layer_of(w_out_stack, j_out), mod_sample(layer_out), mod_ctx(layer_out)]
        if not first:
            aliases = {0: 0}
    w_in_stack, j_in = w_in
    args += [norm_g.reshape(1, D_MODEL), mods, mods, w_in_stack]
    specs += [const((1, D_MODEL)), mod_sample(layer_in), mod_ctx(layer_in), layer_of(w_in_stack, j_in)]
    if in_kind == "even":
        q_gain, k_gain, (cos, s_even, s_odd), bd = even_extra
        tab = pl.BlockSpec((PROJ_TILE, LANES), lambda b, t: (t, 0))
        args += [jnp.tile(q_gain, LANES // HEAD_DIM).reshape(1, LANES),
                 jnp.tile(k_gain, LANES // HEAD_DIM).reshape(1, LANES), cos, s_even, s_odd, bd]
        specs += [const((1, LANES)), const((1, LANES)), tab, tab, tab, const((LANES, LANES))]
    else:
        dw_w, dw_b, ln_g, ln_b = conv
        args += [dw_w, dw_b.reshape(1, C_WIDTH), ln_g.reshape(1, C_WIDTH), ln_b.reshape(1, C_WIDTH)]
        specs += [const((C_CONV, C_WIDTH)), const((1, C_WIDTH)), const((1, C_WIDTH)), const((1, C_WIDTH))]
    out_shapes, out_specs = [], []
    if has_out:
        out_shapes.append(jax.ShapeDtypeStruct((BATCH, T_ALL, D_MODEL), F32))
        out_specs.append(tile(D_MODEL))
    if in_kind == "even":
        out_shapes.append(jax.ShapeDtypeStruct((BATCH, T_ALL, EVEN_OUT), BF16))
        out_specs.append(tile(EVEN_OUT))
    else:
        out_shapes += [jax.ShapeDtypeStruct((BATCH, T_ALL, ODD_OUT), BF16),
                       jax.ShapeDtypeStruct((BATCH, T_ALL, C_WIDTH), F32),
                       jax.ShapeDtypeStruct((BATCH, T_ALL, C_WIDTH), BF16)]
        out_specs += [tile(ODD_OUT), tile(C_WIDTH), tile(C_WIDTH)]
    res = pl.pallas_call(
        _make_proj_kernel(first, has_out, in_kind),
        grid=(BATCH, PROJ_TILES),
        in_specs=specs,
        out_specs=out_specs,
        out_shape=out_shapes,
        input_output_aliases=aliases,
        compiler_params=_cparams(2, PROJ_VMEM_LIMIT),
    )(*args)
    res = list(res)
    stream = res.pop(0) if has_out else None
    return (stream, *res)


def _rope_tables():
    t = np.arange(SEQ)
    row = (t // GRID_W).astype(np.float64)
    col = (t % GRID_W).astype(np.float64)
    half = HEAD_DIM // 2
    freqs = (np.float32(ROPE_THETA) ** (-np.arange(0, half, 2, dtype=np.float32) / half)).astype(np.float64)
    ang = np.concatenate([row[:, None] * freqs, col[:, None] * freqs], axis=-1).astype(np.float32)
    cos = np.repeat(np.cos(ang.astype(np.float64)), 2, axis=-1)
    sin = np.repeat(np.sin(ang.astype(np.float64)), 2, axis=-1)
    lane_even = (np.arange(HEAD_DIM) % 2 == 0)
    s_even = np.where(lane_even, -sin, 0.0)
    s_odd = np.where(lane_even, 0.0, sin)

    def full(tab, ctx_val):
        tab = np.concatenate([np.full((CTX_LEN, HEAD_DIM), ctx_val), tab], axis=0).astype(np.float32)
        return jnp.asarray(np.tile(tab, (1, LANES // HEAD_DIM)))

    return full(cos, 1.0), full(s_even, 0.0), full(s_odd, 0.0)


def _softmax_pv(s_list, v_list, sink=None):
    m = s_list[0].max(axis=-1, keepdims=True)
    for s in s_list[1:]:
        m = jnp.maximum(m, s.max(axis=-1, keepdims=True))
    if sink is not None:
        m = jnp.maximum(m, sink)
    acc = None
    for s, v in zip(s_list, v_list):
        p = jnp.exp2(s - m).astype(BF16)
        part = jnp.dot(p, v, preferred_element_type=F32)
        acc = part if acc is None else acc + part
    extra = jnp.exp2(sink - m) if sink is not None else None
    return acc, extra


def _pipelined(n_heads, scores_fn, finish_fn, depth=PIPELINE_DEPTH):
    ahead = [scores_fn(h) for h in range(min(depth, n_heads))]
    for h in range(n_heads):
        if h + depth < n_heads:
            ahead.append(scores_fn(h + depth))
        finish_fn(h, ahead.pop(0))


class _PairStore:
    def __init__(self, o_ref, g_ref):
        self.o_ref, self.g_ref, self.held = o_ref, g_ref, None

    def put(self, h, o):
        if h % 2 == 0:
            self.held = o
            return
        lanes = slice((h - 1) * HEAD_DIM, (h + 1) * HEAD_DIM)
        pair = jnp.concatenate([self.held, o], axis=1)
        self.o_ref[:, lanes] = (pair * self.g_ref[:, lanes].astype(F32)).astype(BF16)


def _gqa_normalise(acc, extra, kv):
    den = acc[:, LANES:2 * LANES]
    if extra is not None:
        den = den + extra
    o = acc[:, 0:LANES] / den
    return o[:, kv * HEAD_DIM:(kv + 1) * HEAD_DIM]


def _attn_a_kernel(q_ref, g_ref, k_ref, v_ref, o_ref):
    group = A_HEADS // A_KV_HEADS

    def run(n_keys):
        out = _PairStore(o_ref, g_ref)

        def scores(h):
            kv = h // group
            q = q_ref[:, h * HEAD_DIM:(h + 1) * HEAD_DIM]
            k = k_ref[0:n_keys, kv * HEAD_DIM:(kv + 1) * HEAD_DIM]
            return [lax.dot_general(q, k, NT_DIMS, preferred_element_type=F32)]

        def finish(h, s_list):
            acc, extra = _softmax_pv(s_list, [v_ref[0:n_keys, :]])
            out.put(h, _gqa_normalise(acc, extra, h // group))

        _pipelined(A_HEADS, scores, finish, depth=CTX_PIPELINE_DEPTH if n_keys == CTX_LEN else PIPELINE_DEPTH)

    t = pl.program_id(1)
    pl.when(t == 0)(lambda: run(CTX_LEN))
    pl.when(t > 0)(lambda: run(T_ALL))


def _attn_a(h_even):
    w = A_HEADS * HEAD_DIM
    return pl.pallas_call(
        _attn_a_kernel,
        grid=(BATCH, N_TILES),
        in_specs=[
            pl.BlockSpec((None, ROW_TILE, w), lambda b, t: (b, t, EV_AQ // w)),
            pl.BlockSpec((None, ROW_TILE, w), lambda b, t: (b, t, EV_AG // w)),
            pl.BlockSpec((None, T_ALL, LANES), lambda b, t: (b, 0, EV_AK // LANES)),
            pl.BlockSpec((None, T_ALL, 2 * LANES), lambda b, t: (b, 0, EV_AV // (2 * LANES))),
        ],
        out_specs=pl.BlockSpec((None, ROW_TILE, w), lambda b, t: (b, t, 0)),
        out_shape=jax.ShapeDtypeStruct((BATCH, T_ALL, w), BF16),
        compiler_params=_cparams(2),
    )(h_even, h_even, h_even, h_even)


def _attn_b_kernel(sink_ref, q_ref, g_ref, k_ref, v_ref, o_ref):
    group = B_HEADS // B_KV_HEADS
    t = pl.program_id(1)

    def run(row_spans, band_bias):
        out = _PairStore(o_ref, g_ref)
        k_all = jnp.concatenate([k_ref[pl.ds(s, n), :] for s, n in row_spans], axis=0)
        v_all = jnp.concatenate([v_ref[pl.ds(s, n), :] for s, n in row_spans], axis=0)
        k_kv = [k_all[:, kv * HEAD_DIM:(kv + 1) * HEAD_DIM] for kv in range(B_KV_HEADS)]

        def heads_of(u):
            return list(range(u * STACK, (u + 1) * STACK))

        def scores(u):
            heads = heads_of(u)
            q = jnp.concatenate([q_ref[:, h * HEAD_DIM:(h + 1) * HEAD_DIM] for h in heads], axis=0)
            s = lax.dot_general(q, k_kv[heads[0] // group], NT_DIMS, preferred_element_type=F32)
            if band_bias is None:
                return s
            slabs = []
            for i in range(STACK):
                slab = s[i * ROW_TILE:(i + 1) * ROW_TILE]
                slabs.append(jnp.concatenate([slab[:, 0:CTX_LEN], slab[:, CTX_LEN:] + band_bias], axis=1))
            return jnp.concatenate(slabs, axis=0)

        def finish(u, s):
            heads = heads_of(u)
            kv = heads[0] // group
            sink = jnp.concatenate([jnp.full((ROW_TILE, 1), sink_ref[h] * LOG2E, F32) for h in heads], axis=0)
            m = jnp.maximum(s.max(axis=-1, keepdims=True), sink)
            p = jnp.exp2(s - m).astype(BF16)
            acc = jnp.dot(p, v_all, preferred_element_type=F32)
            den = acc[:, LANES:2 * LANES] + jnp.exp2(sink - m)
            o = (acc[:, 0:LANES] / den)[:, kv * HEAD_DIM:(kv + 1) * HEAD_DIM]
            for i, h in enumerate(heads):
                out.put(h, o[i * ROW_TILE:(i + 1) * ROW_TILE])

        _pipelined(B_HEADS // STACK, scores, finish,
                   depth=CTX_PIPELINE_DEPTH if band_bias is None else PIPELINE_DEPTH)

    def lat_tile():
        start = pl.multiple_of(jnp.clip(t * ROW_TILE - B_WINDOW, CTX_LEN, T_ALL - BAND_B), LANES)
        q_pos = t * ROW_TILE + lax.broadcasted_iota(jnp.int32, (ROW_TILE, BAND_B), 0)
        k_pos = start + lax.broadcasted_iota(jnp.int32, (ROW_TILE, BAND_B), 1)
        bias = jnp.where(jnp.abs(k_pos - q_pos) <= B_WINDOW, 0.0, NEG_INF).astype(F32)
        run([(0, CTX_LEN), (start, BAND_B)], bias)

    pl.when(t == 0)(lambda: run([(0, CTX_LEN)], None))
    pl.when(t > 0)(lat_tile)


def _attn_b(h_even, sink):
    w = B_HEADS * HEAD_DIM
    return pl.pallas_call(
        _attn_b_kernel,
        grid=(BATCH, N_TILES),
        in_specs=[
            pl.BlockSpec(memory_space=pltpu.SMEM),
            pl.BlockSpec((None, ROW_TILE, w), lambda b, t: (b, t, EV_BQ // w)),
            pl.BlockSpec((None, ROW_TILE, w), lambda b, t: (b, t, EV_BG // w)),
            pl.BlockSpec((None, T_ALL, LANES), lambda b, t: (b, 0, EV_BK // LANES)),
            pl.BlockSpec((None, T_ALL, 2 * LANES), lambda b, t: (b, 0, EV_BV // (2 * LANES))),
        ],
        out_specs=pl.BlockSpec((None, ROW_TILE, w), lambda b, t: (b, t, 0)),
        out_shape=jax.ShapeDtypeStruct((BATCH, T_ALL, w), BF16),
        compiler_params=_cparams(2),
    )(sink, h_even, h_even, h_even, h_even)


CONV_GROUPS = 8


def _conv_tail(acc, sg, lg_ref, lb_ref):
    mu = jnp.mean(acc, axis=-1, keepdims=True)
    ctr = acc - mu
    var = jnp.mean(ctr * ctr, axis=-1, keepdims=True)
    y = ctr * lax.rsqrt(var + EPS) * lg_ref[...] + lb_ref[...]
    return (_silu(y) * sg).astype(BF16)


def _conv_fixup_kernel(ul_ref, uc_ref, ur_ref, sg_ref, w_ref, b_ref, lg_ref, lb_ref, ya_ref, o_ref):
    del ya_ref
    n_win = 3 * HALO
    for b in range(BATCH):
        win = jnp.concatenate([ul_ref[b], uc_ref[b], ur_ref[b]], axis=0)
        acc = jnp.zeros((HALO, C_WIDTH), F32) + b_ref[...]
        for r in range(CONV_GROUPS):
            shifted = win if r == 0 else pltpu.roll(win, n_win - r, 0)
            for k in range(C_CONV):
                off = HALO - C_CONV // 2 + k
                if off % CONV_GROUPS == r:
                    base = off - r
                    acc = acc + shifted[base:base + HALO, :] * w_ref[k:k + 1, :]
        o_ref[b] = _conv_tail(acc, sg_ref[b].astype(F32), lg_ref, lb_ref)


def _conv_fixup(ya, u, h_odd, dw_w, dw_b, ln_g, ln_b):
    per_tile = PROJ_TILE // HALO
    edge_block = lambda k: (k // 2 + 1) * per_tile - 1 + k % 2
    blk = lambda off, col=0: pl.BlockSpec((BATCH, HALO, C_WIDTH), lambda k: (0, edge_block(k) + off, col))
    const = lambda shape: pl.BlockSpec(shape, lambda k: (0,) * len(shape))
    return pl.pallas_call(
        _conv_fixup_kernel,
        grid=(2 * (PROJ_TILES - 1),),
        in_specs=[
            blk(-1), blk(0), blk(1), blk(0, OD_CG // C_WIDTH),
            const((C_CONV, C_WIDTH)), const((1, C_WIDTH)), const((1, C_WIDTH)), const((1, C_WIDTH)),
            pl.BlockSpec(memory_space=pl.ANY),
        ],
        out_specs=blk(0),
        out_shape=jax.ShapeDtypeStruct((BATCH, T_ALL, C_WIDTH), BF16),
        input_output_aliases={8: 0},
        compiler_params=_cparams(1),
    )(u, u, u, h_odd, dw_w, dw_b.reshape(1, C_WIDTH), ln_g.reshape(1, C_WIDTH), ln_b.reshape(1, C_WIDTH), ya)


TILE_GRID_ROWS = ROW_TILE // GRID_W
BAND_GRID_ROWS = BAND_D // GRID_W
N_DR = 2 * NA_KH - 1
N_DC = 2 * NA_KW - 1
PAIR_ENTRIES = 17
PAIR_RIGHT_ONLY, PAIR_LEFT_ONLY = 15, 16


INTERIOR_FIRST_DR = NA_KH // 2 - 1
INTERIOR_LAST_DR = INTERIOR_FIRST_DR + NA_KH - 1


def _pair_halves(entry):
    if entry == 0:
        return None, None
    if entry == PAIR_RIGHT_ONLY:
        return None, INTERIOR_FIRST_DR
    if entry == PAIR_LEFT_ONLY:
        return INTERIOR_LAST_DR, None
    return entry - 1, entry


def _pair_table_kernel(ext_ref, o_ref):
    qc = lax.broadcasted_iota(jnp.int32, (GRID_W, LANES), 0)
    lane = lax.broadcasted_iota(jnp.int32, (GRID_W, LANES), 1)
    kc = jnp.bitwise_and(lane, GRID_W - 1)
    col_start = jnp.clip(qc - NA_KW // 2, 0, GRID_W - NA_KW)
    col_in = jnp.logical_and(kc >= col_start, kc < col_start + NA_KW)
    is_left = lane < GRID_W

    def half(dr, right):
        rows = jnp.broadcast_to(ext_ref[dr:dr + 1, :], (GRID_W, LANES))
        shift = 1 if right else LANES - GRID_W + 1
        return pltpu.roll(rows, shift, 1, stride=1, stride_axis=0)

    for entry in range(PAIR_ENTRIES):
        left_dr, right_dr = _pair_halves(entry)
        block = jnp.full((GRID_W, LANES), NEG_INF, F32)
        if left_dr is not None:
            block = jnp.where(jnp.logical_and(is_left, col_in), half(left_dr, False), block)
        if right_dr is not None:
            block = jnp.where(jnp.logical_and(jnp.logical_not(is_left), col_in), half(right_dr, True), block)
        o_ref[entry] = block


def _na_pair_table(rpb):
    lead = GRID_W - NA_KW
    ext = jnp.pad(rpb.astype(F32) * LOG2E, ((0, 0), (0, 0), (lead, LANES - N_DC - lead)), mode="edge")
    return pl.pallas_call(
        _pair_table_kernel,
        grid=(D_HEADS,),
        in_specs=[pl.BlockSpec((None, N_DR, LANES), lambda h: (h, 0, 0))],
        out_specs=pl.BlockSpec((None, PAIR_ENTRIES, GRID_W, LANES), lambda h: (h, 0, 0, 0)),
        out_shape=jax.ShapeDtypeStruct((D_HEADS, PAIR_ENTRIES, GRID_W, LANES), F32),
        compiler_params=_cparams(1),
    )(ext)


def _na_pair_entry(variant, a, p):
    if variant == "first":
        return 2 * p - a + NA_KH if p < NA_KH // 2 else 0
    if variant == "last":
        return 2 * p - a if p >= (BAND_GRID_ROWS - NA_KH) // 2 else 0
    left_ok = a <= 2 * p < a + NA_KH
    right_ok = a <= 2 * p + 1 < a + NA_KH
    if left_ok and right_ok:
        return 2 * p - a + NA_KH // 2
    if right_ok:
        return PAIR_RIGHT_ONLY
    if left_ok:
        return PAIR_LEFT_ONLY
    return 0


def _attn_d_kernel(first_tile, q_ref, g_ref, k_ref, v_ref, tab_ref, o_ref):
    t = pl.program_id(1) + first_tile

    def normalise(acc):
        return (acc / pltpu.roll(acc, HEAD_DIM, 1))[:, 0:HEAD_DIM]

    def ctx_tile():
        out = _PairStore(o_ref, g_ref)

        def scores(h):
            lanes = slice(h * HEAD_DIM, (h + 1) * HEAD_DIM)
            return [lax.dot_general(q_ref[:, lanes], k_ref[0:CTX_LEN, lanes], NT_DIMS, preferred_element_type=F32)]

        def finish(h, s_list):
            acc, _ = _softmax_pv(s_list, [v_ref[0:CTX_LEN, h * LANES:(h + 1) * LANES]])
            out.put(h, normalise(acc))

        _pipelined(D_HEADS, scores, finish, depth=CTX_PIPELINE_DEPTH)

    def lat_tile():
        out = _PairStore(o_ref, g_ref)
        start = pl.multiple_of(
            jnp.clip((t - 1) * ROW_TILE - (NA_KH // 2) * GRID_W, 0, SEQ - BAND_D) + CTX_LEN, LANES)
        is_first = t == 1
        is_last = t == N_TILES - 1

        def bias(h):
            rows = []
            for a in range(TILE_GRID_ROWS):
                blocks = []
                for p in range(BAND_GRID_ROWS // 2):
                    entry = jnp.where(is_first, _na_pair_entry("first", a, p),
                                      jnp.where(is_last, _na_pair_entry("last", a, p),
                                                _na_pair_entry("interior", a, p)))
                    blocks.append(tab_ref[h, entry])
                rows.append(jnp.concatenate(blocks, axis=1))
            return jnp.concatenate(rows, axis=0)

        def scores(h):
            lanes = slice(h * HEAD_DIM, (h + 1) * HEAD_DIM)
            q = q_ref[:, lanes]
            kb = k_ref[pl.ds(start, BAND_D), :][:, lanes]
            s_ctx = lax.dot_general(q, k_ref[0:CTX_LEN, lanes], NT_DIMS, preferred_element_type=F32)
            s_band = lax.dot_general(q, kb, NT_DIMS, preferred_element_type=F32) + bias(h)
            return [s_ctx, s_band]

        def finish(h, s_list):
            v_lanes = slice(h * LANES, (h + 1) * LANES)
            v_list = [v_ref[0:CTX_LEN, v_lanes], v_ref[pl.ds(start, BAND_D), :][:, v_lanes]]
            acc, _ = _softmax_pv(s_list, v_list)
            out.put(h, normalise(acc))

        _pipelined(D_HEADS, scores, finish)

    if first_tile == 0:
        pl.when(t == 0)(ctx_tile)
        pl.when(t > 0)(lat_tile)
    else:
        lat_tile()


def _attn_d(h_odd, pair_tab, lat_only):
    w = D_HEADS * HEAD_DIM
    f = 1 if lat_only else 0
    return pl.pallas_call(
        functools.partial(_attn_d_kernel, f),
        grid=(BATCH, N_TILES - f),
        in_specs=[
            pl.BlockSpec((None, ROW_TILE, w), lambda b, t: (b, t + f, OD_DQ // w)),
            pl.BlockSpec((None, ROW_TILE, w), lambda b, t: (b, t + f, OD_DG // w)),
            pl.BlockSpec((None, T_ALL, w), lambda b, t: (b, 0, OD_DK // w)),
            pl.BlockSpec((None, T_ALL, 2 * w), lambda b, t: (b, 0, OD_DV // (2 * w))),
            pl.BlockSpec((D_HEADS, PAIR_ENTRIES, GRID_W, LANES), lambda b, t: (0, 0, 0, 0)),
        ],
        out_specs=pl.BlockSpec((None, ROW_TILE, w), lambda b, t: (b, t, 0)),
        out_shape=jax.ShapeDtypeStruct((BATCH, T_ALL - f * CTX_LEN, w), BF16),
        compiler_params=_cparams(2),
    )(h_odd, h_odd, h_odd, h_odd, pair_tab)


FINAL_TILE = 2 * ROW_TILE


def _outproj_final_kernel(ya_lo_ref, ya_hi_ref, yb_ref, w_ref, x_lo_ref, x_hi_ref, mod_ref, fg_ref, o_ref):
    half = w_ref.shape[0] // 2
    ya = jnp.concatenate([ya_lo_ref[...], ya_hi_ref[...]], axis=0)
    y = jnp.dot(ya, w_ref[0:half, :], preferred_element_type=F32)
    y = y + jnp.dot(yb_ref[...], w_ref[half:, :], preferred_element_type=F32)
    x = jnp.concatenate([x_lo_ref[...], x_hi_ref[...]], axis=0) + mod_ref[:, 2 * D_MODEL:3 * D_MODEL] * y
    ms = jnp.mean(x * x, axis=-1, keepdims=True)
    o_ref[...] = x * lax.rsqrt(ms + EPS) * fg_ref[...]


def _outproj_final(ya, yb, w_out, xs, mods, layer, final_g):
    w_out_stack, j_out = w_out
    half = w_out_stack.shape[1] // 2
    per = FINAL_TILE // ROW_TILE
    blk = lambda width, off: pl.BlockSpec((None, ROW_TILE, width), lambda b, t: (b, per * t + 1 + off, 0))
    return pl.pallas_call(
        _outproj_final_kernel,
        grid=(BATCH, SEQ // FINAL_TILE),
        in_specs=[
            blk(half, 0), blk(half, 1),
            pl.BlockSpec((None, FINAL_TILE, half), lambda b, t: (b, t, 0)),
            pl.BlockSpec((None, 2 * half, D_MODEL), lambda b, t: (j_out, 0, 0)),
            blk(D_MODEL, 0), blk(D_MODEL, 1),
            pl.BlockSpec((None, 1, 3 * D_MODEL), lambda b, t: (layer * MOD_ROWS + b, 0, 0)),
            pl.BlockSpec((1, D_MODEL), lambda b, t: (0, 0)),
        ],
        out_specs=pl.BlockSpec((None, FINAL_TILE, D_MODEL), lambda b, t: (b, t, 0)),
        out_shape=jax.ShapeDtypeStruct((BATCH, SEQ, D_MODEL), F32),
        compiler_params=_cparams(2),
    )(ya, ya, yb, w_out_stack, xs, xs, mods, final_g.reshape(1, D_MODEL))


def kernel(x, c, ctx, c_ctx, mod_w, mod_b, norm_g, ev_w_in, ev_w_out, a_q_gain, a_k_gain, b_sink,
           od_w_in, od_w_out, c_dw_w, c_dw_b, c_ln_g, c_ln_b, d_rpb, final_g):
    cond = jnp.concatenate(
        [c, c_ctx[None, :], jnp.zeros((MOD_ROWS - BATCH - 1, D_MODEL), F32)], axis=0)
    mods = _modulation_all(cond, mod_w, mod_b).reshape(DEPTH * MOD_ROWS, 1, 3 * D_MODEL)
    tables = _rope_tables()
    head_of_lane = np.arange(LANES) // HEAD_DIM
    bd = jnp.asarray((head_of_lane[:, None] == head_of_lane[None, :]).astype(np.float32) / HEAD_DIM, BF16)
    ev_w_in_b, ev_w_out_b, od_w_in_b, od_w_out_b = (w.astype(BF16) for w in (ev_w_in, ev_w_out, od_w_in, od_w_out))

    xs = None
    pending = None
    for i in range(DEPTH):
        j = i // 2
        src = dict(x=x, ctx=ctx) if xs is None else dict(xs=xs)
        if i % 2 == 0:
            extra = (a_q_gain[j], a_k_gain[j], tables, bd)
            stream, h_even = _proj("even", i, norm_g[i], (ev_w_in_b, j), mods, out=pending, even_extra=extra, **src)
            pending = (_attn_a(h_even), _attn_b(h_even, b_sink[j]), (ev_w_out_b, j), i)
        else:
            conv = (c_dw_w[j], c_dw_b[j], c_ln_g[j], c_ln_b[j])
            stream, h_odd, u, ya = _proj("odd", i, norm_g[i], (od_w_in_b, j), mods, out=pending, conv=conv, **src)
            ya = _conv_fixup(ya, u, h_odd, *conv)
            last = i == DEPTH - 1
            pending = (ya, _attn_d(h_odd, _na_pair_table(d_rpb[j]), lat_only=last), (od_w_out_b, j), i)
        if stream is not None:
            xs = stream
    ya, yb, w_out, layer = pending
    return _outproj_final(ya, yb, w_out, xs, mods, layer, final_g)
```

```python
import functools

import numpy as np
import jax
import jax.numpy as jnp
from jax import lax
from jax.experimental import pallas as pl
from jax.experimental.pallas import tpu as pltpu

D_MODEL = 1024
BATCH = 8
SEQ = 2048
DEPTH = 4
GRID_W = 64
CTX_LEN = 256
HEAD_DIM = 64
ROPE_THETA = 10000.0
EPS = 1e-6
NEG_INF = -1e30
ATTN_SCALE = HEAD_DIM ** -0.5
A_HEADS = 8
A_KV_HEADS = 2
B_HEADS = 8
B_KV_HEADS = 2
B_WINDOW = 128
C_WIDTH = 512
C_CONV = 31
D_HEADS = 8
NA_KH = 8
NA_KW = 16
ROWS = SEQ // GRID_W

T_ALL = CTX_LEN + SEQ
ROW_TILE = 256
N_TILES = T_ALL // ROW_TILE
PROJ_TILE = 3 * ROW_TILE
PROJ_TILES = T_ALL // PROJ_TILE
LANES = 128
MXU_N = 256
MOD_ROWS = 16
CTX_MOD_ROW = BATCH
VMEM_LIMIT = 48 * 1024 * 1024
PROJ_VMEM_LIMIT = 56 * 1024 * 1024

EVEN_IN = 2560
ODD_IN = 3584
LOG2E = 1.4426950408889634
Q_SCALE = ATTN_SCALE * LOG2E
STACK = 1
CTX_PIPELINE_DEPTH = 8
B_PIPELINE_DEPTH = 3
PIPELINE_DEPTH = 2
HALO = 16
B_TILE = ROW_TILE
BAND_B = B_TILE + 2 * B_WINDOW
BAND_D = 768

F32 = jnp.float32
BF16 = jnp.bfloat16
NT_DIMS = (((1,), (1,)), ((), ()))


def _cparams(n_grid, vmem=VMEM_LIMIT):
    return pltpu.CompilerParams(dimension_semantics=("arbitrary",) * n_grid, vmem_limit_bytes=vmem)


def _silu(v):
    return v * jax.nn.sigmoid(v)


def _mod_kernel(cond_ref, w_ref, b_ref, o_ref):
    cnd = cond_ref[...]
    act = _silu(cnd).astype(BF16)
    o_ref[...] = jnp.dot(act, w_ref[...].astype(BF16), preferred_element_type=F32) + b_ref[...]


def _modulation_all(cond, mod_w, mod_b):
    n_col = 3
    return pl.pallas_call(
        _mod_kernel,
        grid=(DEPTH, n_col),
        in_specs=[
            pl.BlockSpec((MOD_ROWS, D_MODEL), lambda i, j: (0, 0)),
            pl.BlockSpec((None, D_MODEL, D_MODEL), lambda i, j: (i, 0, j)),
            pl.BlockSpec((None, 1, D_MODEL), lambda i, j: (i, 0, j)),
        ],
        out_specs=pl.BlockSpec((None, MOD_ROWS, D_MODEL), lambda i, j: (i, 0, j)),
        out_shape=jax.ShapeDtypeStruct((DEPTH, MOD_ROWS, 3 * D_MODEL), F32),
        compiler_params=_cparams(2),
    )(cond, mod_w, mod_b.reshape(DEPTH, 1, 3 * D_MODEL))


EVEN_OUT = 2816
EV_AQ, EV_AG, EV_BQ, EV_BG, EV_AV, EV_BV, EV_AK, EV_BK = 0, 512, 1024, 1536, 2048, 2304, 2560, 2688

ODD_OUT = 3072
OD_CG, OD_DQ, OD_DK, OD_DG, OD_DV = 0, 512, 1024, 1536, 2048


def _even_pieces():
    pieces = []
    for j in range(4):
        pieces.append((0 + LANES * j, "aq", EV_AQ + LANES * j))
    pieces.append((512, "ak", EV_AK))
    pieces.append((640, "v", EV_AV))
    for j in range(4):
        pieces.append((768 + LANES * j, "gate", EV_AG + LANES * j))
    for j in range(4):
        pieces.append((1280 + LANES * j, "bq", EV_BQ + LANES * j))
    pieces.append((1792, "bk", EV_BK))
    pieces.append((1920, "v", EV_BV))
    for j in range(4):
        pieces.append((2048 + LANES * j, "gate", EV_BG + LANES * j))
    return pieces


def _rope(v, cos, sin_even, sin_odd):
    nxt = pltpu.roll(v, LANES - 1, 1)
    prv = pltpu.roll(v, 1, 1)
    return v * cos + nxt * sin_even + prv * sin_odd


def _head_rms(v, gain, bd_ref):
    ms = jnp.dot((v * v).astype(BF16), bd_ref[...], preferred_element_type=F32)
    return v * lax.rsqrt(ms + EPS) * gain


def _even_epilogue(h_slabs, w_ref, qg_ref, kg_ref, cos_ref, se_ref, so_ref, bd_ref, o_ref):
    cos, s_even, s_odd = cos_ref[...], se_ref[...], so_ref[...]
    pieces = _even_pieces()
    ones = jnp.ones((PROJ_TILE, LANES), BF16)
    n_chunks = EVEN_IN // MXU_N

    h = jnp.concatenate(h_slabs, axis=0)

    def proj(c):
        return jnp.dot(h, w_ref[:, c * MXU_N:(c + 1) * MXU_N], preferred_element_type=F32)

    acc_next = proj(0)
    for c in range(n_chunks):
        acc = acc_next
        if c + 1 < n_chunks:
            acc_next = proj(c + 1)
        for half in range(MXU_N // LANES):
            src = c * MXU_N + half * LANES
            (_, kind, dst), = [p for p in pieces if p[0] == src]
            v = acc[:, half * LANES:(half + 1) * LANES]
            if kind == "aq":
                v = _rope(_head_rms(v, qg_ref[...], bd_ref), cos, s_even, s_odd) * Q_SCALE
            elif kind == "ak":
                v = _rope(_head_rms(v, kg_ref[...], bd_ref), cos, s_even, s_odd)
            elif kind == "bq":
                v = _rope(v, cos, s_even, s_odd) * Q_SCALE
            elif kind == "bk":
                v = _rope(v, cos, s_even, s_odd)
            elif kind == "gate":
                v = _silu(v)
            elif kind == "v":
                o_ref[:, dst + LANES:dst + 2 * LANES] = ones
            o_ref[:, dst:dst + LANES] = v.astype(BF16)


def _odd_epilogue(t, h_slabs, w_ref, conv_refs, o_ref, u_ref, yc_ref):
    h = jnp.concatenate(h_slabs, axis=0)
    dw_ref, db_ref, lg_ref, lb_ref = conv_refs
    n_slabs = PROJ_TILE // ROW_TILE
    n_cc = C_WIDTH // MXU_N

    def proj(col):
        return jnp.dot(h, w_ref[:, col:col + MXU_N], preferred_element_type=F32)

    ones = jnp.ones((PROJ_TILE, HEAD_DIM), F32)
    heads_per_chunk = MXU_N // HEAD_DIM
    held, u_val, sg_val, conv_acc, windows = {}, {}, {}, {}, {}

    def window(s, c):
        zeros = jnp.zeros((HALO, MXU_N), F32)
        lo, hi = s * ROW_TILE, (s + 1) * ROW_TILE
        left = u_val[c][lo - HALO:lo] if s > 0 else zeros
        right = u_val[c][hi:hi + HALO] if s < n_slabs - 1 else zeros
        if s == 1:
            left = jnp.where(t == 0, zeros, left)
        if s == 0:
            right = jnp.where(t == 0, zeros, right)
        return jnp.concatenate([left, u_val[c][lo:hi], right], axis=0)

    def conv_piece(s, c, r):
        lanes = slice(c * MXU_N, (c + 1) * MXU_N)
        n_win = ROW_TILE + 2 * HALO
        if (s, c) not in windows:
            windows[(s, c)] = window(s, c)
        win = windows[(s, c)]
        shifted = win if r == 0 else pltpu.roll(win, n_win - r, 0)
        acc = conv_acc.get((s, c))
        if acc is None:
            acc = jnp.zeros((ROW_TILE, MXU_N), F32) + db_ref[:, lanes]
        for k in range(C_CONV):
            off = HALO - C_CONV // 2 + k
            if off % CONV_GROUPS == r:
                base = off - r
                acc = acc + shifted[base:base + ROW_TILE, :] * dw_ref[k:k + 1, lanes]
        conv_acc[(s, c)] = acc

    def conv_finish(s):
        acc = jnp.concatenate([conv_acc.pop((s, c)) for c in range(n_cc)], axis=1)
        sg = jnp.concatenate([sg_val[c][s * ROW_TILE:(s + 1) * ROW_TILE] for c in range(n_cc)], axis=1)
        yc_ref[s * ROW_TILE:(s + 1) * ROW_TILE, :] = _conv_tail(acc, sg, lg_ref, lb_ref)

    conv_todo = [("piece", s, c, r) for s in range(n_slabs) for c in range(n_cc) for r in range(CONV_GROUPS)]
    for s in range(n_slabs):
        conv_todo.insert((s + 1) * n_cc * CONV_GROUPS + s, ("finish", s))

    def finish(kind, c, v):
        lanes = slice(c * MXU_N, (c + 1) * MXU_N)
        if kind == "val":
            held[c] = v
        elif kind == "glu":
            u_val[c] = held.pop(c) * jax.nn.sigmoid(v)
            u_ref[:, lanes] = u_val[c]
        elif kind == "v":
            for i in range(heads_per_chunk):
                head = c * heads_per_chunk + i
                ext = jnp.concatenate([v[:, i * HEAD_DIM:(i + 1) * HEAD_DIM], ones], axis=1)
                o_ref[:, OD_DV + head * LANES:OD_DV + (head + 1) * LANES] = ext.astype(BF16)
        else:
            dst = {"cg": OD_CG, "q": OD_DQ, "k": OD_DK, "dg": OD_DG}[kind]
            if kind in ("cg", "dg"):
                v = _silu(v)
                if kind == "cg":
                    sg_val[c] = v
            elif kind == "q":
                v = v * Q_SCALE
            o_ref[:, dst + c * MXU_N:dst + (c + 1) * MXU_N] = v.astype(BF16)

    order = [("val", 0), ("glu", 1), ("cg", 2), ("q", 3), ("k", 4), ("v", 5), ("dg", 6)]
    steps = [(kind, c, g * C_WIDTH + c * MXU_N) for c in range(C_WIDTH // MXU_N) for kind, g in order[:2]]
    steps += [(kind, c, g * C_WIDTH + c * MXU_N) for kind, g in order[2:] for c in range(C_WIDTH // MXU_N)]
    conv_from = 4
    per_step = -(-len(conv_todo) // (len(steps) - conv_from))
    acc_next = proj(steps[0][2])
    for i, (kind, c, _) in enumerate(steps):
        acc = acc_next
        if i + 1 < len(steps):
            acc_next = proj(steps[i + 1][2])
        finish(kind, c, acc)
        if i >= conv_from:
            for item in conv_todo[(i - conv_from) * per_step:(i - conv_from + 1) * per_step]:
                if item[0] == "piece":
                    conv_piece(*item[1:])
                else:
                    conv_finish(item[1])


def _make_proj_kernel(first, has_out, in_kind):
    def kern(*refs):
        it = iter(refs)
        t = pl.program_id(1)
        if first:
            ctx_ref, xa_ref, xb_ref, xc_ref = next(it), next(it), next(it), next(it)
        else:
            x_ref = next(it)
        if has_out:
            ya_ref, yb_ref, wo_ref, cur_ref, cur_ctx_ref = (next(it) for _ in range(5))
        norm_ref, nxt_ref, nxt_ctx_ref, wi_ref = (next(it) for _ in range(4))
        if in_kind == "even":
            extra = [next(it) for _ in range(6)]
        else:
            conv_refs = [next(it) for _ in range(4)]
        if has_out:
            xo_ref = next(it)
        o_ref = next(it)
        if in_kind == "odd":
            u_ref, yc_ref = next(it), next(it)

        h_slabs = []
        for s in range(PROJ_TILE // ROW_TILE):
            rows = slice(s * ROW_TILE, (s + 1) * ROW_TILE)
            is_ctx = jnp.logical_and(t == 0, s == 0)
            if first:
                x = jnp.where(t == 0, ctx_ref[...], xa_ref[...]) if s == 0 else (xb_ref, xc_ref)[s - 1][...]
            else:
                x = x_ref[rows, :]
            if has_out:
                half = wo_ref.shape[0] // 2
                y = jnp.dot(ya_ref[rows, :], wo_ref[0:half, :], preferred_element_type=F32)
                y = y + jnp.dot(yb_ref[rows, :], wo_ref[half:, :], preferred_element_type=F32)
                cur = jnp.where(is_ctx, cur_ctx_ref[...], cur_ref[...]) if s == 0 else cur_ref[...]
                x = x + cur[:, 2 * D_MODEL:3 * D_MODEL] * y
                xo_ref[rows, :] = x
            nxt = jnp.where(is_ctx, nxt_ctx_ref[...], nxt_ref[...]) if s == 0 else nxt_ref[...]
            ms = jnp.mean(x * x, axis=-1, keepdims=True)
            normed = x * lax.rsqrt(ms + EPS) * norm_ref[...]
            h_slabs.append((normed * (1.0 + nxt[:, D_MODEL:2 * D_MODEL]) + nxt[:, 0:D_MODEL]).astype(BF16))
        if in_kind == "even":
            _even_epilogue(h_slabs, wi_ref, *extra, o_ref)
        else:
            _odd_epilogue(t, h_slabs, wi_ref, conv_refs, o_ref, u_ref, yc_ref)

    return kern


def _proj(in_kind, layer_in, norm_g, w_in, mods, *, x=None, ctx=None, xs=None, out=None, even_extra=None,
          conv=None):
    first = xs is None
    has_out = out is not None
    const = lambda shape: pl.BlockSpec(shape, lambda b, t: (0,) * len(shape), pipeline_mode=pl.Buffered(1))
    layer_of = lambda stack, j: pl.BlockSpec((None,) + stack.shape[1:], lambda b, t: (j, 0, 0),
                                             pipeline_mode=pl.Buffered(1))
    tile = lambda width: pl.BlockSpec((None, PROJ_TILE, width), lambda b, t: (b, t, 0))
    mod_sample = lambda layer: pl.BlockSpec((None, 1, 3 * D_MODEL), lambda b, t: (layer * MOD_ROWS + b, 0, 0))
    mod_ctx = lambda layer: pl.BlockSpec((None, 1, 3 * D_MODEL),
                                         lambda b, t: (layer * MOD_ROWS + CTX_MOD_ROW, 0, 0))
    args, specs = [], []
    if first:
        per = PROJ_TILE // ROW_TILE
        blk = lambda off: pl.BlockSpec((None, ROW_TILE, D_MODEL),
                                       lambda b, t: (b, jnp.maximum(per * t + off, 0), 0))
        args += [ctx, x, x, x]
        specs += [pl.BlockSpec((None, CTX_LEN, D_MODEL), lambda b, t: (b, 0, 0)), blk(-1), blk(0), blk(1)]
    else:
        args.append(xs)
        specs.append(tile(D_MODEL))
    aliases = {}
    if has_out:
        ya, yb, (w_out_stack, j_out), layer_out = out
        half = w_out_stack.shape[1] // 2
        args += [ya, yb, w_out_stack, mods, mods]
        specs += [tile(half), tile(half), layer_of(w_out_stack, j_out), mod_sample(layer_out), mod_ctx(layer_out)]
        if not first:
            aliases = {0: 0}
    w_in_stack, j_in = w_in
    args += [norm_g.reshape(1, D_MODEL), mods, mods, w_in_stack]
    specs += [const((1, D_MODEL)), mod_sample(layer_in), mod_ctx(layer_in), layer_of(w_in_stack, j_in)]
    if in_kind == "even":
        q_gain, k_gain, (cos, s_even, s_odd), bd = even_extra
        tab = pl.BlockSpec((PROJ_TILE, LANES), lambda b, t: (t, 0))
        args += [jnp.tile(q_gain, LANES // HEAD_DIM).reshape(1, LANES),
                 jnp.tile(k_gain, LANES // HEAD_DIM).reshape(1, LANES), cos, s_even, s_odd, bd]
        specs += [const((1, LANES)), const((1, LANES)), tab, tab, tab, const((LANES, LANES))]
    else:
        dw_w, dw_b, ln_g, ln_b = conv
        args += [dw_w, dw_b.reshape(1, C_WIDTH), ln_g.reshape(1, C_WIDTH), ln_b.reshape(1, C_WIDTH)]
        specs += [const((C_CONV, C_WIDTH)), const((1, C_WIDTH)), const((1, C_WIDTH)), const((1, C_WIDTH))]
    out_shapes, out_specs = [], []
    if has_out:
        out_shapes.append(jax.ShapeDtypeStruct((BATCH, T_ALL, D_MODEL), F32))
        out_specs.append(tile(D_MODEL))
    if in_kind == "even":
        out_shapes.append(jax.ShapeDtypeStruct((BATCH, T_ALL, EVEN_OUT), BF16))
        out_specs.append(tile(EVEN_OUT))
    else:
        out_shapes += [jax.ShapeDtypeStruct((BATCH, T_ALL, ODD_OUT), BF16),
                       jax.ShapeDtypeStruct((BATCH, T_ALL, C_WIDTH), F32),
                       jax.ShapeDtypeStruct((BATCH, T_ALL, C_WIDTH), BF16)]
        out_specs += [tile(ODD_OUT), tile(C_WIDTH), tile(C_WIDTH)]
    res = pl.pallas_call(
        _make_proj_kernel(first, has_out, in_kind),
        grid=(BATCH, PROJ_TILES),
        in_specs=specs,
        out_specs=out_specs,
        out_shape=out_shapes,
        input_output_aliases=aliases,
        compiler_params=_cparams(2, PROJ_VMEM_LIMIT),
    )(*args)
    res = list(res)
    stream = res.pop(0) if has_out else None
    return (stream, *res)


def _rope_tables():
    t = np.arange(SEQ)
    row = (t // GRID_W).astype(np.float64)
    col = (t % GRID_W).astype(np.float64)
    half = HEAD_DIM // 2
    freqs = (np.float32(ROPE_THETA) ** (-np.arange(0, half, 2, dtype=np.float32) / half)).astype(np.float64)
    ang = np.concatenate([row[:, None] * freqs, col[:, None] * freqs], axis=-1).astype(np.float32)
    cos = np.repeat(np.cos(ang.astype(np.float64)), 2, axis=-1)
    sin = np.repeat(np.sin(ang.astype(np.float64)), 2, axis=-1)
    lane_even = (np.arange(HEAD_DIM) % 2 == 0)
    s_even = np.where(lane_even, -sin, 0.0)
    s_odd = np.where(lane_even, 0.0, sin)

    def full(tab, ctx_val):
        tab = np.concatenate([np.full((CTX_LEN, HEAD_DIM), ctx_val), tab], axis=0).astype(np.float32)
        return jnp.asarray(np.tile(tab, (1, LANES // HEAD_DIM)))

    return full(cos, 1.0), full(s_even, 0.0), full(s_odd, 0.0)


def _softmax_pv(s_list, v_list, sink=None):
    m = s_list[0].max(axis=-1, keepdims=True)
    for s in s_list[1:]:
        m = jnp.maximum(m, s.max(axis=-1, keepdims=True))
    if sink is not None:
        m = jnp.maximum(m, sink)
    acc = None
    for s, v in zip(s_list, v_list):
        p = jnp.exp2(s - m).astype(BF16)
        part = jnp.dot(p, v, preferred_element_type=F32)
        acc = part if acc is None else acc + part
    extra = jnp.exp2(sink - m) if sink is not None else None
    return acc, extra


def _pipelined(n_heads, scores_fn, finish_fn, depth=PIPELINE_DEPTH):
    ahead = [scores_fn(h) for h in range(min(depth, n_heads))]
    for h in range(n_heads):
        if h + depth < n_heads:
            ahead.append(scores_fn(h + depth))
        finish_fn(h, ahead.pop(0))


class _PairStore:
    def __init__(self, o_ref, g_ref):
        self.o_ref, self.g_ref, self.held = o_ref, g_ref, None

    def put(self, h, o):
        if h % 2 == 0:
            self.held = o
            return
        lanes = slice((h - 1) * HEAD_DIM, (h + 1) * HEAD_DIM)
        pair = jnp.concatenate([self.held, o], axis=1)
        self.o_ref[:, lanes] = (pair * self.g_ref[:, lanes].astype(F32)).astype(BF16)


def _gqa_normalise(acc, extra, kv):
    den = acc[:, LANES:2 * LANES]
    if extra is not None:
        den = den + extra
    o = acc[:, 0:LANES] / den
    return o[:, kv * HEAD_DIM:(kv + 1) * HEAD_DIM]


def _attn_a_kernel(q_ref, g_ref, k_ref, v_ref, o_ref):
    group = A_HEADS // A_KV_HEADS

    def run(n_keys):
        out = _PairStore(o_ref, g_ref)

        def scores(h):
            kv = h // group
            q = q_ref[:, h * HEAD_DIM:(h + 1) * HEAD_DIM]
            k = k_ref[0:n_keys, kv * HEAD_DIM:(kv + 1) * HEAD_DIM]
            return [lax.dot_general(q, k, NT_DIMS, preferred_element_type=F32)]

        def finish(h, s_list):
            acc, extra = _softmax_pv(s_list, [v_ref[0:n_keys, :]])
            out.put(h, _gqa_normalise(acc, extra, h // group))

        _pipelined(A_HEADS, scores, finish, depth=CTX_PIPELINE_DEPTH if n_keys == CTX_LEN else PIPELINE_DEPTH)

    t = pl.program_id(1)
    pl.when(t == 0)(lambda: run(CTX_LEN))
    pl.when(t > 0)(lambda: run(T_ALL))


def _attn_a(h_even):
    w = A_HEADS * HEAD_DIM
    return pl.pallas_call(
        _attn_a_kernel,
        grid=(BATCH, N_TILES),
        in_specs=[
            pl.BlockSpec((None, ROW_TILE, w), lambda b, t: (b, t, EV_AQ // w)),
            pl.BlockSpec((None, ROW_TILE, w), lambda b, t: (b, t, EV_AG // w)),
            pl.BlockSpec((None, T_ALL, LANES), lambda b, t: (b, 0, EV_AK // LANES)),
            pl.BlockSpec((None, T_ALL, 2 * LANES), lambda b, t: (b, 0, EV_AV // (2 * LANES))),
        ],
        out_specs=pl.BlockSpec((None, ROW_TILE, w), lambda b, t: (b, t, 0)),
        out_shape=jax.ShapeDtypeStruct((BATCH, T_ALL, w), BF16),
        compiler_params=_cparams(2),
    )(h_even, h_even, h_even, h_even)


def _attn_b_kernel(sink_ref, q_ref, g_ref, k_ref, v_ref, o_ref):
    group = B_HEADS // B_KV_HEADS
    t = pl.program_id(1)

    def run(row_spans, band_bias):
        out = _PairStore(o_ref, g_ref)
        k_all = jnp.concatenate([k_ref[pl.ds(s, n), :] for s, n in row_spans], axis=0)
        v_all = jnp.concatenate([v_ref[pl.ds(s, n), :] for s, n in row_spans], axis=0)
        k_kv = [k_all[:, kv * HEAD_DIM:(kv + 1) * HEAD_DIM] for kv in range(B_KV_HEADS)]

        def heads_of(u):
            return list(range(u * STACK, (u + 1) * STACK))

        def scores(u):
            heads = heads_of(u)
            q = jnp.concatenate([q_ref[:, h * HEAD_DIM:(h + 1) * HEAD_DIM] for h in heads], axis=0)
            s = lax.dot_general(q, k_kv[heads[0] // group], NT_DIMS, preferred_element_type=F32)
            if band_bias is None:
                return s
            slabs = []
            for i in range(STACK):
                slab = s[i * B_TILE:(i + 1) * B_TILE]
                slabs.append(jnp.concatenate([slab[:, 0:CTX_LEN], slab[:, CTX_LEN:] + band_bias], axis=1))
            return jnp.concatenate(slabs, axis=0)

        def finish(u, s):
            heads = heads_of(u)
            kv = heads[0] // group
            sink = jnp.concatenate([jnp.full((B_TILE, 1), sink_ref[h] * LOG2E, F32) for h in heads], axis=0)
            m = jnp.maximum(s.max(axis=-1, keepdims=True), sink)
            p = jnp.exp2(s - m).astype(BF16)
            acc = jnp.dot(p, v_all, preferred_element_type=F32)
            den = acc[:, LANES:2 * LANES] + jnp.exp2(sink - m)
            o = (acc[:, 0:LANES] / den)[:, kv * HEAD_DIM:(kv + 1) * HEAD_DIM]
            for i, h in enumerate(heads):
                out.put(h, o[i * B_TILE:(i + 1) * B_TILE])

        _pipelined(B_HEADS // STACK, scores, finish,
                   depth=CTX_PIPELINE_DEPTH if band_bias is None else B_PIPELINE_DEPTH)

    def lat_tile():
        start = pl.multiple_of(jnp.clip(t * B_TILE - B_WINDOW, CTX_LEN, T_ALL - BAND_B), LANES)
        q_pos = t * B_TILE + lax.broadcasted_iota(jnp.int32, (B_TILE, BAND_B), 0)
        k_pos = start + lax.broadcasted_iota(jnp.int32, (B_TILE, BAND_B), 1)
        bias = jnp.where(jnp.abs(k_pos - q_pos) <= B_WINDOW, 0.0, NEG_INF).astype(F32)
        run([(0, CTX_LEN), (start, BAND_B)], bias)

    ctx_tiles = CTX_LEN // B_TILE
    pl.when(t < ctx_tiles)(lambda: run([(0, CTX_LEN)], None))
    pl.when(t >= ctx_tiles)(lat_tile)


def _attn_b(h_even, sink):
    w = B_HEADS * HEAD_DIM
    return pl.pallas_call(
        _attn_b_kernel,
        grid=(BATCH, T_ALL // B_TILE),
        in_specs=[
            pl.BlockSpec(memory_space=pltpu.SMEM),
            pl.BlockSpec((None, B_TILE, w), lambda b, t: (b, t, EV_BQ // w)),
            pl.BlockSpec((None, B_TILE, w), lambda b, t: (b, t, EV_BG // w)),
            pl.BlockSpec((None, T_ALL, LANES), lambda b, t: (b, 0, EV_BK // LANES)),
            pl.BlockSpec((None, T_ALL, 2 * LANES), lambda b, t: (b, 0, EV_BV // (2 * LANES))),
        ],
        out_specs=pl.BlockSpec((None, B_TILE, w), lambda b, t: (b, t, 0)),
        out_shape=jax.ShapeDtypeStruct((BATCH, T_ALL, w), BF16),
        compiler_params=_cparams(2),
    )(sink, h_even, h_even, h_even, h_even)


CONV_GROUPS = 8


def _conv_tail(acc, sg, lg_ref, lb_ref):
    mu = jnp.mean(acc, axis=-1, keepdims=True)
    ctr = acc - mu
    var = jnp.mean(ctr * ctr, axis=-1, keepdims=True)
    y = ctr * lax.rsqrt(var + EPS) * lg_ref[...] + lb_ref[...]
    return (_silu(y) * sg).astype(BF16)


def _conv_fixup_kernel(ul_ref, uc_ref, ur_ref, sg_ref, w_ref, b_ref, lg_ref, lb_ref, ya_ref, o_ref):
    del ya_ref
    n_win = 3 * HALO
    for b in range(BATCH):
        win = jnp.concatenate([ul_ref[b], uc_ref[b], ur_ref[b]], axis=0)
        acc = jnp.zeros((HALO, C_WIDTH), F32) + b_ref[...]
        for r in range(CONV_GROUPS):
            shifted = win if r == 0 else pltpu.roll(win, n_win - r, 0)
            for k in range(C_CONV):
                off = HALO - C_CONV // 2 + k
                if off % CONV_GROUPS == r:
                    base = off - r
                    acc = acc + shifted[base:base + HALO, :] * w_ref[k:k + 1, :]
        o_ref[b] = _conv_tail(acc, sg_ref[b].astype(F32), lg_ref, lb_ref)


def _conv_fixup(ya, u, h_odd, dw_w, dw_b, ln_g, ln_b):
    per_tile = PROJ_TILE // HALO
    edge_block = lambda k: (k // 2 + 1) * per_tile - 1 + k % 2
    blk = lambda off, col=0: pl.BlockSpec((BATCH, HALO, C_WIDTH), lambda k: (0, edge_block(k) + off, col))
    const = lambda shape: pl.BlockSpec(shape, lambda k: (0,) * len(shape))
    return pl.pallas_call(
        _conv_fixup_kernel,
        grid=(2 * (PROJ_TILES - 1),),
        in_specs=[
            blk(-1), blk(0), blk(1), blk(0, OD_CG // C_WIDTH),
            const((C_CONV, C_WIDTH)), const((1, C_WIDTH)), const((1, C_WIDTH)), const((1, C_WIDTH)),
            pl.BlockSpec(memory_space=pl.ANY),
        ],
        out_specs=blk(0),
        out_shape=jax.ShapeDtypeStruct((BATCH, T_ALL, C_WIDTH), BF16),
        input_output_aliases={8: 0},
        compiler_params=_cparams(1),
    )(u, u, u, h_odd, dw_w, dw_b.reshape(1, C_WIDTH), ln_g.reshape(1, C_WIDTH), ln_b.reshape(1, C_WIDTH), ya)


TILE_GRID_ROWS = ROW_TILE // GRID_W
BAND_GRID_ROWS = BAND_D // GRID_W
N_DR = 2 * NA_KH - 1
N_DC = 2 * NA_KW - 1
PAIR_ENTRIES = 17
PAIR_RIGHT_ONLY, PAIR_LEFT_ONLY = 15, 16


INTERIOR_FIRST_DR = NA_KH // 2 - 1
INTERIOR_LAST_DR = INTERIOR_FIRST_DR + NA_KH - 1


def _pair_halves(entry):
    if entry == 0:
        return None, None
    if entry == PAIR_RIGHT_ONLY:
        return None, INTERIOR_FIRST_DR
    if entry == PAIR_LEFT_ONLY:
        return INTERIOR_LAST_DR, None
    return entry - 1, entry


def _pair_table_kernel(ext_ref, o_ref):
    qc = lax.broadcasted_iota(jnp.int32, (GRID_W, LANES), 0)
    lane = lax.broadcasted_iota(jnp.int32, (GRID_W, LANES), 1)
    kc = jnp.bitwise_and(lane, GRID_W - 1)
    col_start = jnp.clip(qc - NA_KW // 2, 0, GRID_W - NA_KW)
    col_in = jnp.logical_and(kc >= col_start, kc < col_start + NA_KW)
    is_left = lane < GRID_W

    def half(dr, right):
        rows = jnp.broadcast_to(ext_ref[dr:dr + 1, :], (GRID_W, LANES))
        shift = 1 if right else LANES - GRID_W + 1
        return pltpu.roll(rows, shift, 1, stride=1, stride_axis=0)

    for entry in range(PAIR_ENTRIES):
        left_dr, right_dr = _pair_halves(entry)
        block = jnp.full((GRID_W, LANES), NEG_INF, F32)
        if left_dr is not None:
            block = jnp.where(jnp.logical_and(is_left, col_in), half(left_dr, False), block)
        if right_dr is not None:
            block = jnp.where(jnp.logical_and(jnp.logical_not(is_left), col_in), half(right_dr, True), block)
        o_ref[entry] = block


def _na_pair_table(rpb):
    lead = GRID_W - NA_KW
    ext = jnp.pad(rpb.astype(F32) * LOG2E, ((0, 0), (0, 0), (lead, LANES - N_DC - lead)), mode="edge")
    return pl.pallas_call(
        _pair_table_kernel,
        grid=(D_HEADS,),
        in_specs=[pl.BlockSpec((None, N_DR, LANES), lambda h: (h, 0, 0))],
        out_specs=pl.BlockSpec((None, PAIR_ENTRIES, GRID_W, LANES), lambda h: (h, 0, 0, 0)),
        out_shape=jax.ShapeDtypeStruct((D_HEADS, PAIR_ENTRIES, GRID_W, LANES), F32),
        compiler_params=_cparams(1),
    )(ext)


def _na_pair_entry(variant, a, p):
    if variant == "first":
        return 2 * p - a + NA_KH if p < NA_KH // 2 else 0
    if variant == "last":
        return 2 * p - a if p >= (BAND_GRID_ROWS - NA_KH) // 2 else 0
    left_ok = a <= 2 * p < a + NA_KH
    right_ok = a <= 2 * p + 1 < a + NA_KH
    if left_ok and right_ok:
        return 2 * p - a + NA_KH // 2
    if right_ok:
        return PAIR_RIGHT_ONLY
    if left_ok:
        return PAIR_LEFT_ONLY
    return 0


def _attn_d_kernel(first_tile, q_ref, g_ref, k_ref, v_ref, tab_ref, o_ref):
    t = pl.program_id(1) + first_tile

    def normalise(acc):
        return (acc / pltpu.roll(acc, HEAD_DIM, 1))[:, 0:HEAD_DIM]

    def ctx_tile():
        out = _PairStore(o_ref, g_ref)

        def scores(h):
            lanes = slice(h * HEAD_DIM, (h + 1) * HEAD_DIM)
            return [lax.dot_general(q_ref[:, lanes], k_ref[0:CTX_LEN, lanes], NT_DIMS, preferred_element_type=F32)]

        def finish(h, s_list):
            acc, _ = _softmax_pv(s_list, [v_ref[0:CTX_LEN, h * LANES:(h + 1) * LANES]])
            out.put(h, normalise(acc))

        _pipelined(D_HEADS, scores, finish, depth=CTX_PIPELINE_DEPTH)

    def lat_tile():
        out = _PairStore(o_ref, g_ref)
        start = pl.multiple_of(
            jnp.clip((t - 1) * ROW_TILE - (NA_KH // 2) * GRID_W, 0, SEQ - BAND_D) + CTX_LEN, LANES)
        is_first = t == 1
        is_last = t == N_TILES - 1

        def bias(h):
            rows = []
            for a in range(TILE_GRID_ROWS):
                blocks = []
                for p in range(BAND_GRID_ROWS // 2):
                    entry = jnp.where(is_first, _na_pair_entry("first", a, p),
                                      jnp.where(is_last, _na_pair_entry("last", a, p),
                                                _na_pair_entry("interior", a, p)))
                    blocks.append(tab_ref[h, entry])
                rows.append(jnp.concatenate(blocks, axis=1))
            return jnp.concatenate(rows, axis=0)

        def scores(h):
            lanes = slice(h * HEAD_DIM, (h + 1) * HEAD_DIM)
            q = q_ref[:, lanes]
            kb = k_ref[pl.ds(start, BAND_D), :][:, lanes]
            s_ctx = lax.dot_general(q, k_ref[0:CTX_LEN, lanes], NT_DIMS, preferred_element_type=F32)
            s_band = lax.dot_general(q, kb, NT_DIMS, preferred_element_type=F32) + bias(h)
            return [s_ctx, s_band]

        def finish(h, s_list):
            v_lanes = slice(h * LANES, (h + 1) * LANES)
            v_list = [v_ref[0:CTX_LEN, v_lanes], v_ref[pl.ds(start, BAND_D), :][:, v_lanes]]
            acc, _ = _softmax_pv(s_list, v_list)
            out.put(h, normalise(acc))

        _pipelined(D_HEADS, scores, finish)

    if first_tile == 0:
        pl.when(t == 0)(ctx_tile)
        pl.when(t > 0)(lat_tile)
    else:
        lat_tile()


def _attn_d(h_odd, pair_tab, lat_only):
    w = D_HEADS * HEAD_DIM
    f = 1 if lat_only else 0
    return pl.pallas_call(
        functools.partial(_attn_d_kernel, f),
        grid=(BATCH, N_TILES - f),
        in_specs=[
            pl.BlockSpec((None, ROW_TILE, w), lambda b, t: (b, t + f, OD_DQ // w)),
            pl.BlockSpec((None, ROW_TILE, w), lambda b, t: (b, t + f, OD_DG // w)),
            pl.BlockSpec((None, T_ALL, w), lambda b, t: (b, 0, OD_DK // w)),
            pl.BlockSpec((None, T_ALL, 2 * w), lambda b, t: (b, 0, OD_DV // (2 * w))),
            pl.BlockSpec((D_HEADS, PAIR_ENTRIES, GRID_W, LANES), lambda b, t: (0, 0, 0, 0)),
        ],
        out_specs=pl.BlockSpec((None, ROW_TILE, w), lambda b, t: (b, t, 0)),
        out_shape=jax.ShapeDtypeStruct((BATCH, T_ALL - f * CTX_LEN, w), BF16),
        compiler_params=_cparams(2),
    )(h_odd, h_odd, h_odd, h_odd, pair_tab)


FINAL_TILE = 4 * ROW_TILE
FINAL_BLOCKS = FINAL_TILE // ROW_TILE


def _outproj_final_kernel(*refs):
    ya_refs, refs = refs[:FINAL_BLOCKS], refs[FINAL_BLOCKS:]
    yb_ref, w_ref = refs[:2]
    x_refs, (mod_ref, fg_ref, o_ref) = refs[2:2 + FINAL_BLOCKS], refs[2 + FINAL_BLOCKS:]
    half = w_ref.shape[0] // 2
    ya = jnp.concatenate([r[...] for r in ya_refs], axis=0)
    y = jnp.dot(ya, w_ref[0:half, :], preferred_element_type=F32)
    y = y + jnp.dot(yb_ref[...], w_ref[half:, :], preferred_element_type=F32)
    x = jnp.concatenate([r[...] for r in x_refs], axis=0) + mod_ref[:, 2 * D_MODEL:3 * D_MODEL] * y
    ms = jnp.mean(x * x, axis=-1, keepdims=True)
    o_ref[...] = x * lax.rsqrt(ms + EPS) * fg_ref[...]


def _outproj_final(ya, yb, w_out, xs, mods, layer, final_g):
    w_out_stack, j_out = w_out
    half = w_out_stack.shape[1] // 2
    blks = lambda width: [pl.BlockSpec((None, ROW_TILE, width),
                                       lambda b, t, off=off: (b, FINAL_BLOCKS * t + 1 + off, 0))
                          for off in range(FINAL_BLOCKS)]
    return pl.pallas_call(
        _outproj_final_kernel,
        grid=(BATCH, SEQ // FINAL_TILE),
        in_specs=[
            *blks(half),
            pl.BlockSpec((None, FINAL_TILE, half), lambda b, t: (b, t, 0)),
            pl.BlockSpec((None, 2 * half, D_MODEL), lambda b, t: (j_out, 0, 0)),
            *blks(D_MODEL),
            pl.BlockSpec((None, 1, 3 * D_MODEL), lambda b, t: (layer * MOD_ROWS + b, 0, 0)),
            pl.BlockSpec((1, D_MODEL), lambda b, t: (0, 0)),
        ],
        out_specs=pl.BlockSpec((None, FINAL_TILE, D_MODEL), lambda b, t: (b, t, 0)),
        out_shape=jax.ShapeDtypeStruct((BATCH, SEQ, D_MODEL), F32),
        compiler_params=_cparams(2),
    )(*[ya] * FINAL_BLOCKS, yb, w_out_stack, *[xs] * FINAL_BLOCKS, mods, final_g.reshape(1, D_MODEL))


def kernel(x, c, ctx, c_ctx, mod_w, mod_b, norm_g, ev_w_in, ev_w_out, a_q_gain, a_k_gain, b_sink,
           od_w_in, od_w_out, c_dw_w, c_dw_b, c_ln_g, c_ln_b, d_rpb, final_g):
    cond = jnp.concatenate(
        [c, c_ctx[None, :], jnp.zeros((MOD_ROWS - BATCH - 1, D_MODEL), F32)], axis=0)
    mods = _modulation_all(cond, mod_w, mod_b).reshape(DEPTH * MOD_ROWS, 1, 3 * D_MODEL)
    tables = _rope_tables()
    head_of_lane = np.arange(LANES) // HEAD_DIM
    bd = jnp.asarray((head_of_lane[:, None] == head_of_lane[None, :]).astype(np.float32) / HEAD_DIM, BF16)
    ev_w_in_b, ev_w_out_b, od_w_in_b, od_w_out_b = (w.astype(BF16) for w in (ev_w_in, ev_w_out, od_w_in, od_w_out))

    xs = None
    pending = None
    for i in range(DEPTH):
        j = i // 2
        src = dict(x=x, ctx=ctx) if xs is None else dict(xs=xs)
        if i % 2 == 0:
            extra = (a_q_gain[j], a_k_gain[j], tables, bd)
            stream, h_even = _proj("even", i, norm_g[i], (ev_w_in_b, j), mods, out=pending, even_extra=extra, **src)
            pending = (_attn_a(h_even), _attn_b(h_even, b_sink[j]), (ev_w_out_b, j), i)
        else:
            conv = (c_dw_w[j], c_dw_b[j], c_ln_g[j], c_ln_b[j])
            stream, h_odd, u, ya = _proj("odd", i, norm_g[i], (od_w_in_b, j), mods, out=pending, conv=conv, **src)
            ya = _conv_fixup(ya, u, h_odd, *conv)
            last = i == DEPTH - 1
            pending = (ya, _attn_d(h_odd, _na_pair_table(d_rpb[j]), lat_only=last), (od_w_out_b, j), i)
        if stream is not None:
            xs = stream
    ya, yb, w_out, layer = pending
    return _outproj_final(ya, yb, w_out, xs, mods, layer, final_g)
```

```python
import functools

import numpy as np
import jax
import jax.numpy as jnp
from jax import lax
from jax.experimental import pallas as pl
from jax.experimental.pallas import tpu as pltpu

D_MODEL = 1024
BATCH = 8
SEQ = 2048
DEPTH = 4
GRID_W = 64
CTX_LEN = 256
HEAD_DIM = 64
ROPE_THETA = 10000.0
EPS = 1e-6
NEG_INF = -1e30
ATTN_SCALE = HEAD_DIM ** -0.5
A_HEADS = 8
A_KV_HEADS = 2
B_HEADS = 8
B_KV_HEADS = 2
B_WINDOW = 128
C_WIDTH = 512
C_CONV = 31
D_HEADS = 8
NA_KH = 8
NA_KW = 16
ROWS = SEQ // GRID_W

T_ALL = CTX_LEN + SEQ
ROW_TILE = 256
N_TILES = T_ALL // ROW_TILE
PROJ_TILE = 3 * ROW_TILE
PROJ_TILES = T_ALL // PROJ_TILE
LANES = 128
MXU_N = 256
MOD_ROWS = 16
CTX_MOD_ROW = BATCH
VMEM_LIMIT = 48 * 1024 * 1024
PROJ_VMEM_LIMIT = 56 * 1024 * 1024

EVEN_IN = 2560
ODD_IN = 3584
LOG2E = 1.4426950408889634
Q_SCALE = ATTN_SCALE * LOG2E
CTX_PIPELINE_DEPTH = 8
B_PIPELINE_DEPTH = 3
PIPELINE_DEPTH = 2
HALO = 16
B_TILE = ROW_TILE
BAND_B = B_TILE + 2 * B_WINDOW
BAND_D = 768

F32 = jnp.float32
BF16 = jnp.bfloat16
NT_DIMS = (((1,), (1,)), ((), ()))


def _cparams(n_grid, vmem=VMEM_LIMIT):
    return pltpu.CompilerParams(dimension_semantics=("arbitrary",) * n_grid, vmem_limit_bytes=vmem)


def _silu(v):
    return v * jax.nn.sigmoid(v)


def _mod_kernel(cond_ref, w_ref, b_ref, o_ref):
    cnd = cond_ref[...]
    act = _silu(cnd).astype(BF16)
    o_ref[...] = jnp.dot(act, w_ref[...].astype(BF16), preferred_element_type=F32) + b_ref[...]


def _modulation_all(cond, mod_w, mod_b):
    n_col = 3
    return pl.pallas_call(
        _mod_kernel,
        grid=(DEPTH, n_col),
        in_specs=[
            pl.BlockSpec((MOD_ROWS, D_MODEL), lambda i, j: (0, 0)),
            pl.BlockSpec((None, D_MODEL, D_MODEL), lambda i, j: (i, 0, j)),
            pl.BlockSpec((None, 1, D_MODEL), lambda i, j: (i, 0, j)),
        ],
        out_specs=pl.BlockSpec((None, MOD_ROWS, D_MODEL), lambda i, j: (i, 0, j)),
        out_shape=jax.ShapeDtypeStruct((DEPTH, MOD_ROWS, 3 * D_MODEL), F32),
        compiler_params=_cparams(2),
    )(cond, mod_w, mod_b.reshape(DEPTH, 1, 3 * D_MODEL))


EVEN_OUT = 2816
EV_AQ, EV_AG, EV_BQ, EV_BG, EV_AV, EV_BV, EV_AK, EV_BK = 0, 512, 1024, 1536, 2048, 2304, 2560, 2688

ODD_OUT = 3072
OD_CG, OD_DQ, OD_DK, OD_DG, OD_DV = 0, 512, 1024, 1536, 2048


def _even_pieces():
    pieces = []
    for j in range(4):
        pieces.append((0 + LANES * j, "aq", EV_AQ + LANES * j))
    pieces.append((512, "ak", EV_AK))
    pieces.append((640, "v", EV_AV))
    for j in range(4):
        pieces.append((768 + LANES * j, "gate", EV_AG + LANES * j))
    for j in range(4):
        pieces.append((1280 + LANES * j, "bq", EV_BQ + LANES * j))
    pieces.append((1792, "bk", EV_BK))
    pieces.append((1920, "v", EV_BV))
    for j in range(4):
        pieces.append((2048 + LANES * j, "gate", EV_BG + LANES * j))
    return pieces


def _rope(v, cos, sin_even, sin_odd):
    nxt = pltpu.roll(v, LANES - 1, 1)
    prv = pltpu.roll(v, 1, 1)
    return v * cos + nxt * sin_even + prv * sin_odd


def _head_rms(v, gain, bd_ref):
    ms = jnp.dot((v * v).astype(BF16), bd_ref[...], preferred_element_type=F32)
    return v * lax.rsqrt(ms + EPS) * gain


def _even_epilogue(h_slabs, w_ref, qg_ref, kg_ref, cos_ref, se_ref, so_ref, bd_ref, o_ref):
    cos, s_even, s_odd = cos_ref[...], se_ref[...], so_ref[...]
    pieces = _even_pieces()
    ones = jnp.ones((PROJ_TILE, LANES), BF16)
    n_chunks = EVEN_IN // MXU_N

    h = jnp.concatenate(h_slabs, axis=0)

    def proj(c):
        return jnp.dot(h, w_ref[:, c * MXU_N:(c + 1) * MXU_N], preferred_element_type=F32)

    acc_next = proj(0)
    for c in range(n_chunks):
        acc = acc_next
        if c + 1 < n_chunks:
            acc_next = proj(c + 1)
        for half in range(MXU_N // LANES):
            src = c * MXU_N + half * LANES
            (_, kind, dst), = [p for p in pieces if p[0] == src]
            v = acc[:, half * LANES:(half + 1) * LANES]
            if kind == "aq":
                v = _rope(_head_rms(v, qg_ref[...], bd_ref), cos, s_even, s_odd) * Q_SCALE
            elif kind == "ak":
                v = _rope(_head_rms(v, kg_ref[...], bd_ref), cos, s_even, s_odd)
            elif kind == "bq":
                v = _rope(v, cos, s_even, s_odd) * Q_SCALE
            elif kind == "bk":
                v = _rope(v, cos, s_even, s_odd)
            elif kind == "gate":
                v = _silu(v)
            elif kind == "v":
                o_ref[:, dst + LANES:dst + 2 * LANES] = ones
            o_ref[:, dst:dst + LANES] = v.astype(BF16)


def _odd_epilogue(t, h_slabs, w_ref, conv_refs, o_ref, u_ref, yc_ref):
    h = jnp.concatenate(h_slabs, axis=0)
    dw_ref, db_ref, lg_ref, lb_ref = conv_refs
    n_slabs = PROJ_TILE // ROW_TILE
    n_cc = C_WIDTH // MXU_N

    def proj(col):
        return jnp.dot(h, w_ref[:, col:col + MXU_N], preferred_element_type=F32)

    ones = jnp.ones((PROJ_TILE, HEAD_DIM), F32)
    heads_per_chunk = MXU_N // HEAD_DIM
    held, u_val, sg_val, conv_acc, windows = {}, {}, {}, {}, {}

    def window(s, c):
        zeros = jnp.zeros((HALO, MXU_N), F32)
        lo, hi = s * ROW_TILE, (s + 1) * ROW_TILE
        left = u_val[c][lo - HALO:lo] if s > 0 else zeros
        right = u_val[c][hi:hi + HALO] if s < n_slabs - 1 else zeros
        if s == 1:
            left = jnp.where(t == 0, zeros, left)
        if s == 0:
            right = jnp.where(t == 0, zeros, right)
        return jnp.concatenate([left, u_val[c][lo:hi], right], axis=0)

    def conv_piece(s, c, r):
        lanes = slice(c * MXU_N, (c + 1) * MXU_N)
        n_win = ROW_TILE + 2 * HALO
        if (s, c) not in windows:
            windows[(s, c)] = window(s, c)
        win = windows[(s, c)]
        shifted = win if r == 0 else pltpu.roll(win, n_win - r, 0)
        acc = conv_acc.get((s, c))
        if acc is None:
            acc = jnp.zeros((ROW_TILE, MXU_N), F32) + db_ref[:, lanes]
        for k in range(C_CONV):
            off = HALO - C_CONV // 2 + k
            if off % CONV_GROUPS == r:
                base = off - r
                acc = acc + shifted[base:base + ROW_TILE, :] * dw_ref[k:k + 1, lanes]
        conv_acc[(s, c)] = acc

    def conv_finish(s):
        acc = jnp.concatenate([conv_acc.pop((s, c)) for c in range(n_cc)], axis=1)
        sg = jnp.concatenate([sg_val[c][s * ROW_TILE:(s + 1) * ROW_TILE] for c in range(n_cc)], axis=1)
        yc_ref[s * ROW_TILE:(s + 1) * ROW_TILE, :] = _conv_tail(acc, sg, lg_ref, lb_ref)

    def finish(kind, c, v):
        lanes = slice(c * MXU_N, (c + 1) * MXU_N)
        if kind == "val":
            held[c] = v
        elif kind == "glu":
            u_val[c] = held.pop(c) * jax.nn.sigmoid(v)
            u_ref[:, lanes] = u_val[c]
        elif kind == "v":
            for i in range(heads_per_chunk):
                head = c * heads_per_chunk + i
                ext = jnp.concatenate([v[:, i * HEAD_DIM:(i + 1) * HEAD_DIM], ones], axis=1)
                o_ref[:, OD_DV + head * LANES:OD_DV + (head + 1) * LANES] = ext.astype(BF16)
        else:
            dst = {"cg": OD_CG, "q": OD_DQ, "k": OD_DK, "dg": OD_DG}[kind]
            if kind in ("cg", "dg"):
                v = _silu(v)
                if kind == "cg":
                    sg_val[c] = v
            elif kind == "q":
                v = v * Q_SCALE
            o_ref[:, dst + c * MXU_N:dst + (c + 1) * MXU_N] = v.astype(BF16)

    order = [("val", 0), ("glu", 1), ("cg", 2), ("q", 3), ("k", 4), ("v", 5), ("dg", 6)]
    steps = [(kind, c, g * C_WIDTH + c * MXU_N) for c in range(C_WIDTH // MXU_N) for kind, g in order[:2]]
    steps += [(kind, c, g * C_WIDTH + c * MXU_N) for kind, g in order[2:] for c in range(C_WIDTH // MXU_N)]
    acc_next = proj(steps[0][2])
    for i, (kind, c, _) in enumerate(steps):
        acc = acc_next
        if i + 1 < len(steps):
            acc_next = proj(steps[i + 1][2])
        finish(kind, c, acc)
    for s in range(n_slabs):
        for c in range(n_cc):
            for r in range(CONV_GROUPS):
                conv_piece(s, c, r)
        conv_finish(s)


def _make_proj_kernel(first, has_out, in_kind):
    def kern(*refs):
        it = iter(refs)
        t = pl.program_id(1)
        if first:
            ctx_ref, xa_ref, xb_ref, xc_ref = next(it), next(it), next(it), next(it)
        else:
            x_ref = next(it)
        if has_out:
            ya_ref, yb_ref, wo_ref, cur_ref, cur_ctx_ref = (next(it) for _ in range(5))
        norm_ref, nxt_ref, nxt_ctx_ref, wi_ref = (next(it) for _ in range(4))
        if in_kind == "even":
            extra = [next(it) for _ in range(6)]
        else:
            conv_refs = [next(it) for _ in range(4)]
        if has_out:
            xo_ref = next(it)
        o_ref = next(it)
        if in_kind == "odd":
            u_ref, yc_ref = next(it), next(it)

        h_slabs = []
        for s in range(PROJ_TILE // ROW_TILE):
            rows = slice(s * ROW_TILE, (s + 1) * ROW_TILE)
            is_ctx = jnp.logical_and(t == 0, s == 0)
            if first:
                x = jnp.where(t == 0, ctx_ref[...], xa_ref[...]) if s == 0 else (xb_ref, xc_ref)[s - 1][...]
            else:
                x = x_ref[rows, :]
            if has_out:
                half = wo_ref.shape[0] // 2
                y = jnp.dot(ya_ref[rows, :], wo_ref[0:half, :], preferred_element_type=F32)
                y = y + jnp.dot(yb_ref[rows, :], wo_ref[half:, :], preferred_element_type=F32)
                cur = jnp.where(is_ctx, cur_ctx_ref[...], cur_ref[...]) if s == 0 else cur_ref[...]
                x = x + cur[:, 2 * D_MODEL:3 * D_MODEL] * y
                xo_ref[rows, :] = x
            nxt = jnp.where(is_ctx, nxt_ctx_ref[...], nxt_ref[...]) if s == 0 else nxt_ref[...]
            ms = jnp.mean(x * x, axis=-1, keepdims=True)
            normed = x * lax.rsqrt(ms + EPS) * norm_ref[...]
            h_slabs.append((normed * (1.0 + nxt[:, D_MODEL:2 * D_MODEL]) + nxt[:, 0:D_MODEL]).astype(BF16))
        if in_kind == "even":
            _even_epilogue(h_slabs, wi_ref, *extra, o_ref)
        else:
            _odd_epilogue(t, h_slabs, wi_ref, conv_refs, o_ref, u_ref, yc_ref)

    return kern


def _proj(in_kind, layer_in, norm_g, w_in, mods, *, x=None, ctx=None, xs=None, out=None, even_extra=None,
          conv=None):
    first = xs is None
    has_out = out is not None
    const = lambda shape: pl.BlockSpec(shape, lambda b, t: (0,) * len(shape), pipeline_mode=pl.Buffered(1))
    layer_of = lambda stack, j: pl.BlockSpec((None,) + stack.shape[1:], lambda b, t: (j, 0, 0),
                                             pipeline_mode=pl.Buffered(1))
    tile = lambda width: pl.BlockSpec((None, PROJ_TILE, width), lambda b, t: (b, t, 0))
    mod_sample = lambda layer: pl.BlockSpec((None, 1, 3 * D_MODEL), lambda b, t: (layer * MOD_ROWS + b, 0, 0))
    mod_ctx = lambda layer: pl.BlockSpec((None, 1, 3 * D_MODEL),
                                         lambda b, t: (layer * MOD_ROWS + CTX_MOD_ROW, 0, 0))
    args, specs = [], []
    if first:
        per = PROJ_TILE // ROW_TILE
        blk = lambda off: pl.BlockSpec((None, ROW_TILE, D_MODEL),
                                       lambda b, t: (b, jnp.maximum(per * t + off, 0), 0))
        args += [ctx, x, x, x]
        specs += [pl.BlockSpec((None, CTX_LEN, D_MODEL), lambda b, t: (b, 0, 0)), blk(-1), blk(0), blk(1)]
    else:
        args.append(xs)
        specs.append(tile(D_MODEL))
    aliases = {}
    if has_out:
        ya, yb, (w_out_stack, j_out), layer_out = out
        half = w_out_stack.shape[1] // 2
        args += [ya, yb, w_out_stack, mods, mods]
        specs += [tile(half), tile(half), layer_of(w_out_stack, j_out), mod_sample(layer_out), mod_ctx(layer_out)]
        if not first:
            aliases = {0: 0}
    w_in_stack, j_in = w_in
    args += [norm_g.reshape(1, D_MODEL), mods, mods, w_in_stack]
    specs += [const((1, D_MODEL)), mod_sample(layer_in), mod_ctx(layer_in), layer_of(w_in_stack, j_in)]
    if in_kind == "even":
        q_gain, k_gain, (cos, s_even, s_odd), bd = even_extra
        tab = pl.BlockSpec((PROJ_TILE, LANES), lambda b, t: (t, 0))
        args += [jnp.tile(q_gain, LANES // HEAD_DIM).reshape(1, LANES),
                 jnp.tile(k_gain, LANES // HEAD_DIM).reshape(1, LANES), cos, s_even, s_odd, bd]
        specs += [const((1, LANES)), const((1, LANES)), tab, tab, tab, const((LANES, LANES))]
    else:
        dw_w, dw_b, ln_g, ln_b = conv
        args += [dw_w, dw_b.reshape(1, C_WIDTH), ln_g.reshape(1, C_WIDTH), ln_b.reshape(1, C_WIDTH)]
        specs += [const((C_CONV, C_WIDTH)), const((1, C_WIDTH)), const((1, C_WIDTH)), const((1, C_WIDTH))]
    out_shapes, out_specs = [], []
    if has_out:
        out_shapes.append(jax.ShapeDtypeStruct((BATCH, T_ALL, D_MODEL), F32))
        out_specs.append(tile(D_MODEL))
    if in_kind == "even":
        out_shapes.append(jax.ShapeDtypeStruct((BATCH, T_ALL, EVEN_OUT), BF16))
        out_specs.append(tile(EVEN_OUT))
    else:
        out_shapes += [jax.ShapeDtypeStruct((BATCH, T_ALL, ODD_OUT), BF16),
                       jax.ShapeDtypeStruct((BATCH, T_ALL, C_WIDTH), F32),
                       jax.ShapeDtypeStruct((BATCH, T_ALL, C_WIDTH), BF16)]
        out_specs += [tile(ODD_OUT), tile(C_WIDTH), tile(C_WIDTH)]
    res = pl.pallas_call(
        _make_proj_kernel(first, has_out, in_kind),
        grid=(BATCH, PROJ_TILES),
        in_specs=specs,
        out_specs=out_specs,
        out_shape=out_shapes,
        input_output_aliases=aliases,
        compiler_params=_cparams(2, PROJ_VMEM_LIMIT),
    )(*args)
    res = list(res)
    stream = res.pop(0) if has_out else None
    return (stream, *res)


def _rope_tables():
    t = np.arange(SEQ)
    row = (t // GRID_W).astype(np.float64)
    col = (t % GRID_W).astype(np.float64)
    half = HEAD_DIM // 2
    freqs = (np.float32(ROPE_THETA) ** (-np.arange(0, half, 2, dtype=np.float32) / half)).astype(np.float64)
    ang = np.concatenate([row[:, None] * freqs, col[:, None] * freqs], axis=-1).astype(np.float32)
    cos = np.repeat(np.cos(ang.astype(np.float64)), 2, axis=-1)
    sin = np.repeat(np.sin(ang.astype(np.float64)), 2, axis=-1)
    lane_even = (np.arange(HEAD_DIM) % 2 == 0)
    s_even = np.where(lane_even, -sin, 0.0)
    s_odd = np.where(lane_even, 0.0, sin)

    def full(tab, ctx_val):
        tab = np.concatenate([np.full((CTX_LEN, HEAD_DIM), ctx_val), tab], axis=0).astype(np.float32)
        return jnp.asarray(np.tile(tab, (1, LANES // HEAD_DIM)))

    return full(cos, 1.0), full(s_even, 0.0), full(s_odd, 0.0)


def _softmax_pv(s_list, v_list, sink=None):
    m = s_list[0].max(axis=-1, keepdims=True)
    for s in s_list[1:]:
        m = jnp.maximum(m, s.max(axis=-1, keepdims=True))
    if sink is not None:
        m = jnp.maximum(m, sink)
    acc = None
    for s, v in zip(s_list, v_list):
        p = jnp.exp2(s - m).astype(BF16)
        part = jnp.dot(p, v, preferred_element_type=F32)
        acc = part if acc is None else acc + part
    extra = jnp.exp2(sink - m) if sink is not None else None
    return acc, extra


def _pipelined(n_heads, scores_fn, finish_fn, depth=PIPELINE_DEPTH):
    ahead = [scores_fn(h) for h in range(min(depth, n_heads))]
    for h in range(n_heads):
        if h + depth < n_heads:
            ahead.append(scores_fn(h + depth))
        finish_fn(h, ahead.pop(0))


class _PairStore:
    def __init__(self, o_ref, g_ref):
        self.o_ref, self.g_ref, self.held = o_ref, g_ref, None

    def put(self, h, o):
        if h % 2 == 0:
            self.held = o
            return
        lanes = slice((h - 1) * HEAD_DIM, (h + 1) * HEAD_DIM)
        pair = jnp.concatenate([self.held, o], axis=1)
        self.o_ref[:, lanes] = (pair * self.g_ref[:, lanes].astype(F32)).astype(BF16)


def _gqa_normalise(acc, extra, kv):
    den = acc[:, LANES:2 * LANES]
    if extra is not None:
        den = den + extra
    o = acc[:, 0:LANES] / den
    return o[:, kv * HEAD_DIM:(kv + 1) * HEAD_DIM]


def _attn_a_kernel(q_ref, g_ref, k_ref, v_ref, o_ref):
    group = A_HEADS // A_KV_HEADS

    def run(n_keys):
        out = _PairStore(o_ref, g_ref)

        def scores(h):
            kv = h // group
            q = q_ref[:, h * HEAD_DIM:(h + 1) * HEAD_DIM]
            k = k_ref[0:n_keys, kv * HEAD_DIM:(kv + 1) * HEAD_DIM]
            return [lax.dot_general(q, k, NT_DIMS, preferred_element_type=F32)]

        def finish(h, s_list):
            acc, extra = _softmax_pv(s_list, [v_ref[0:n_keys, :]])
            out.put(h, _gqa_normalise(acc, extra, h // group))

        _pipelined(A_HEADS, scores, finish, depth=CTX_PIPELINE_DEPTH if n_keys == CTX_LEN else PIPELINE_DEPTH)

    t = pl.program_id(1)
    pl.when(t == 0)(lambda: run(CTX_LEN))
    pl.when(t > 0)(lambda: run(T_ALL))


def _attn_a(h_even):
    w = A_HEADS * HEAD_DIM
    return pl.pallas_call(
        _attn_a_kernel,
        grid=(BATCH, N_TILES),
        in_specs=[
            pl.BlockSpec((None, ROW_TILE, w), lambda b, t: (b, t, EV_AQ // w)),
            pl.BlockSpec((None, ROW_TILE, w), lambda b, t: (b, t, EV_AG // w)),
            pl.BlockSpec((None, T_ALL, LANES), lambda b, t: (b, 0, EV_AK // LANES)),
            pl.BlockSpec((None, T_ALL, 2 * LANES), lambda b, t: (b, 0, EV_AV // (2 * LANES))),
        ],
        out_specs=pl.BlockSpec((None, ROW_TILE, w), lambda b, t: (b, t, 0)),
        out_shape=jax.ShapeDtypeStruct((BATCH, T_ALL, w), BF16),
        compiler_params=_cparams(2),
    )(h_even, h_even, h_even, h_even)


def _attn_b_kernel(sink_ref, q_ref, g_ref, k_ref, v_ref, o_ref):
    group = B_HEADS // B_KV_HEADS
    t = pl.program_id(1)

    def run(row_spans, band_bias):
        out = _PairStore(o_ref, g_ref)
        k_all = jnp.concatenate([k_ref[pl.ds(s, n), :] for s, n in row_spans], axis=0)
        v_all = jnp.concatenate([v_ref[pl.ds(s, n), :] for s, n in row_spans], axis=0)
        k_kv = [k_all[:, kv * HEAD_DIM:(kv + 1) * HEAD_DIM] for kv in range(B_KV_HEADS)]

        def scores(h):
            q = q_ref[:, h * HEAD_DIM:(h + 1) * HEAD_DIM]
            s = lax.dot_general(q, k_kv[h // group], NT_DIMS, preferred_element_type=F32)
            if band_bias is None:
                return s
            return jnp.concatenate([s[:, 0:CTX_LEN], s[:, CTX_LEN:] + band_bias], axis=1)

        def finish(h, s):
            kv = h // group
            sink = sink_ref[h] * LOG2E
            m = jnp.maximum(s.max(axis=-1, keepdims=True), sink)
            p = jnp.exp2(s - m).astype(BF16)
            acc = jnp.dot(p, v_all, preferred_element_type=F32)
            den = acc[:, LANES:2 * LANES] + jnp.exp2(sink - m)
            out.put(h, (acc[:, 0:LANES] / den)[:, kv * HEAD_DIM:(kv + 1) * HEAD_DIM])

        _pipelined(B_HEADS, scores, finish, depth=CTX_PIPELINE_DEPTH if band_bias is None else B_PIPELINE_DEPTH)

    def lat_tile():
        start = pl.multiple_of(jnp.clip(t * B_TILE - B_WINDOW, CTX_LEN, T_ALL - BAND_B), LANES)
        q_pos = t * B_TILE + lax.broadcasted_iota(jnp.int32, (B_TILE, BAND_B), 0)
        k_pos = start + lax.broadcasted_iota(jnp.int32, (B_TILE, BAND_B), 1)
        bias = jnp.where(jnp.abs(k_pos - q_pos) <= B_WINDOW, 0.0, NEG_INF).astype(F32)
        run([(0, CTX_LEN), (start, BAND_B)], bias)

    ctx_tiles = CTX_LEN // B_TILE
    pl.when(t < ctx_tiles)(lambda: run([(0, CTX_LEN)], None))
    pl.when(t >= ctx_tiles)(lat_tile)


def _attn_b(h_even, sink):
    w = B_HEADS * HEAD_DIM
    return pl.pallas_call(
        _attn_b_kernel,
        grid=(BATCH, T_ALL // B_TILE),
        in_specs=[
            pl.BlockSpec(memory_space=pltpu.SMEM),
            pl.BlockSpec((None, B_TILE, w), lambda b, t: (b, t, EV_BQ // w)),
            pl.BlockSpec((None, B_TILE, w), lambda b, t: (b, t, EV_BG // w)),
            pl.BlockSpec((None, T_ALL, LANES), lambda b, t: (b, 0, EV_BK // LANES)),
            pl.BlockSpec((None, T_ALL, 2 * LANES), lambda b, t: (b, 0, EV_BV // (2 * LANES))),
        ],
        out_specs=pl.BlockSpec((None, B_TILE, w), lambda b, t: (b, t, 0)),
        out_shape=jax.ShapeDtypeStruct((BATCH, T_ALL, w), BF16),
        compiler_params=_cparams(2),
    )(sink, h_even, h_even, h_even, h_even)


CONV_GROUPS = 8


def _conv_tail(acc, sg, lg_ref, lb_ref):
    mu = jnp.mean(acc, axis=-1, keepdims=True)
    ctr = acc - mu
    var = jnp.mean(ctr * ctr, axis=-1, keepdims=True)
    y = ctr * lax.rsqrt(var + EPS) * lg_ref[...] + lb_ref[...]
    return (_silu(y) * sg).astype(BF16)


def _conv_fixup_kernel(ul_ref, uc_ref, ur_ref, sg_ref, w_ref, b_ref, lg_ref, lb_ref, ya_ref, o_ref):
    del ya_ref
    n_win = 3 * HALO
    for b in range(BATCH):
        win = jnp.concatenate([ul_ref[b], uc_ref[b], ur_ref[b]], axis=0)
        acc = jnp.zeros((HALO, C_WIDTH), F32) + b_ref[...]
        for r in range(CONV_GROUPS):
            shifted = win if r == 0 else pltpu.roll(win, n_win - r, 0)
            for k in range(C_CONV):
                off = HALO - C_CONV // 2 + k
                if off % CONV_GROUPS == r:
                    base = off - r
                    acc = acc + shifted[base:base + HALO, :] * w_ref[k:k + 1, :]
        o_ref[b] = _conv_tail(acc, sg_ref[b].astype(F32), lg_ref, lb_ref)


def _conv_fixup(ya, u, h_odd, dw_w, dw_b, ln_g, ln_b):
    per_tile = PROJ_TILE // HALO
    edge_block = lambda k: (k // 2 + 1) * per_tile - 1 + k % 2
    blk = lambda off, col=0: pl.BlockSpec((BATCH, HALO, C_WIDTH), lambda k: (0, edge_block(k) + off, col))
    const = lambda shape: pl.BlockSpec(shape, lambda k: (0,) * len(shape))
    return pl.pallas_call(
        _conv_fixup_kernel,
        grid=(2 * (PROJ_TILES - 1),),
        in_specs=[
            blk(-1), blk(0), blk(1), blk(0, OD_CG // C_WIDTH),
            const((C_CONV, C_WIDTH)), const((1, C_WIDTH)), const((1, C_WIDTH)), const((1, C_WIDTH)),
            pl.BlockSpec(memory_space=pl.ANY),
        ],
        out_specs=blk(0),
        out_shape=jax.ShapeDtypeStruct((BATCH, T_ALL, C_WIDTH), BF16),
        input_output_aliases={8: 0},
        compiler_params=_cparams(1),
    )(u, u, u, h_odd, dw_w, dw_b.reshape(1, C_WIDTH), ln_g.reshape(1, C_WIDTH), ln_b.reshape(1, C_WIDTH), ya)


TILE_GRID_ROWS = ROW_TILE // GRID_W
BAND_GRID_ROWS = BAND_D // GRID_W
N_DR = 2 * NA_KH - 1
N_DC = 2 * NA_KW - 1
PAIR_ENTRIES = 17
PAIR_RIGHT_ONLY, PAIR_LEFT_ONLY = 15, 16


INTERIOR_FIRST_DR = NA_KH // 2 - 1
INTERIOR_LAST_DR = INTERIOR_FIRST_DR + NA_KH - 1


def _pair_halves(entry):
    if entry == 0:
        return None, None
    if entry == PAIR_RIGHT_ONLY:
        return None, INTERIOR_FIRST_DR
    if entry == PAIR_LEFT_ONLY:
        return INTERIOR_LAST_DR, None
    return entry - 1, entry


def _pair_table_kernel(ext_ref, o_ref):
    qc = lax.broadcasted_iota(jnp.int32, (GRID_W, LANES), 0)
    lane = lax.broadcasted_iota(jnp.int32, (GRID_W, LANES), 1)
    kc = jnp.bitwise_and(lane, GRID_W - 1)
    col_start = jnp.clip(qc - NA_KW // 2, 0, GRID_W - NA_KW)
    col_in = jnp.logical_and(kc >= col_start, kc < col_start + NA_KW)
    is_left = lane < GRID_W

    def half(dr, right):
        rows = jnp.broadcast_to(ext_ref[dr:dr + 1, :], (GRID_W, LANES))
        shift = 1 if right else LANES - GRID_W + 1
        return pltpu.roll(rows, shift, 1, stride=1, stride_axis=0)

    for entry in range(PAIR_ENTRIES):
        left_dr, right_dr = _pair_halves(entry)
        block = jnp.full((GRID_W, LANES), NEG_INF, F32)
        if left_dr is not None:
            block = jnp.where(jnp.logical_and(is_left, col_in), half(left_dr, False), block)
        if right_dr is not None:
            block = jnp.where(jnp.logical_and(jnp.logical_not(is_left), col_in), half(right_dr, True), block)
        o_ref[entry] = block


def _na_pair_table(rpb):
    lead = GRID_W - NA_KW
    ext = jnp.pad(rpb.astype(F32) * LOG2E, ((0, 0), (0, 0), (lead, LANES - N_DC - lead)), mode="edge")
    return pl.pallas_call(
        _pair_table_kernel,
        grid=(D_HEADS,),
        in_specs=[pl.BlockSpec((None, N_DR, LANES), lambda h: (h, 0, 0))],
        out_specs=pl.BlockSpec((None, PAIR_ENTRIES, GRID_W, LANES), lambda h: (h, 0, 0, 0)),
        out_shape=jax.ShapeDtypeStruct((D_HEADS, PAIR_ENTRIES, GRID_W, LANES), F32),
        compiler_params=_cparams(1),
    )(ext)


def _na_pair_entry(variant, a, p):
    if variant == "first":
        return 2 * p - a + NA_KH if p < NA_KH // 2 else 0
    if variant == "last":
        return 2 * p - a if p >= (BAND_GRID_ROWS - NA_KH) // 2 else 0
    left_ok = a <= 2 * p < a + NA_KH
    right_ok = a <= 2 * p + 1 < a + NA_KH
    if left_ok and right_ok:
        return 2 * p - a + NA_KH // 2
    if right_ok:
        return PAIR_RIGHT_ONLY
    if left_ok:
        return PAIR_LEFT_ONLY
    return 0


def _attn_d_kernel(first_tile, q_ref, g_ref, k_ref, v_ref, tab_ref, o_ref):
    t = pl.program_id(1) + first_tile

    def normalise(acc):
        return (acc / pltpu.roll(acc, HEAD_DIM, 1))[:, 0:HEAD_DIM]

    def ctx_tile():
        out = _PairStore(o_ref, g_ref)

        def scores(h):
            lanes = slice(h * HEAD_DIM, (h + 1) * HEAD_DIM)
            return [lax.dot_general(q_ref[:, lanes], k_ref[0:CTX_LEN, lanes], NT_DIMS, preferred_element_type=F32)]

        def finish(h, s_list):
            acc, _ = _softmax_pv(s_list, [v_ref[0:CTX_LEN, h * LANES:(h + 1) * LANES]])
            out.put(h, normalise(acc))

        _pipelined(D_HEADS, scores, finish, depth=CTX_PIPELINE_DEPTH)

    def lat_tile():
        out = _PairStore(o_ref, g_ref)
        start = pl.multiple_of(
            jnp.clip((t - 1) * ROW_TILE - (NA_KH // 2) * GRID_W, 0, SEQ - BAND_D) + CTX_LEN, LANES)
        is_first = t == 1
        is_last = t == N_TILES - 1

        def bias(h):
            rows = []
            for a in range(TILE_GRID_ROWS):
                blocks = []
                for p in range(BAND_GRID_ROWS // 2):
                    entry = jnp.where(is_first, _na_pair_entry("first", a, p),
                                      jnp.where(is_last, _na_pair_entry("last", a, p),
                                                _na_pair_entry("interior", a, p)))
                    blocks.append(tab_ref[h, entry])
                rows.append(jnp.concatenate(blocks, axis=1))
            return jnp.concatenate(rows, axis=0)

        def scores(h):
            lanes = slice(h * HEAD_DIM, (h + 1) * HEAD_DIM)
            q = q_ref[:, lanes]
            kb = k_ref[pl.ds(start, BAND_D), :][:, lanes]
            s_ctx = lax.dot_general(q, k_ref[0:CTX_LEN, lanes], NT_DIMS, preferred_element_type=F32)
            s_band = lax.dot_general(q, kb, NT_DIMS, preferred_element_type=F32) + bias(h)
            return [s_ctx, s_band]

        def finish(h, s_list):
            v_lanes = slice(h * LANES, (h + 1) * LANES)
            v_list = [v_ref[0:CTX_LEN, v_lanes], v_ref[pl.ds(start, BAND_D), :][:, v_lanes]]
            acc, _ = _softmax_pv(s_list, v_list)
            out.put(h, normalise(acc))

        _pipelined(D_HEADS, scores, finish)

    if first_tile == 0:
        pl.when(t == 0)(ctx_tile)
        pl.when(t > 0)(lat_tile)
    else:
        lat_tile()


def _attn_d(h_odd, pair_tab, lat_only):
    w = D_HEADS * HEAD_DIM
    f = 1 if lat_only else 0
    return pl.pallas_call(
        functools.partial(_attn_d_kernel, f),
        grid=(BATCH, N_TILES - f),
        in_specs=[
            pl.BlockSpec((None, ROW_TILE, w), lambda b, t: (b, t + f, OD_DQ // w)),
            pl.BlockSpec((None, ROW_TILE, w), lambda b, t: (b, t + f, OD_DG // w)),
            pl.BlockSpec((None, T_ALL, w), lambda b, t: (b, 0, OD_DK // w)),
            pl.BlockSpec((None, T_ALL, 2 * w), lambda b, t: (b, 0, OD_DV // (2 * w))),
            pl.BlockSpec((D_HEADS, PAIR_ENTRIES, GRID_W, LANES), lambda b, t: (0, 0, 0, 0)),
        ],
        out_specs=pl.BlockSpec((None, ROW_TILE, w), lambda b, t: (b, t, 0)),
        out_shape=jax.ShapeDtypeStruct((BATCH, T_ALL - f * CTX_LEN, w), BF16),
        compiler_params=_cparams(2),
    )(h_odd, h_odd, h_odd, h_odd, pair_tab)


FINAL_TILE = 4 * ROW_TILE
FINAL_BLOCKS = FINAL_TILE // ROW_TILE


def _outproj_final_kernel(*refs):
    ya_refs, refs = refs[:FINAL_BLOCKS], refs[FINAL_BLOCKS:]
    yb_ref, w_ref = refs[:2]
    x_refs, (mod_ref, fg_ref, o_ref) = refs[2:2 + FINAL_BLOCKS], refs[2 + FINAL_BLOCKS:]
    half = w_ref.shape[0] // 2
    ya = jnp.concatenate([r[...] for r in ya_refs], axis=0)
    y = jnp.dot(ya, w_ref[0:half, :], preferred_element_type=F32)
    y = y + jnp.dot(yb_ref[...], w_ref[half:, :], preferred_element_type=F32)
    x = jnp.concatenate([r[...] for r in x_refs], axis=0) + mod_ref[:, 2 * D_MODEL:3 * D_MODEL] * y
    ms = jnp.mean(x * x, axis=-1, keepdims=True)
    o_ref[...] = x * lax.rsqrt(ms + EPS) * fg_ref[...]


def _outproj_final(ya, yb, w_out, xs, mods, layer, final_g):
    w_out_stack, j_out = w_out
    half = w_out_stack.shape[1] // 2
    blks = lambda width: [pl.BlockSpec((None, ROW_TILE, width),
                                       lambda b, t, off=off: (b, FINAL_BLOCKS * t + 1 + off, 0))
                          for off in range(FINAL_BLOCKS)]
    return pl.pallas_call(
        _outproj_final_kernel,
        grid=(BATCH, SEQ // FINAL_TILE),
        in_specs=[
            *blks(half),
            pl.BlockSpec((None, FINAL_TILE, half), lambda b, t: (b, t, 0)),
            pl.BlockSpec((None, 2 * half, D_MODEL), lambda b, t: (j_out, 0, 0)),
            *blks(D_MODEL),
            pl.BlockSpec((None, 1, 3 * D_MODEL), lambda b, t: (layer * MOD_ROWS + b, 0, 0)),
            pl.BlockSpec((1, D_MODEL), lambda b, t: (0, 0)),
        ],
        out_specs=pl.BlockSpec((None, FINAL_TILE, D_MODEL), lambda b, t: (b, t, 0)),
        out_shape=jax.ShapeDtypeStruct((BATCH, SEQ, D_MODEL), F32),
        compiler_params=_cparams(2),
    )(*[ya] * FINAL_BLOCKS, yb, w_out_stack, *[xs] * FINAL_BLOCKS, mods, final_g.reshape(1, D_MODEL))


def kernel(x, c, ctx, c_ctx, mod_w, mod_b, norm_g, ev_w_in, ev_w_out, a_q_gain, a_k_gain, b_sink,
           od_w_in, od_w_out, c_dw_w, c_dw_b, c_ln_g, c_ln_b, d_rpb, final_g):
    cond = jnp.concatenate(
        [c, c_ctx[None, :], jnp.zeros((MOD_ROWS - BATCH - 1, D_MODEL), F32)], axis=0)
    mods = _modulation_all(cond, mod_w, mod_b).reshape(DEPTH * MOD_ROWS, 1, 3 * D_MODEL)
    tables = _rope_tables()
    head_of_lane = np.arange(LANES) // HEAD_DIM
    bd = jnp.asarray((head_of_lane[:, None] == head_of_lane[None, :]).astype(np.float32) / HEAD_DIM, BF16)
    ev_w_in_b, ev_w_out_b, od_w_in_b, od_w_out_b = (w.astype(BF16) for w in (ev_w_in, ev_w_out, od_w_in, od_w_out))

    xs = None
    pending = None
    for i in range(DEPTH):
        j = i // 2
        src = dict(x=x, ctx=ctx) if xs is None else dict(xs=xs)
        if i % 2 == 0:
            extra = (a_q_gain[j], a_k_gain[j], tables, bd)
            stream, h_even = _proj("even", i, norm_g[i], (ev_w_in_b, j), mods, out=pending, even_extra=extra, **src)
            pending = (_attn_a(h_even), _attn_b(h_even, b_sink[j]), (ev_w_out_b, j), i)
        else:
            conv = (c_dw_w[j], c_dw_b[j], c_ln_g[j], c_ln_b[j])
            stream, h_odd, u, ya = _proj("odd", i, norm_g[i], (od_w_in_b, j), mods, out=pending, conv=conv, **src)
            ya = _conv_fixup(ya, u, h_odd, *conv)
            last = i == DEPTH - 1
            pending = (ya, _attn_d(h_odd, _na_pair_table(d_rpb[j]), lat_only=last), (od_w_out_b, j), i)
        if stream is not None:
            xs = stream
    ya, yb, w_out, layer = pending
    return _outproj_final(ya, yb, w_out, xs, mods, layer, final_g)
```

```python
import functools

import numpy as np
import jax
import jax.numpy as jnp
from jax import lax
from jax.experimental import pallas as pl
from jax.experimental.pallas import tpu as pltpu

D_MODEL = 1024
BATCH = 8
SEQ = 2048
DEPTH = 4
GRID_W = 64
CTX_LEN = 256
HEAD_DIM = 64
ROPE_THETA = 10000.0
EPS = 1e-6
NEG_INF = -1e30
ATTN_SCALE = HEAD_DIM ** -0.5
A_HEADS = 8
A_KV_HEADS = 2
B_HEADS = 8
B_KV_HEADS = 2
B_WINDOW = 128
C_WIDTH = 512
C_CONV = 31
D_HEADS = 8
NA_KH = 8
NA_KW = 16
ROWS = SEQ // GRID_W

T_ALL = CTX_LEN + SEQ
ROW_TILE = 256
N_TILES = T_ALL // ROW_TILE
PROJ_TILE = 3 * ROW_TILE
PROJ_TILES = T_ALL // PROJ_TILE
LANES = 128
MXU_N = 256
MOD_ROWS = 16
CTX_MOD_ROW = BATCH
VMEM_LIMIT = 48 * 1024 * 1024
PROJ_VMEM_LIMIT = 56 * 1024 * 1024

EVEN_IN = 2560
LOG2E = 1.4426950408889634
Q_SCALE = ATTN_SCALE * LOG2E
CTX_PIPELINE_DEPTH = 8
B_PIPELINE_DEPTH = 3
PIPELINE_DEPTH = 2
HALO = 16
B_TILE = ROW_TILE
BAND_B = B_TILE + 2 * B_WINDOW
BAND_D = 768

F32 = jnp.float32
BF16 = jnp.bfloat16
NT_DIMS = (((1,), (1,)), ((), ()))


def _cparams(n_grid, vmem=VMEM_LIMIT):
    return pltpu.CompilerParams(dimension_semantics=("arbitrary",) * n_grid, vmem_limit_bytes=vmem)


def _silu(v):
    return v * jax.nn.sigmoid(v)


def _mod_kernel(cond_ref, w_ref, b_ref, o_ref):
    cnd = cond_ref[...]
    act = _silu(cnd).astype(BF16)
    o_ref[...] = jnp.dot(act, w_ref[...].astype(BF16), preferred_element_type=F32) + b_ref[...]


def _modulation_all(cond, mod_w, mod_b):
    n_col = 3
    return pl.pallas_call(
        _mod_kernel,
        grid=(DEPTH, n_col),
        in_specs=[
            pl.BlockSpec((MOD_ROWS, D_MODEL), lambda i, j: (0, 0)),
            pl.BlockSpec((None, D_MODEL, D_MODEL), lambda i, j: (i, 0, j)),
            pl.BlockSpec((None, 1, D_MODEL), lambda i, j: (i, 0, j)),
        ],
        out_specs=pl.BlockSpec((None, MOD_ROWS, D_MODEL), lambda i, j: (i, 0, j)),
        out_shape=jax.ShapeDtypeStruct((DEPTH, MOD_ROWS, 3 * D_MODEL), F32),
        compiler_params=_cparams(2),
    )(cond, mod_w, mod_b.reshape(DEPTH, 1, 3 * D_MODEL))


EVEN_OUT = 2816
EV_AQ, EV_AG, EV_BQ, EV_BG, EV_AV, EV_BV, EV_AK, EV_BK = 0, 512, 1024, 1536, 2048, 2304, 2560, 2688

ODD_OUT = 3072
OD_CG, OD_DQ, OD_DK, OD_DG, OD_DV = 0, 512, 1024, 1536, 2048


def _even_pieces():
    pieces = []
    for j in range(4):
        pieces.append((0 + LANES * j, "aq", EV_AQ + LANES * j))
    pieces.append((512, "ak", EV_AK))
    pieces.append((640, "v", EV_AV))
    for j in range(4):
        pieces.append((768 + LANES * j, "gate", EV_AG + LANES * j))
    for j in range(4):
        pieces.append((1280 + LANES * j, "bq", EV_BQ + LANES * j))
    pieces.append((1792, "bk", EV_BK))
    pieces.append((1920, "v", EV_BV))
    for j in range(4):
        pieces.append((2048 + LANES * j, "gate", EV_BG + LANES * j))
    return pieces


def _rope(v, cos, sin_even, sin_odd):
    nxt = pltpu.roll(v, LANES - 1, 1)
    prv = pltpu.roll(v, 1, 1)
    return v * cos + nxt * sin_even + prv * sin_odd


def _head_rms(v, gain, bd_ref):
    ms = jnp.dot((v * v).astype(BF16), bd_ref[...], preferred_element_type=F32)
    return v * lax.rsqrt(ms + EPS) * gain


def _even_epilogue(h_slabs, w_ref, qg_ref, kg_ref, cos_ref, se_ref, so_ref, bd_ref, o_ref):
    cos, s_even, s_odd = cos_ref[...], se_ref[...], so_ref[...]
    pieces = _even_pieces()
    ones = jnp.ones((PROJ_TILE, LANES), BF16)
    n_chunks = EVEN_IN // MXU_N

    h = jnp.concatenate(h_slabs, axis=0)

    def proj(c):
        return jnp.dot(h, w_ref[:, c * MXU_N:(c + 1) * MXU_N], preferred_element_type=F32)

    acc_next = proj(0)
    for c in range(n_chunks):
        acc = acc_next
        if c + 1 < n_chunks:
            acc_next = proj(c + 1)
        for half in range(MXU_N // LANES):
            src = c * MXU_N + half * LANES
            (_, kind, dst), = [p for p in pieces if p[0] == src]
            v = acc[:, half * LANES:(half + 1) * LANES]
            if kind == "aq":
                v = _rope(_head_rms(v, qg_ref[...], bd_ref), cos, s_even, s_odd) * Q_SCALE
            elif kind == "ak":
                v = _rope(_head_rms(v, kg_ref[...], bd_ref), cos, s_even, s_odd)
            elif kind == "bq":
                v = _rope(v, cos, s_even, s_odd) * Q_SCALE
            elif kind == "bk":
                v = _rope(v, cos, s_even, s_odd)
            elif kind == "gate":
                v = _silu(v)
            elif kind == "v":
                o_ref[:, dst + LANES:dst + 2 * LANES] = ones
            o_ref[:, dst:dst + LANES] = v.astype(BF16)


def _odd_epilogue(t, h_slabs, w_ref, conv_refs, o_ref, u_ref, yc_ref):
    h = jnp.concatenate(h_slabs, axis=0)
    dw_ref, db_ref, lg_ref, lb_ref = conv_refs
    n_slabs = PROJ_TILE // ROW_TILE
    n_cc = C_WIDTH // MXU_N

    def proj(col):
        return jnp.dot(h, w_ref[:, col:col + MXU_N], preferred_element_type=F32)

    ones = jnp.ones((PROJ_TILE, HEAD_DIM), F32)
    heads_per_chunk = MXU_N // HEAD_DIM
    held, u_val, sg_val, conv_acc, windows = {}, {}, {}, {}, {}

    def window(s, c):
        zeros = jnp.zeros((HALO, MXU_N), F32)
        lo, hi = s * ROW_TILE, (s + 1) * ROW_TILE
        left = u_val[c][lo - HALO:lo] if s > 0 else zeros
        right = u_val[c][hi:hi + HALO] if s < n_slabs - 1 else zeros
        if s == 1:
            left = jnp.where(t == 0, zeros, left)
        if s == 0:
            right = jnp.where(t == 0, zeros, right)
        return jnp.concatenate([left, u_val[c][lo:hi], right], axis=0)

    def conv_piece(s, c, r):
        lanes = slice(c * MXU_N, (c + 1) * MXU_N)
        n_win = ROW_TILE + 2 * HALO
        if (s, c) not in windows:
            windows[(s, c)] = window(s, c)
        win = windows[(s, c)]
        shifted = win if r == 0 else pltpu.roll(win, n_win - r, 0)
        acc = conv_acc.get((s, c))
        if acc is None:
            acc = jnp.zeros((ROW_TILE, MXU_N), F32) + db_ref[:, lanes]
        for k in range(C_CONV):
            off = HALO - C_CONV // 2 + k
            if off % CONV_GROUPS == r:
                base = off - r
                acc = acc + shifted[base:base + ROW_TILE, :] * dw_ref[k:k + 1, lanes]
        conv_acc[(s, c)] = acc

    def conv_finish(s):
        acc = jnp.concatenate([conv_acc.pop((s, c)) for c in range(n_cc)], axis=1)
        sg = jnp.concatenate([sg_val[c][s * ROW_TILE:(s + 1) * ROW_TILE] for c in range(n_cc)], axis=1)
        yc_ref[s * ROW_TILE:(s + 1) * ROW_TILE, :] = _conv_tail(acc, sg, lg_ref, lb_ref)

    def finish(kind, c, v):
        lanes = slice(c * MXU_N, (c + 1) * MXU_N)
        if kind == "val":
            held[c] = v
        elif kind == "glu":
            u_val[c] = held.pop(c) * jax.nn.sigmoid(v)
            u_ref[:, lanes] = u_val[c]
        elif kind == "v":
            for i in range(heads_per_chunk):
                head = c * heads_per_chunk + i
                ext = jnp.concatenate([v[:, i * HEAD_DIM:(i + 1) * HEAD_DIM], ones], axis=1)
                o_ref[:, OD_DV + head * LANES:OD_DV + (head + 1) * LANES] = ext.astype(BF16)
        else:
            dst = {"cg": OD_CG, "q": OD_DQ, "k": OD_DK, "dg": OD_DG}[kind]
            if kind in ("cg", "dg"):
                v = _silu(v)
                if kind == "cg":
                    sg_val[c] = v
            elif kind == "q":
                v = v * Q_SCALE
            o_ref[:, dst + c * MXU_N:dst + (c + 1) * MXU_N] = v.astype(BF16)

    order = [("val", 0), ("glu", 1), ("cg", 2), ("q", 3), ("k", 4), ("v", 5), ("dg", 6)]
    steps = [(kind, c, g * C_WIDTH + c * MXU_N) for c in range(C_WIDTH // MXU_N) for kind, g in order[:2]]
    steps += [(kind, c, g * C_WIDTH + c * MXU_N) for kind, g in order[2:] for c in range(C_WIDTH // MXU_N)]
    acc_next = proj(steps[0][2])
    for i, (kind, c, _) in enumerate(steps):
        acc = acc_next
        if i + 1 < len(steps):
            acc_next = proj(steps[i + 1][2])
        finish(kind, c, acc)
    for s in range(n_slabs):
        for c in range(n_cc):
            for r in range(CONV_GROUPS):
                conv_piece(s, c, r)
        conv_finish(s)


def _make_proj_kernel(first, has_out, in_kind):
    def kern(*refs):
        it = iter(refs)
        t = pl.program_id(1)
        if first:
            ctx_ref, xa_ref, xb_ref, xc_ref = next(it), next(it), next(it), next(it)
        else:
            x_ref = next(it)
        if has_out:
            ya_ref, yb_ref, wo_ref, cur_ref, cur_ctx_ref = (next(it) for _ in range(5))
        norm_ref, nxt_ref, nxt_ctx_ref, wi_ref = (next(it) for _ in range(4))
        if in_kind == "even":
            extra = [next(it) for _ in range(6)]
        else:
            conv_refs = [next(it) for _ in range(4)]
        if has_out:
            xo_ref = next(it)
        o_ref = next(it)
        if in_kind == "odd":
            u_ref, yc_ref = next(it), next(it)

        h_slabs = []
        for s in range(PROJ_TILE // ROW_TILE):
            rows = slice(s * ROW_TILE, (s + 1) * ROW_TILE)
            is_ctx = jnp.logical_and(t == 0, s == 0)
            if first:
                x = jnp.where(t == 0, ctx_ref[...], xa_ref[...]) if s == 0 else (xb_ref, xc_ref)[s - 1][...]
            else:
                x = x_ref[rows, :]
            if has_out:
                half = wo_ref.shape[0] // 2
                y = jnp.dot(ya_ref[rows, :], wo_ref[0:half, :], preferred_element_type=F32)
                y = y + jnp.dot(yb_ref[rows, :], wo_ref[half:, :], preferred_element_type=F32)
                cur = jnp.where(is_ctx, cur_ctx_ref[...], cur_ref[...]) if s == 0 else cur_ref[...]
                x = x + cur[:, 2 * D_MODEL:3 * D_MODEL] * y
                xo_ref[rows, :] = x
            nxt = jnp.where(is_ctx, nxt_ctx_ref[...], nxt_ref[...]) if s == 0 else nxt_ref[...]
            ms = jnp.mean(x * x, axis=-1, keepdims=True)
            normed = x * lax.rsqrt(ms + EPS) * norm_ref[...]
            h_slabs.append((normed * (1.0 + nxt[:, D_MODEL:2 * D_MODEL]) + nxt[:, 0:D_MODEL]).astype(BF16))
        if in_kind == "even":
            _even_epilogue(h_slabs, wi_ref, *extra, o_ref)
        else:
            _odd_epilogue(t, h_slabs, wi_ref, conv_refs, o_ref, u_ref, yc_ref)

    return kern


def _proj(in_kind, layer_in, norm_g, w_in, mods, *, x=None, ctx=None, xs=None, out=None, even_extra=None,
          conv=None):
    first = xs is None
    has_out = out is not None
    const = lambda shape: pl.BlockSpec(shape, lambda b, t: (0,) * len(shape), pipeline_mode=pl.Buffered(1))
    layer_of = lambda stack, j: pl.BlockSpec((None,) + stack.shape[1:], lambda b, t: (j, 0, 0),
                                             pipeline_mode=pl.Buffered(1))
    tile = lambda width: pl.BlockSpec((None, PROJ_TILE, width), lambda b, t: (b, t, 0))
    mod_sample = lambda layer: pl.BlockSpec((None, 1, 3 * D_MODEL), lambda b, t: (layer * MOD_ROWS + b, 0, 0))
    mod_ctx = lambda layer: pl.BlockSpec((None, 1, 3 * D_MODEL),
                                         lambda b, t: (layer * MOD_ROWS + CTX_MOD_ROW, 0, 0))
    args, specs = [], []
    if first:
        per = PROJ_TILE // ROW_TILE
        blk = lambda off: pl.BlockSpec((None, ROW_TILE, D_MODEL),
                                       lambda b, t: (b, jnp.maximum(per * t + off, 0), 0))
        args += [ctx, x, x, x]
        specs += [pl.BlockSpec((None, CTX_LEN, D_MODEL), lambda b, t: (b, 0, 0)), blk(-1), blk(0), blk(1)]
    else:
        args.append(xs)
        specs.append(tile(D_MODEL))
    aliases = {}
    if has_out:
        ya, yb, (w_out_stack, j_out), layer_out = out
        half = w_out_stack.shape[1] // 2
        args += [ya, yb, w_out_stack, mods, mods]
        specs += [tile(half), tile(half), layer_of(w_out_stack, j_out), mod_sample(layer_out), mod_ctx(layer_out)]
        if not first:
            aliases = {0: 0}
    w_in_stack, j_in = w_in
    args += [norm_g.reshape(1, D_MODEL), mods, mods, w_in_stack]
    specs += [const((1, D_MODEL)), mod_sample(layer_in), mod_ctx(layer_in), layer_of(w_in_stack, j_in)]
    if in_kind == "even":
        q_gain, k_gain, (cos, s_even, s_odd), bd = even_extra
        tab = pl.BlockSpec((PROJ_TILE, LANES), lambda b, t: (t, 0))
        args += [jnp.tile(q_gain, LANES // HEAD_DIM).reshape(1, LANES),
                 jnp.tile(k_gain, LANES // HEAD_DIM).reshape(1, LANES), cos, s_even, s_odd, bd]
        specs += [const((1, LANES)), const((1, LANES)), tab, tab, tab, const((LANES, LANES))]
    else:
        dw_w, dw_b, ln_g, ln_b = conv
        args += [dw_w, dw_b.reshape(1, C_WIDTH), ln_g.reshape(1, C_WIDTH), ln_b.reshape(1, C_WIDTH)]
        specs += [const((C_CONV, C_WIDTH)), const((1, C_WIDTH)), const((1, C_WIDTH)), const((1, C_WIDTH))]
    out_shapes, out_specs = [], []
    if has_out:
        out_shapes.append(jax.ShapeDtypeStruct((BATCH, T_ALL, D_MODEL), F32))
        out_specs.append(tile(D_MODEL))
    if in_kind == "even":
        out_shapes.append(jax.ShapeDtypeStruct((BATCH, T_ALL, EVEN_OUT), BF16))
        out_specs.append(tile(EVEN_OUT))
    else:
        out_shapes += [jax.ShapeDtypeStruct((BATCH, T_ALL, ODD_OUT), BF16),
                       jax.ShapeDtypeStruct((BATCH, T_ALL, C_WIDTH), F32),
                       jax.ShapeDtypeStruct((BATCH, T_ALL, C_WIDTH), BF16)]
        out_specs += [tile(ODD_OUT), tile(C_WIDTH), tile(C_WIDTH)]
    res = pl.pallas_call(
        _make_proj_kernel(first, has_out, in_kind),
        grid=(BATCH, PROJ_TILES),
        in_specs=specs,
        out_specs=out_specs,
        out_shape=out_shapes,
        input_output_aliases=aliases,
        compiler_params=_cparams(2, PROJ_VMEM_LIMIT),
    )(*args)
    res = list(res)
    stream = res.pop(0) if has_out else None
    return (stream, *res)


def _rope_tables():
    t = np.arange(SEQ)
    row = (t // GRID_W).astype(np.float64)
    col = (t % GRID_W).astype(np.float64)
    half = HEAD_DIM // 2
    freqs = (np.float32(ROPE_THETA) ** (-np.arange(0, half, 2, dtype=np.float32) / half)).astype(np.float64)
    ang = np.concatenate([row[:, None] * freqs, col[:, None] * freqs], axis=-1).astype(np.float32)
    cos = np.repeat(np.cos(ang.astype(np.float64)), 2, axis=-1)
    sin = np.repeat(np.sin(ang.astype(np.float64)), 2, axis=-1)
    lane_even = (np.arange(HEAD_DIM) % 2 == 0)
    s_even = np.where(lane_even, -sin, 0.0)
    s_odd = np.where(lane_even, 0.0, sin)

    def full(tab, ctx_val):
        tab = np.concatenate([np.full((CTX_LEN, HEAD_DIM), ctx_val), tab], axis=0).astype(np.float32)
        return jnp.asarray(np.tile(tab, (1, LANES // HEAD_DIM)))

    return full(cos, 1.0), full(s_even, 0.0), full(s_odd, 0.0)


def _softmax_pv(s_list, v_list, sink=None):
    m = s_list[0].max(axis=-1, keepdims=True)
    for s in s_list[1:]:
        m = jnp.maximum(m, s.max(axis=-1, keepdims=True))
    if sink is not None:
        m = jnp.maximum(m, sink)
    acc = None
    for s, v in zip(s_list, v_list):
        p = jnp.exp2(s - m).astype(BF16)
        part = jnp.dot(p, v, preferred_element_type=F32)
        acc = part if acc is None else acc + part
    extra = jnp.exp2(sink - m) if sink is not None else None
    return acc, extra


def _pipelined(n_heads, scores_fn, finish_fn, depth=PIPELINE_DEPTH):
    ahead = [scores_fn(h) for h in range(min(depth, n_heads))]
    for h in range(n_heads):
        if h + depth < n_heads:
            ahead.append(scores_fn(h + depth))
        finish_fn(h, ahead.pop(0))


class _PairStore:
    def __init__(self, o_ref, g_ref):
        self.o_ref, self.g_ref, self.held = o_ref, g_ref, None

    def put(self, h, o):
        if h % 2 == 0:
            self.held = o
            return
        lanes = slice((h - 1) * HEAD_DIM, (h + 1) * HEAD_DIM)
        pair = jnp.concatenate([self.held, o], axis=1)
        self.o_ref[:, lanes] = (pair * self.g_ref[:, lanes].astype(F32)).astype(BF16)


def _gqa_normalise(acc, extra, kv):
    den = acc[:, LANES:2 * LANES]
    if extra is not None:
        den = den + extra
    o = acc[:, 0:LANES] / den
    return o[:, kv * HEAD_DIM:(kv + 1) * HEAD_DIM]


def _attn_a_kernel(q_ref, g_ref, k_ref, v_ref, o_ref):
    group = A_HEADS // A_KV_HEADS

    def run(n_keys):
        out = _PairStore(o_ref, g_ref)

        def scores(h):
            kv = h // group
            q = q_ref[:, h * HEAD_DIM:(h + 1) * HEAD_DIM]
            k = k_ref[0:n_keys, kv * HEAD_DIM:(kv + 1) * HEAD_DIM]
            return [lax.dot_general(q, k, NT_DIMS, preferred_element_type=F32)]

        def finish(h, s_list):
            acc, extra = _softmax_pv(s_list, [v_ref[0:n_keys, :]])
            out.put(h, _gqa_normalise(acc, extra, h // group))

        _pipelined(A_HEADS, scores, finish, depth=CTX_PIPELINE_DEPTH if n_keys == CTX_LEN else PIPELINE_DEPTH)

    t = pl.program_id(1)
    pl.when(t == 0)(lambda: run(CTX_LEN))
    pl.when(t > 0)(lambda: run(T_ALL))


def _attn_b_kernel(sink_ref, q_ref, g_ref, k_ref, v_ref, o_ref):
    group = B_HEADS // B_KV_HEADS
    t = pl.program_id(1)

    def run(row_spans, band_bias):
        out = _PairStore(o_ref, g_ref)
        k_all = jnp.concatenate([k_ref[pl.ds(s, n), :] for s, n in row_spans], axis=0)
        v_all = jnp.concatenate([v_ref[pl.ds(s, n), :] for s, n in row_spans], axis=0)
        k_kv = [k_all[:, kv * HEAD_DIM:(kv + 1) * HEAD_DIM] for kv in range(B_KV_HEADS)]

        def scores(h):
            q = q_ref[:, h * HEAD_DIM:(h + 1) * HEAD_DIM]
            s = lax.dot_general(q, k_kv[h // group], NT_DIMS, preferred_element_type=F32)
            if band_bias is None:
                return s
            return jnp.concatenate([s[:, 0:CTX_LEN], s[:, CTX_LEN:] + band_bias], axis=1)

        def finish(h, s):
            kv = h // group
            sink = sink_ref[h] * LOG2E
            m = jnp.maximum(s.max(axis=-1, keepdims=True), sink)
            p = jnp.exp2(s - m).astype(BF16)
            acc = jnp.dot(p, v_all, preferred_element_type=F32)
            den = acc[:, LANES:2 * LANES] + jnp.exp2(sink - m)
            out.put(h, (acc[:, 0:LANES] / den)[:, kv * HEAD_DIM:(kv + 1) * HEAD_DIM])

        _pipelined(B_HEADS, scores, finish, depth=CTX_PIPELINE_DEPTH if band_bias is None else B_PIPELINE_DEPTH)

    def lat_tile():
        start = pl.multiple_of(jnp.clip(t * B_TILE - B_WINDOW, CTX_LEN, T_ALL - BAND_B), LANES)
        q_pos = t * B_TILE + lax.broadcasted_iota(jnp.int32, (B_TILE, BAND_B), 0)
        k_pos = start + lax.broadcasted_iota(jnp.int32, (B_TILE, BAND_B), 1)
        bias = jnp.where(jnp.abs(k_pos - q_pos) <= B_WINDOW, 0.0, NEG_INF).astype(F32)
        run([(0, CTX_LEN), (start, BAND_B)], bias)

    ctx_tiles = CTX_LEN // B_TILE
    pl.when(t < ctx_tiles)(lambda: run([(0, CTX_LEN)], None))
    pl.when(t >= ctx_tiles)(lat_tile)


def _attn_ab_kernel(sink_ref, qa_ref, ga_ref, ka_ref, va_ref, qb_ref, gb_ref, kb_ref, vb_ref, oa_ref, ob_ref):
    _attn_a_kernel(qa_ref, ga_ref, ka_ref, va_ref, oa_ref)
    _attn_b_kernel(sink_ref, qb_ref, gb_ref, kb_ref, vb_ref, ob_ref)


def _attn_ab(h_even, sink):
    assert B_TILE == ROW_TILE and A_HEADS == B_HEADS
    w = A_HEADS * HEAD_DIM
    tile = lambda col: pl.BlockSpec((None, ROW_TILE, w), lambda b, t: (b, t, col // w))
    keys = lambda col: pl.BlockSpec((None, T_ALL, LANES), lambda b, t: (b, 0, col // LANES))
    values = lambda col: pl.BlockSpec((None, T_ALL, 2 * LANES), lambda b, t: (b, 0, col // (2 * LANES)))
    out_spec = pl.BlockSpec((None, ROW_TILE, w), lambda b, t: (b, t, 0))
    out_shape = jax.ShapeDtypeStruct((BATCH, T_ALL, w), BF16)
    return pl.pallas_call(
        _attn_ab_kernel,
        grid=(BATCH, N_TILES),
        in_specs=[
            pl.BlockSpec(memory_space=pltpu.SMEM),
            tile(EV_AQ), tile(EV_AG), keys(EV_AK), values(EV_AV),
            tile(EV_BQ), tile(EV_BG), keys(EV_BK), values(EV_BV),
        ],
        out_specs=[out_spec, out_spec],
        out_shape=[out_shape, out_shape],
        compiler_params=_cparams(2),
    )(sink, *[h_even] * 8)


CONV_GROUPS = 8


def _conv_tail(acc, sg, lg_ref, lb_ref):
    mu = jnp.mean(acc, axis=-1, keepdims=True)
    ctr = acc - mu
    var = jnp.mean(ctr * ctr, axis=-1, keepdims=True)
    y = ctr * lax.rsqrt(var + EPS) * lg_ref[...] + lb_ref[...]
    return (_silu(y) * sg).astype(BF16)


def _conv_fixup_kernel(ul_ref, uc_ref, ur_ref, sg_ref, w_ref, b_ref, lg_ref, lb_ref, ya_ref, o_ref):
    del ya_ref
    n_win = 3 * HALO
    for b in range(BATCH):
        win = jnp.concatenate([ul_ref[b], uc_ref[b], ur_ref[b]], axis=0)
        acc = jnp.zeros((HALO, C_WIDTH), F32) + b_ref[...]
        for r in range(CONV_GROUPS):
            shifted = win if r == 0 else pltpu.roll(win, n_win - r, 0)
            for k in range(C_CONV):
                off = HALO - C_CONV // 2 + k
                if off % CONV_GROUPS == r:
                    base = off - r
                    acc = acc + shifted[base:base + HALO, :] * w_ref[k:k + 1, :]
        o_ref[b] = _conv_tail(acc, sg_ref[b].astype(F32), lg_ref, lb_ref)


def _conv_fixup(ya, u, h_odd, dw_w, dw_b, ln_g, ln_b):
    per_tile = PROJ_TILE // HALO
    edge_block = lambda k: (k // 2 + 1) * per_tile - 1 + k % 2
    blk = lambda off, col=0: pl.BlockSpec((BATCH, HALO, C_WIDTH), lambda k: (0, edge_block(k) + off, col))
    const = lambda shape: pl.BlockSpec(shape, lambda k: (0,) * len(shape))
    return pl.pallas_call(
        _conv_fixup_kernel,
        grid=(2 * (PROJ_TILES - 1),),
        in_specs=[
            blk(-1), blk(0), blk(1), blk(0, OD_CG // C_WIDTH),
            const((C_CONV, C_WIDTH)), const((1, C_WIDTH)), const((1, C_WIDTH)), const((1, C_WIDTH)),
            pl.BlockSpec(memory_space=pl.ANY),
        ],
        out_specs=blk(0),
        out_shape=jax.ShapeDtypeStruct((BATCH, T_ALL, C_WIDTH), BF16),
        input_output_aliases={8: 0},
        compiler_params=_cparams(1),
    )(u, u, u, h_odd, dw_w, dw_b.reshape(1, C_WIDTH), ln_g.reshape(1, C_WIDTH), ln_b.reshape(1, C_WIDTH), ya)


TILE_GRID_ROWS = ROW_TILE // GRID_W
BAND_GRID_ROWS = BAND_D // GRID_W
N_DR = 2 * NA_KH - 1
N_DC = 2 * NA_KW - 1
PAIR_ENTRIES = 17
PAIR_RIGHT_ONLY, PAIR_LEFT_ONLY = 15, 16


INTERIOR_FIRST_DR = NA_KH // 2 - 1
INTERIOR_LAST_DR = INTERIOR_FIRST_DR + NA_KH - 1


def _pair_halves(entry):
    if entry == 0:
        return None, None
    if entry == PAIR_RIGHT_ONLY:
        return None, INTERIOR_FIRST_DR
    if entry == PAIR_LEFT_ONLY:
        return INTERIOR_LAST_DR, None
    return entry - 1, entry


def _pair_table_kernel(ext_ref, o_ref):
    qc = lax.broadcasted_iota(jnp.int32, (GRID_W, LANES), 0)
    lane = lax.broadcasted_iota(jnp.int32, (GRID_W, LANES), 1)
    kc = jnp.bitwise_and(lane, GRID_W - 1)
    col_start = jnp.clip(qc - NA_KW // 2, 0, GRID_W - NA_KW)
    col_in = jnp.logical_and(kc >= col_start, kc < col_start + NA_KW)
    is_left = lane < GRID_W

    def half(dr, right):
        rows = jnp.broadcast_to(ext_ref[dr:dr + 1, :], (GRID_W, LANES))
        shift = 1 if right else LANES - GRID_W + 1
        return pltpu.roll(rows, shift, 1, stride=1, stride_axis=0)

    for entry in range(PAIR_ENTRIES):
        left_dr, right_dr = _pair_halves(entry)
        block = jnp.full((GRID_W, LANES), NEG_INF, F32)
        if left_dr is not None:
            block = jnp.where(jnp.logical_and(is_left, col_in), half(left_dr, False), block)
        if right_dr is not None:
            block = jnp.where(jnp.logical_and(jnp.logical_not(is_left), col_in), half(right_dr, True), block)
        o_ref[entry] = block


def _na_pair_table(rpb):
    lead = GRID_W - NA_KW
    ext = jnp.pad(rpb.astype(F32) * LOG2E, ((0, 0), (0, 0), (lead, LANES - N_DC - lead)), mode="edge")
    return pl.pallas_call(
        _pair_table_kernel,
        grid=(D_HEADS,),
        in_specs=[pl.BlockSpec((None, N_DR, LANES), lambda h: (h, 0, 0))],
        out_specs=pl.BlockSpec((None, PAIR_ENTRIES, GRID_W, LANES), lambda h: (h, 0, 0, 0)),
        out_shape=jax.ShapeDtypeStruct((D_HEADS, PAIR_ENTRIES, GRID_W, LANES), F32),
        compiler_params=_cparams(1),
    )(ext)


def _na_pair_entry(variant, a, p):
    if variant == "first":
        return 2 * p - a + NA_KH if p < NA_KH // 2 else 0
    if variant == "last":
        return 2 * p - a if p >= (BAND_GRID_ROWS - NA_KH) // 2 else 0
    left_ok = a <= 2 * p < a + NA_KH
    right_ok = a <= 2 * p + 1 < a + NA_KH
    if left_ok and right_ok:
        return 2 * p - a + NA_KH // 2
    if right_ok:
        return PAIR_RIGHT_ONLY
    if left_ok:
        return PAIR_LEFT_ONLY
    return 0


def _attn_d_kernel(first_tile, q_ref, g_ref, k_ref, v_ref, tab_ref, o_ref):
    t = pl.program_id(1) + first_tile

    def normalise(acc):
        return (acc / pltpu.roll(acc, HEAD_DIM, 1))[:, 0:HEAD_DIM]

    def ctx_tile():
        out = _PairStore(o_ref, g_ref)

        def scores(h):
            lanes = slice(h * HEAD_DIM, (h + 1) * HEAD_DIM)
            return [lax.dot_general(q_ref[:, lanes], k_ref[0:CTX_LEN, lanes], NT_DIMS, preferred_element_type=F32)]

        def finish(h, s_list):
            acc, _ = _softmax_pv(s_list, [v_ref[0:CTX_LEN, h * LANES:(h + 1) * LANES]])
            out.put(h, normalise(acc))

        _pipelined(D_HEADS, scores, finish, depth=CTX_PIPELINE_DEPTH)

    def lat_tile():
        out = _PairStore(o_ref, g_ref)
        start = pl.multiple_of(
            jnp.clip((t - 1) * ROW_TILE - (NA_KH // 2) * GRID_W, 0, SEQ - BAND_D) + CTX_LEN, LANES)
        is_first = t == 1
        is_last = t == N_TILES - 1

        def bias(h):
            rows = []
            for a in range(TILE_GRID_ROWS):
                blocks = []
                for p in range(BAND_GRID_ROWS // 2):
                    entry = jnp.where(is_first, _na_pair_entry("first", a, p),
                                      jnp.where(is_last, _na_pair_entry("last", a, p),
                                                _na_pair_entry("interior", a, p)))
                    blocks.append(tab_ref[h, entry])
                rows.append(jnp.concatenate(blocks, axis=1))
            return jnp.concatenate(rows, axis=0)

        def scores(h):
            lanes = slice(h * HEAD_DIM, (h + 1) * HEAD_DIM)
            q = q_ref[:, lanes]
            kb = k_ref[pl.ds(start, BAND_D), :][:, lanes]
            s_ctx = lax.dot_general(q, k_ref[0:CTX_LEN, lanes], NT_DIMS, preferred_element_type=F32)
            s_band = lax.dot_general(q, kb, NT_DIMS, preferred_element_type=F32) + bias(h)
            return [s_ctx, s_band]

        def finish(h, s_list):
            v_lanes = slice(h * LANES, (h + 1) * LANES)
            v_list = [v_ref[0:CTX_LEN, v_lanes], v_ref[pl.ds(start, BAND_D), :][:, v_lanes]]
            acc, _ = _softmax_pv(s_list, v_list)
            out.put(h, normalise(acc))

        _pipelined(D_HEADS, scores, finish)

    if first_tile == 0:
        pl.when(t == 0)(ctx_tile)
        pl.when(t > 0)(lat_tile)
    else:
        lat_tile()


def _attn_d(h_odd, pair_tab, lat_only):
    w = D_HEADS * HEAD_DIM
    f = 1 if lat_only else 0
    return pl.pallas_call(
        functools.partial(_attn_d_kernel, f),
        grid=(BATCH, N_TILES - f),
        in_specs=[
            pl.BlockSpec((None, ROW_TILE, w), lambda b, t: (b, t + f, OD_DQ // w)),
            pl.BlockSpec((None, ROW_TILE, w), lambda b, t: (b, t + f, OD_DG // w)),
            pl.BlockSpec((None, T_ALL, w), lambda b, t: (b, 0, OD_DK // w)),
            pl.BlockSpec((None, T_ALL, 2 * w), lambda b, t: (b, 0, OD_DV // (2 * w))),
            pl.BlockSpec((D_HEADS, PAIR_ENTRIES, GRID_W, LANES), lambda b, t: (0, 0, 0, 0)),
        ],
        out_specs=pl.BlockSpec((None, ROW_TILE, w), lambda b, t: (b, t, 0)),
        out_shape=jax.ShapeDtypeStruct((BATCH, T_ALL - f * CTX_LEN, w), BF16),
        compiler_params=_cparams(2),
    )(h_odd, h_odd, h_odd, h_odd, pair_tab)


FINAL_TILE = 4 * ROW_TILE
FINAL_BLOCKS = FINAL_TILE // ROW_TILE


def _outproj_final_kernel(*refs):
    ya_refs, refs = refs[:FINAL_BLOCKS], refs[FINAL_BLOCKS:]
    yb_ref, w_ref = refs[:2]
    x_refs, (mod_ref, fg_ref, o_ref) = refs[2:2 + FINAL_BLOCKS], refs[2 + FINAL_BLOCKS:]
    half = w_ref.shape[0] // 2
    ya = jnp.concatenate([r[...] for r in ya_refs], axis=0)
    y = jnp.dot(ya, w_ref[0:half, :], preferred_element_type=F32)
    y = y + jnp.dot(yb_ref[...], w_ref[half:, :], preferred_element_type=F32)
    x = jnp.concatenate([r[...] for r in x_refs], axis=0) + mod_ref[:, 2 * D_MODEL:3 * D_MODEL] * y
    ms = jnp.mean(x * x, axis=-1, keepdims=True)
    o_ref[...] = x * lax.rsqrt(ms + EPS) * fg_ref[...]


def _outproj_final(ya, yb, w_out, xs, mods, layer, final_g):
    w_out_stack, j_out = w_out
    half = w_out_stack.shape[1] // 2
    blks = lambda width: [pl.BlockSpec((None, ROW_TILE, width),
                                       lambda b, t, off=off: (b, FINAL_BLOCKS * t + 1 + off, 0))
                          for off in range(FINAL_BLOCKS)]
    return pl.pallas_call(
        _outproj_final_kernel,
        grid=(BATCH, SEQ // FINAL_TILE),
        in_specs=[
            *blks(half),
            pl.BlockSpec((None, FINAL_TILE, half), lambda b, t: (b, t, 0)),
            pl.BlockSpec((None, 2 * half, D_MODEL), lambda b, t: (j_out, 0, 0)),
            *blks(D_MODEL),
            pl.BlockSpec((None, 1, 3 * D_MODEL), lambda b, t: (layer * MOD_ROWS + b, 0, 0)),
            pl.BlockSpec((1, D_MODEL), lambda b, t: (0, 0)),
        ],
        out_specs=pl.BlockSpec((None, FINAL_TILE, D_MODEL), lambda b, t: (b, t, 0)),
        out_shape=jax.ShapeDtypeStruct((BATCH, SEQ, D_MODEL), F32),
        compiler_params=_cparams(2),
    )(*[ya] * FINAL_BLOCKS, yb, w_out_stack, *[xs] * FINAL_BLOCKS, mods, final_g.reshape(1, D_MODEL))


def kernel(x, c, ctx, c_ctx, mod_w, mod_b, norm_g, ev_w_in, ev_w_out, a_q_gain, a_k_gain, b_sink,
           od_w_in, od_w_out, c_dw_w, c_dw_b, c_ln_g, c_ln_b, d_rpb, final_g):
    cond = jnp.concatenate(
        [c, c_ctx[None, :], jnp.zeros((MOD_ROWS - BATCH - 1, D_MODEL), F32)], axis=0)
    mods = _modulation_all(cond, mod_w, mod_b).reshape(DEPTH * MOD_ROWS, 1, 3 * D_MODEL)
    tables = _rope_tables()
    head_of_lane = np.arange(LANES) // HEAD_DIM
    bd = jnp.asarray((head_of_lane[:, None] == head_of_lane[None, :]).astype(np.float32) / HEAD_DIM, BF16)
    ev_w_in_b, ev_w_out_b, od_w_in_b, od_w_out_b = (w.astype(BF16) for w in (ev_w_in, ev_w_out, od_w_in, od_w_out))

    xs = None
    pending = None
    for i in range(DEPTH):
        j = i // 2
        src = dict(x=x, ctx=ctx) if xs is None else dict(xs=xs)
        if i % 2 == 0:
            extra = (a_q_gain[j], a_k_gain[j], tables, bd)
            stream, h_even = _proj("even", i, norm_g[i], (ev_w_in_b, j), mods, out=pending, even_extra=extra, **src)
            pending = (*_attn_ab(h_even, b_sink[j]), (ev_w_out_b, j), i)
        else:
            conv = (c_dw_w[j], c_dw_b[j], c_ln_g[j], c_ln_b[j])
            stream, h_odd, u, ya = _proj("odd", i, norm_g[i], (od_w_in_b, j), mods, out=pending, conv=conv, **src)
            ya = _conv_fixup(ya, u, h_odd, *conv)
            last = i == DEPTH - 1
            pending = (ya, _attn_d(h_odd, _na_pair_table(d_rpb[j]), lat_only=last), (od_w_out_b, j), i)
        if stream is not None:
            xs = stream
    ya, yb, w_out, layer = pending
    return _outproj_final(ya, yb, w_out, xs, mods, layer, final_g)
```

```python
import functools

import numpy as np
import jax
import jax.numpy as jnp
from jax import lax
from jax.experimental import pallas as pl
from jax.experimental.pallas import tpu as pltpu

D_MODEL = 1024
BATCH = 8
SEQ = 2048
DEPTH = 4
GRID_W = 64
CTX_LEN = 256
HEAD_DIM = 64
ROPE_THETA = 10000.0
EPS = 1e-6
NEG_INF = -1e30
ATTN_SCALE = HEAD_DIM ** -0.5
A_HEADS = 8
A_KV_HEADS = 2
B_HEADS = 8
B_KV_HEADS = 2
B_WINDOW = 128
C_WIDTH = 512
C_CONV = 31
D_HEADS = 8
NA_KH = 8
NA_KW = 16
ROWS = SEQ // GRID_W

T_ALL = CTX_LEN + SEQ
ROW_TILE = 256
N_TILES = T_ALL // ROW_TILE
PROJ_TILE = 3 * ROW_TILE
PROJ_TILES = T_ALL // PROJ_TILE
LANES = 128
MXU_N = 256
MOD_ROWS = 16
CTX_MOD_ROW = BATCH
VMEM_LIMIT = 48 * 1024 * 1024
PROJ_VMEM_LIMIT = 56 * 1024 * 1024

EVEN_IN = 2560
LOG2E = 1.4426950408889634
Q_SCALE = ATTN_SCALE * LOG2E
CTX_PIPELINE_DEPTH = 8
B_PIPELINE_DEPTH = 3
PIPELINE_DEPTH = 2
HALO = 16
B_TILE = ROW_TILE
BAND_B = B_TILE + 2 * B_WINDOW
BAND_D = 768

F32 = jnp.float32
BF16 = jnp.bfloat16
NT_DIMS = (((1,), (1,)), ((), ()))
BAND_EXP_DTYPE = BF16


def _cparams(n_grid, vmem=VMEM_LIMIT):
    return pltpu.CompilerParams(dimension_semantics=("arbitrary",) * n_grid, vmem_limit_bytes=vmem)


def _silu(v):
    return v * jax.nn.sigmoid(v)


def _mod_kernel(cond_ref, w_ref, b_ref, o_ref):
    cnd = cond_ref[...]
    act = _silu(cnd).astype(BF16)
    o_ref[...] = jnp.dot(act, w_ref[...].astype(BF16), preferred_element_type=F32) + b_ref[...]


def _modulation_all(cond, mod_w, mod_b):
    n_col = 3
    return pl.pallas_call(
        _mod_kernel,
        grid=(DEPTH, n_col),
        in_specs=[
            pl.BlockSpec((MOD_ROWS, D_MODEL), lambda i, j: (0, 0)),
            pl.BlockSpec((None, D_MODEL, D_MODEL), lambda i, j: (i, 0, j)),
            pl.BlockSpec((None, 1, D_MODEL), lambda i, j: (i, 0, j)),
        ],
        out_specs=pl.BlockSpec((None, MOD_ROWS, D_MODEL), lambda i, j: (i, 0, j)),
        out_shape=jax.ShapeDtypeStruct((DEPTH, MOD_ROWS, 3 * D_MODEL), F32),
        compiler_params=_cparams(2),
    )(cond, mod_w, mod_b.reshape(DEPTH, 1, 3 * D_MODEL))


EVEN_OUT = 2816
EV_AQ, EV_AG, EV_BQ, EV_BG, EV_AV, EV_BV, EV_AK, EV_BK = 0, 512, 1024, 1536, 2048, 2304, 2560, 2688

ODD_OUT = 3072
OD_CG, OD_DQ, OD_DK, OD_DG, OD_DV = 0, 512, 1024, 1536, 2048


def _even_pieces():
    pieces = []
    for j in range(4):
        pieces.append((0 + LANES * j, "aq", EV_AQ + LANES * j))
    pieces.append((512, "ak", EV_AK))
    pieces.append((640, "v", EV_AV))
    for j in range(4):
        pieces.append((768 + LANES * j, "gate", EV_AG + LANES * j))
    for j in range(4):
        pieces.append((1280 + LANES * j, "bq", EV_BQ + LANES * j))
    pieces.append((1792, "bk", EV_BK))
    pieces.append((1920, "v", EV_BV))
    for j in range(4):
        pieces.append((2048 + LANES * j, "gate", EV_BG + LANES * j))
    return pieces


def _rope(v, cos, sin_even, sin_odd):
    nxt = pltpu.roll(v, LANES - 1, 1)
    prv = pltpu.roll(v, 1, 1)
    return v * cos + nxt * sin_even + prv * sin_odd


def _head_rms(v, gain, bd_ref):
    ms = jnp.dot((v * v).astype(BF16), bd_ref[...], preferred_element_type=F32)
    return v * lax.rsqrt(ms + EPS) * gain


def _even_epilogue(h_slabs, w_ref, qg_ref, kg_ref, cos_ref, se_ref, so_ref, bd_ref, o_ref):
    cos, s_even, s_odd = cos_ref[...], se_ref[...], so_ref[...]
    pieces = _even_pieces()
    ones = jnp.ones((PROJ_TILE, LANES), BF16)
    n_chunks = EVEN_IN // MXU_N

    h = jnp.concatenate(h_slabs, axis=0)

    def proj(c):
        return jnp.dot(h, w_ref[:, c * MXU_N:(c + 1) * MXU_N], preferred_element_type=F32)

    acc_next = proj(0)
    for c in range(n_chunks):
        acc = acc_next
        if c + 1 < n_chunks:
            acc_next = proj(c + 1)
        for half in range(MXU_N // LANES):
            src = c * MXU_N + half * LANES
            (_, kind, dst), = [p for p in pieces if p[0] == src]
            v = acc[:, half * LANES:(half + 1) * LANES]
            if kind == "aq":
                v = _rope(_head_rms(v, qg_ref[...], bd_ref), cos, s_even, s_odd) * Q_SCALE
            elif kind == "ak":
                v = _rope(_head_rms(v, kg_ref[...], bd_ref), cos, s_even, s_odd)
            elif kind == "bq":
                v = _rope(v, cos, s_even, s_odd) * Q_SCALE
            elif kind == "bk":
                v = _rope(v, cos, s_even, s_odd)
            elif kind == "gate":
                v = _silu(v)
            elif kind == "v":
                o_ref[:, dst + LANES:dst + 2 * LANES] = ones
            o_ref[:, dst:dst + LANES] = v.astype(BF16)


def _odd_epilogue(t, h_slabs, w_ref, conv_refs, o_ref, u_ref, yc_ref):
    h = jnp.concatenate(h_slabs, axis=0)
    dw_ref, db_ref, lg_ref, lb_ref = conv_refs
    n_slabs = PROJ_TILE // ROW_TILE
    n_cc = C_WIDTH // MXU_N

    def proj(col):
        return jnp.dot(h, w_ref[:, col:col + MXU_N], preferred_element_type=F32)

    ones = jnp.ones((PROJ_TILE, HEAD_DIM), F32)
    heads_per_chunk = MXU_N // HEAD_DIM
    held, u_val, sg_val, conv_acc, windows = {}, {}, {}, {}, {}

    def window(s, c):
        zeros = jnp.zeros((HALO, MXU_N), F32)
        lo, hi = s * ROW_TILE, (s + 1) * ROW_TILE
        left = u_val[c][lo - HALO:lo] if s > 0 else zeros
        right = u_val[c][hi:hi + HALO] if s < n_slabs - 1 else zeros
        if s == 1:
            left = jnp.where(t == 0, zeros, left)
        if s == 0:
            right = jnp.where(t == 0, zeros, right)
        return jnp.concatenate([left, u_val[c][lo:hi], right], axis=0)

    def conv_piece(s, c, r):
        lanes = slice(c * MXU_N, (c + 1) * MXU_N)
        n_win = ROW_TILE + 2 * HALO
        if (s, c) not in windows:
            windows[(s, c)] = window(s, c)
        win = windows[(s, c)]
        shifted = win if r == 0 else pltpu.roll(win, n_win - r, 0)
        acc = conv_acc.get((s, c))
        if acc is None:
            acc = jnp.zeros((ROW_TILE, MXU_N), F32) + db_ref[:, lanes]
        for k in range(C_CONV):
            off = HALO - C_CONV // 2 + k
            if off % CONV_GROUPS == r:
                base = off - r
                acc = acc + shifted[base:base + ROW_TILE, :] * dw_ref[k:k + 1, lanes]
        conv_acc[(s, c)] = acc

    def conv_finish(s):
        acc = jnp.concatenate([conv_acc.pop((s, c)) for c in range(n_cc)], axis=1)
        sg = jnp.concatenate([sg_val[c][s * ROW_TILE:(s + 1) * ROW_TILE] for c in range(n_cc)], axis=1)
        yc_ref[s * ROW_TILE:(s + 1) * ROW_TILE, :] = _conv_tail(acc, sg, lg_ref, lb_ref)

    def finish(kind, c, v):
        lanes = slice(c * MXU_N, (c + 1) * MXU_N)
        if kind == "val":
            held[c] = v
        elif kind == "glu":
            u_val[c] = held.pop(c) * jax.nn.sigmoid(v)
            u_ref[:, lanes] = u_val[c]
        elif kind == "v":
            for i in range(heads_per_chunk):
                head = c * heads_per_chunk + i
                ext = jnp.concatenate([v[:, i * HEAD_DIM:(i + 1) * HEAD_DIM], ones], axis=1)
                o_ref[:, OD_DV + head * LANES:OD_DV + (head + 1) * LANES] = ext.astype(BF16)
        else:
            dst = {"cg": OD_CG, "q": OD_DQ, "k": OD_DK, "dg": OD_DG}[kind]
            if kind in ("cg", "dg"):
                v = _silu(v)
                if kind == "cg":
                    sg_val[c] = v
            elif kind == "q":
                v = v * Q_SCALE
            o_ref[:, dst + c * MXU_N:dst + (c + 1) * MXU_N] = v.astype(BF16)

    order = [("val", 0), ("glu", 1), ("cg", 2), ("q", 3), ("k", 4), ("v", 5), ("dg", 6)]
    steps = [(kind, c, g * C_WIDTH + c * MXU_N) for c in range(C_WIDTH // MXU_N) for kind, g in order[:2]]
    steps += [(kind, c, g * C_WIDTH + c * MXU_N) for kind, g in order[2:] for c in range(C_WIDTH // MXU_N)]
    acc_next = proj(steps[0][2])
    for i, (kind, c, _) in enumerate(steps):
        acc = acc_next
        if i + 1 < len(steps):
            acc_next = proj(steps[i + 1][2])
        finish(kind, c, acc)
    for s in range(n_slabs):
        for c in range(n_cc):
            for r in range(CONV_GROUPS):
                conv_piece(s, c, r)
        conv_finish(s)


def _make_proj_kernel(first, has_out, in_kind):
    def kern(*refs):
        it = iter(refs)
        t = pl.program_id(1)
        if first:
            ctx_ref, xa_ref, xb_ref, xc_ref = next(it), next(it), next(it), next(it)
        else:
            x_ref = next(it)
        if has_out:
            ya_ref, yb_ref, wo_ref, cur_ref, cur_ctx_ref = (next(it) for _ in range(5))
        norm_ref, nxt_ref, nxt_ctx_ref, wi_ref = (next(it) for _ in range(4))
        if in_kind == "even":
            extra = [next(it) for _ in range(6)]
        else:
            conv_refs = [next(it) for _ in range(4)]
        if has_out:
            xo_ref = next(it)
        o_ref = next(it)
        if in_kind == "odd":
            u_ref, yc_ref = next(it), next(it)

        h_slabs = []
        for s in range(PROJ_TILE // ROW_TILE):
            rows = slice(s * ROW_TILE, (s + 1) * ROW_TILE)
            is_ctx = jnp.logical_and(t == 0, s == 0)
            if first:
                x = jnp.where(t == 0, ctx_ref[...], xa_ref[...]) if s == 0 else (xb_ref, xc_ref)[s - 1][...]
            else:
                x = x_ref[rows, :]
            if has_out:
                half = wo_ref.shape[0] // 2
                y = jnp.dot(ya_ref[rows, :], wo_ref[0:half, :], preferred_element_type=F32)
                y = y + jnp.dot(yb_ref[rows, :], wo_ref[half:, :], preferred_element_type=F32)
                cur = jnp.where(is_ctx, cur_ctx_ref[...], cur_ref[...]) if s == 0 else cur_ref[...]
                x = x + cur[:, 2 * D_MODEL:3 * D_MODEL] * y
                xo_ref[rows, :] = x
            nxt = jnp.where(is_ctx, nxt_ctx_ref[...], nxt_ref[...]) if s == 0 else nxt_ref[...]
            ms = jnp.mean(x * x, axis=-1, keepdims=True)
            normed = x * lax.rsqrt(ms + EPS) * norm_ref[...]
            h_slabs.append((normed * (1.0 + nxt[:, D_MODEL:2 * D_MODEL]) + nxt[:, 0:D_MODEL]).astype(BF16))
        if in_kind == "even":
            _even_epilogue(h_slabs, wi_ref, *extra, o_ref)
        else:
            _odd_epilogue(t, h_slabs, wi_ref, conv_refs, o_ref, u_ref, yc_ref)

    return kern


def _proj(in_kind, layer_in, norm_g, w_in, mods, *, x=None, ctx=None, xs=None, out=None, even_extra=None,
          conv=None):
    first = xs is None
    has_out = out is not None
    const = lambda shape: pl.BlockSpec(shape, lambda b, t: (0,) * len(shape), pipeline_mode=pl.Buffered(1))
    layer_of = lambda stack, j: pl.BlockSpec((None,) + stack.shape[1:], lambda b, t: (j, 0, 0),
                                             pipeline_mode=pl.Buffered(1))
    tile = lambda width: pl.BlockSpec((None, PROJ_TILE, width), lambda b, t: (b, t, 0))
    mod_sample = lambda layer: pl.BlockSpec((None, 1, 3 * D_MODEL), lambda b, t: (layer * MOD_ROWS + b, 0, 0))
    mod_ctx = lambda layer: pl.BlockSpec((None, 1, 3 * D_MODEL),
                                         lambda b, t: (layer * MOD_ROWS + CTX_MOD_ROW, 0, 0))
    args, specs = [], []
    if first:
        per = PROJ_TILE // ROW_TILE
        blk = lambda off: pl.BlockSpec((None, ROW_TILE, D_MODEL),
                                       lambda b, t: (b, jnp.maximum(per * t + off, 0), 0))
        args += [ctx, x, x, x]
        specs += [pl.BlockSpec((None, CTX_LEN, D_MODEL), lambda b, t: (b, 0, 0)), blk(-1), blk(0), blk(1)]
    else:
        args.append(xs)
        specs.append(tile(D_MODEL))
    aliases = {}
    if has_out:
        ya, yb, (w_out_stack, j_out), layer_out = out
        half = w_out_stack.shape[1] // 2
        args += [ya, yb, w_out_stack, mods, mods]
        specs += [tile(half), tile(half), layer_of(w_out_stack, j_out), mod_sample(layer_out), mod_ctx(layer_out)]
        if not first:
            aliases = {0: 0}
    w_in_stack, j_in = w_in
    args += [norm_g.reshape(1, D_MODEL), mods, mods, w_in_stack]
    specs += [const((1, D_MODEL)), mod_sample(layer_in), mod_ctx(layer_in), layer_of(w_in_stack, j_in)]
    if in_kind == "even":
        q_gain, k_gain, (cos, s_even, s_odd), bd = even_extra
        tab = pl.BlockSpec((PROJ_TILE, LANES), lambda b, t: (t, 0))
        args += [jnp.tile(q_gain, LANES // HEAD_DIM).reshape(1, LANES),
                 jnp.tile(k_gain, LANES // HEAD_DIM).reshape(1, LANES), cos, s_even, s_odd, bd]
        specs += [const((1, LANES)), const((1, LANES)), tab, tab, tab, const((LANES, LANES))]
    else:
        dw_w, dw_b, ln_g, ln_b = conv
        args += [dw_w, dw_b.reshape(1, C_WIDTH), ln_g.reshape(1, C_WIDTH), ln_b.reshape(1, C_WIDTH)]
        specs += [const((C_CONV, C_WIDTH)), const((1, C_WIDTH)), const((1, C_WIDTH)), const((1, C_WIDTH))]
    out_shapes, out_specs = [], []
    if has_out:
        out_shapes.append(jax.ShapeDtypeStruct((BATCH, T_ALL, D_MODEL), F32))
        out_specs.append(tile(D_MODEL))
    if in_kind == "even":
        out_shapes.append(jax.ShapeDtypeStruct((BATCH, T_ALL, EVEN_OUT), BF16))
        out_specs.append(tile(EVEN_OUT))
    else:
        out_shapes += [jax.ShapeDtypeStruct((BATCH, T_ALL, ODD_OUT), BF16),
                       jax.ShapeDtypeStruct((BATCH, T_ALL, C_WIDTH), F32),
                       jax.ShapeDtypeStruct((BATCH, T_ALL, C_WIDTH), BF16)]
        out_specs += [tile(ODD_OUT), tile(C_WIDTH), tile(C_WIDTH)]
    res = pl.pallas_call(
        _make_proj_kernel(first, has_out, in_kind),
        grid=(BATCH, PROJ_TILES),
        in_specs=specs,
        out_specs=out_specs,
        out_shape=out_shapes,
        input_output_aliases=aliases,
        compiler_params=_cparams(2, PROJ_VMEM_LIMIT),
    )(*args)
    res = list(res)
    stream = res.pop(0) if has_out else None
    return (stream, *res)


def _rope_tables():
    t = np.arange(SEQ)
    row = (t // GRID_W).astype(np.float64)
    col = (t % GRID_W).astype(np.float64)
    half = HEAD_DIM // 2
    freqs = (np.float32(ROPE_THETA) ** (-np.arange(0, half, 2, dtype=np.float32) / half)).astype(np.float64)
    ang = np.concatenate([row[:, None] * freqs, col[:, None] * freqs], axis=-1).astype(np.float32)
    cos = np.repeat(np.cos(ang.astype(np.float64)), 2, axis=-1)
    sin = np.repeat(np.sin(ang.astype(np.float64)), 2, axis=-1)
    lane_even = (np.arange(HEAD_DIM) % 2 == 0)
    s_even = np.where(lane_even, -sin, 0.0)
    s_odd = np.where(lane_even, 0.0, sin)

    def full(tab, ctx_val):
        tab = np.concatenate([np.full((CTX_LEN, HEAD_DIM), ctx_val), tab], axis=0).astype(np.float32)
        return jnp.asarray(np.tile(tab, (1, LANES // HEAD_DIM)))

    return full(cos, 1.0), full(s_even, 0.0), full(s_odd, 0.0)


def _softmax_pv(s_list, v_list, sink=None, exp_dtype=F32):
    m = s_list[0].max(axis=-1, keepdims=True)
    for s in s_list[1:]:
        m = jnp.maximum(m, s.max(axis=-1, keepdims=True))
    if sink is not None:
        m = jnp.maximum(m, sink)
    acc = None
    for s, v in zip(s_list, v_list):
        p = jnp.exp2((s - m).astype(exp_dtype)).astype(BF16)
        part = jnp.dot(p, v, preferred_element_type=F32)
        acc = part if acc is None else acc + part
    extra = jnp.exp2(sink - m) if sink is not None else None
    return acc, extra


def _pipelined(n_heads, scores_fn, finish_fn, depth=PIPELINE_DEPTH):
    ahead = [scores_fn(h) for h in range(min(depth, n_heads))]
    for h in range(n_heads):
        if h + depth < n_heads:
            ahead.append(scores_fn(h + depth))
        finish_fn(h, ahead.pop(0))


class _PairStore:
    def __init__(self, o_ref, g_ref):
        self.o_ref, self.g_ref, self.held = o_ref, g_ref, None

    def put(self, h, o):
        if h % 2 == 0:
            self.held = o
            return
        lanes = slice((h - 1) * HEAD_DIM, (h + 1) * HEAD_DIM)
        pair = jnp.concatenate([self.held, o], axis=1)
        self.o_ref[:, lanes] = (pair * self.g_ref[:, lanes].astype(F32)).astype(BF16)


def _gqa_normalise(acc, extra, kv):
    den = acc[:, LANES:2 * LANES]
    if extra is not None:
        den = den + extra
    o = acc[:, 0:LANES] / den
    return o[:, kv * HEAD_DIM:(kv + 1) * HEAD_DIM]


def _attn_a_kernel(q_ref, g_ref, k_ref, v_ref, o_ref):
    group = A_HEADS // A_KV_HEADS

    def run(n_keys):
        out = _PairStore(o_ref, g_ref)

        def scores(h):
            kv = h // group
            q = q_ref[:, h * HEAD_DIM:(h + 1) * HEAD_DIM]
            k = k_ref[0:n_keys, kv * HEAD_DIM:(kv + 1) * HEAD_DIM]
            return [lax.dot_general(q, k, NT_DIMS, preferred_element_type=F32)]

        def finish(h, s_list):
            acc, extra = _softmax_pv(s_list, [v_ref[0:n_keys, :]])
            out.put(h, _gqa_normalise(acc, extra, h // group))

        _pipelined(A_HEADS, scores, finish, depth=CTX_PIPELINE_DEPTH if n_keys == CTX_LEN else PIPELINE_DEPTH)

    t = pl.program_id(1)
    pl.when(t == 0)(lambda: run(CTX_LEN))
    pl.when(t > 0)(lambda: run(T_ALL))


def _attn_b_kernel(sink_ref, q_ref, g_ref, k_ref, v_ref, o_ref):
    group = B_HEADS // B_KV_HEADS
    t = pl.program_id(1)

    def run(row_spans, band_bias):
        out = _PairStore(o_ref, g_ref)
        k_all = jnp.concatenate([k_ref[pl.ds(s, n), :] for s, n in row_spans], axis=0)
        v_all = jnp.concatenate([v_ref[pl.ds(s, n), :] for s, n in row_spans], axis=0)
        k_kv = [k_all[:, kv * HEAD_DIM:(kv + 1) * HEAD_DIM] for kv in range(B_KV_HEADS)]

        def scores(h):
            q = q_ref[:, h * HEAD_DIM:(h + 1) * HEAD_DIM]
            s = lax.dot_general(q, k_kv[h // group], NT_DIMS, preferred_element_type=F32)
            if band_bias is None:
                return s
            return jnp.concatenate([s[:, 0:CTX_LEN], s[:, CTX_LEN:] + band_bias], axis=1)

        def finish(h, s):
            kv = h // group
            sink = sink_ref[h] * LOG2E
            m = jnp.maximum(s.max(axis=-1, keepdims=True), sink)
            p = jnp.exp2((s - m).astype(BAND_EXP_DTYPE)).astype(BF16)
            acc = jnp.dot(p, v_all, preferred_element_type=F32)
            den = acc[:, LANES:2 * LANES] + jnp.exp2(sink - m)
            out.put(h, (acc[:, 0:LANES] / den)[:, kv * HEAD_DIM:(kv + 1) * HEAD_DIM])

        _pipelined(B_HEADS, scores, finish, depth=CTX_PIPELINE_DEPTH if band_bias is None else B_PIPELINE_DEPTH)

    def lat_tile():
        start = pl.multiple_of(jnp.clip(t * B_TILE - B_WINDOW, CTX_LEN, T_ALL - BAND_B), LANES)
        q_pos = t * B_TILE + lax.broadcasted_iota(jnp.int32, (B_TILE, BAND_B), 0)
        k_pos = start + lax.broadcasted_iota(jnp.int32, (B_TILE, BAND_B), 1)
        bias = jnp.where(jnp.abs(k_pos - q_pos) <= B_WINDOW, 0.0, NEG_INF).astype(F32)
        run([(0, CTX_LEN), (start, BAND_B)], bias)

    ctx_tiles = CTX_LEN // B_TILE
    pl.when(t < ctx_tiles)(lambda: run([(0, CTX_LEN)], None))
    pl.when(t >= ctx_tiles)(lat_tile)


def _attn_ab_kernel(sink_ref, qa_ref, ga_ref, ka_ref, va_ref, qb_ref, gb_ref, kb_ref, vb_ref, oa_ref, ob_ref):
    _attn_a_kernel(qa_ref, ga_ref, ka_ref, va_ref, oa_ref)
    _attn_b_kernel(sink_ref, qb_ref, gb_ref, kb_ref, vb_ref, ob_ref)


def _attn_ab(h_even, sink):
    assert B_TILE == ROW_TILE and A_HEADS == B_HEADS
    w = A_HEADS * HEAD_DIM
    tile = lambda col: pl.BlockSpec((None, ROW_TILE, w), lambda b, t: (b, t, col // w))
    keys = lambda col: pl.BlockSpec((None, T_ALL, LANES), lambda b, t: (b, 0, col // LANES))
    values = lambda col: pl.BlockSpec((None, T_ALL, 2 * LANES), lambda b, t: (b, 0, col // (2 * LANES)))
    out_spec = pl.BlockSpec((None, ROW_TILE, w), lambda b, t: (b, t, 0))
    out_shape = jax.ShapeDtypeStruct((BATCH, T_ALL, w), BF16)
    return pl.pallas_call(
        _attn_ab_kernel,
        grid=(BATCH, N_TILES),
        in_specs=[
            pl.BlockSpec(memory_space=pltpu.SMEM),
            tile(EV_AQ), tile(EV_AG), keys(EV_AK), values(EV_AV),
            tile(EV_BQ), tile(EV_BG), keys(EV_BK), values(EV_BV),
        ],
        out_specs=[out_spec, out_spec],
        out_shape=[out_shape, out_shape],
        compiler_params=_cparams(2),
    )(sink, *[h_even] * 8)


CONV_GROUPS = 8


def _conv_tail(acc, sg, lg_ref, lb_ref):
    mu = jnp.mean(acc, axis=-1, keepdims=True)
    ctr = acc - mu
    var = jnp.mean(ctr * ctr, axis=-1, keepdims=True)
    y = ctr * lax.rsqrt(var + EPS) * lg_ref[...] + lb_ref[...]
    return (_silu(y) * sg).astype(BF16)


def _conv_fixup_kernel(ul_ref, uc_ref, ur_ref, sg_ref, w_ref, b_ref, lg_ref, lb_ref, ya_ref, o_ref):
    del ya_ref
    n_win = 3 * HALO
    for b in range(BATCH):
        win = jnp.concatenate([ul_ref[b], uc_ref[b], ur_ref[b]], axis=0)
        acc = jnp.zeros((HALO, C_WIDTH), F32) + b_ref[...]
        for r in range(CONV_GROUPS):
            shifted = win if r == 0 else pltpu.roll(win, n_win - r, 0)
            for k in range(C_CONV):
                off = HALO - C_CONV // 2 + k
                if off % CONV_GROUPS == r:
                    base = off - r
                    acc = acc + shifted[base:base + HALO, :] * w_ref[k:k + 1, :]
        o_ref[b] = _conv_tail(acc, sg_ref[b].astype(F32), lg_ref, lb_ref)


def _conv_fixup(ya, u, h_odd, dw_w, dw_b, ln_g, ln_b):
    per_tile = PROJ_TILE // HALO
    edge_block = lambda k: (k // 2 + 1) * per_tile - 1 + k % 2
    blk = lambda off, col=0: pl.BlockSpec((BATCH, HALO, C_WIDTH), lambda k: (0, edge_block(k) + off, col))
    const = lambda shape: pl.BlockSpec(shape, lambda k: (0,) * len(shape))
    return pl.pallas_call(
        _conv_fixup_kernel,
        grid=(2 * (PROJ_TILES - 1),),
        in_specs=[
            blk(-1), blk(0), blk(1), blk(0, OD_CG // C_WIDTH),
            const((C_CONV, C_WIDTH)), const((1, C_WIDTH)), const((1, C_WIDTH)), const((1, C_WIDTH)),
            pl.BlockSpec(memory_space=pl.ANY),
        ],
        out_specs=blk(0),
        out_shape=jax.ShapeDtypeStruct((BATCH, T_ALL, C_WIDTH), BF16),
        input_output_aliases={8: 0},
        compiler_params=_cparams(1),
    )(u, u, u, h_odd, dw_w, dw_b.reshape(1, C_WIDTH), ln_g.reshape(1, C_WIDTH), ln_b.reshape(1, C_WIDTH), ya)


TILE_GRID_ROWS = ROW_TILE // GRID_W
BAND_GRID_ROWS = BAND_D // GRID_W
N_DR = 2 * NA_KH - 1
N_DC = 2 * NA_KW - 1
PAIR_ENTRIES = 17
PAIR_RIGHT_ONLY, PAIR_LEFT_ONLY = 15, 16


INTERIOR_FIRST_DR = NA_KH // 2 - 1
INTERIOR_LAST_DR = INTERIOR_FIRST_DR + NA_KH - 1


def _pair_halves(entry):
    if entry == 0:
        return None, None
    if entry == PAIR_RIGHT_ONLY:
        return None, INTERIOR_FIRST_DR
    if entry == PAIR_LEFT_ONLY:
        return INTERIOR_LAST_DR, None
    return entry - 1, entry


def _pair_table_kernel(ext_ref, o_ref):
    qc = lax.broadcasted_iota(jnp.int32, (GRID_W, LANES), 0)
    lane = lax.broadcasted_iota(jnp.int32, (GRID_W, LANES), 1)
    kc = jnp.bitwise_and(lane, GRID_W - 1)
    col_start = jnp.clip(qc - NA_KW // 2, 0, GRID_W - NA_KW)
    col_in = jnp.logical_and(kc >= col_start, kc < col_start + NA_KW)
    is_left = lane < GRID_W

    def half(dr, right):
        rows = jnp.broadcast_to(ext_ref[dr:dr + 1, :], (GRID_W, LANES))
        shift = 1 if right else LANES - GRID_W + 1
        return pltpu.roll(rows, shift, 1, stride=1, stride_axis=0)

    for entry in range(PAIR_ENTRIES):
        left_dr, right_dr = _pair_halves(entry)
        block = jnp.full((GRID_W, LANES), NEG_INF, F32)
        if left_dr is not None:
            block = jnp.where(jnp.logical_and(is_left, col_in), half(left_dr, False), block)
        if right_dr is not None:
            block = jnp.where(jnp.logical_and(jnp.logical_not(is_left), col_in), half(right_dr, True), block)
        o_ref[entry] = block


def _na_pair_table(rpb):
    lead = GRID_W - NA_KW
    ext = jnp.pad(rpb.astype(F32) * LOG2E, ((0, 0), (0, 0), (lead, LANES - N_DC - lead)), mode="edge")
    return pl.pallas_call(
        _pair_table_kernel,
        grid=(D_HEADS,),
        in_specs=[pl.BlockSpec((None, N_DR, LANES), lambda h: (h, 0, 0))],
        out_specs=pl.BlockSpec((None, PAIR_ENTRIES, GRID_W, LANES), lambda h: (h, 0, 0, 0)),
        out_shape=jax.ShapeDtypeStruct((D_HEADS, PAIR_ENTRIES, GRID_W, LANES), F32),
        compiler_params=_cparams(1),
    )(ext)


def _na_pair_entry(variant, a, p):
    if variant == "first":
        return 2 * p - a + NA_KH if p < NA_KH // 2 else 0
    if variant == "last":
        return 2 * p - a if p >= (BAND_GRID_ROWS - NA_KH) // 2 else 0
    left_ok = a <= 2 * p < a + NA_KH
    right_ok = a <= 2 * p + 1 < a + NA_KH
    if left_ok and right_ok:
        return 2 * p - a + NA_KH // 2
    if right_ok:
        return PAIR_RIGHT_ONLY
    if left_ok:
        return PAIR_LEFT_ONLY
    return 0


def _attn_d_kernel(first_tile, q_ref, g_ref, k_ref, v_ref, tab_ref, o_ref):
    t = pl.program_id(1) + first_tile

    def normalise(acc):
        return (acc / pltpu.roll(acc, HEAD_DIM, 1))[:, 0:HEAD_DIM]

    def ctx_tile():
        out = _PairStore(o_ref, g_ref)

        def scores(h):
            lanes = slice(h * HEAD_DIM, (h + 1) * HEAD_DIM)
            return [lax.dot_general(q_ref[:, lanes], k_ref[0:CTX_LEN, lanes], NT_DIMS, preferred_element_type=F32)]

        def finish(h, s_list):
            acc, _ = _softmax_pv(s_list, [v_ref[0:CTX_LEN, h * LANES:(h + 1) * LANES]])
            out.put(h, normalise(acc))

        _pipelined(D_HEADS, scores, finish, depth=CTX_PIPELINE_DEPTH)

    def lat_tile():
        out = _PairStore(o_ref, g_ref)
        start = pl.multiple_of(
            jnp.clip((t - 1) * ROW_TILE - (NA_KH // 2) * GRID_W, 0, SEQ - BAND_D) + CTX_LEN, LANES)
        is_first = t == 1
        is_last = t == N_TILES - 1

        def bias(h):
            rows = []
            for a in range(TILE_GRID_ROWS):
                blocks = []
                for p in range(BAND_GRID_ROWS // 2):
                    entry = jnp.where(is_first, _na_pair_entry("first", a, p),
                                      jnp.where(is_last, _na_pair_entry("last", a, p),
                                                _na_pair_entry("interior", a, p)))
                    blocks.append(tab_ref[h, entry])
                rows.append(jnp.concatenate(blocks, axis=1))
            return jnp.concatenate(rows, axis=0)

        def scores(h):
            lanes = slice(h * HEAD_DIM, (h + 1) * HEAD_DIM)
            q = q_ref[:, lanes]
            kb = k_ref[pl.ds(start, BAND_D), :][:, lanes]
            s_ctx = lax.dot_general(q, k_ref[0:CTX_LEN, lanes], NT_DIMS, preferred_element_type=F32)
            s_band = lax.dot_general(q, kb, NT_DIMS, preferred_element_type=F32) + bias(h)
            return [s_ctx, s_band]

        def finish(h, s_list):
            v_lanes = slice(h * LANES, (h + 1) * LANES)
            v_list = [v_ref[0:CTX_LEN, v_lanes], v_ref[pl.ds(start, BAND_D), :][:, v_lanes]]
            acc, _ = _softmax_pv(s_list, v_list, exp_dtype=BAND_EXP_DTYPE)
            out.put(h, normalise(acc))

        _pipelined(D_HEADS, scores, finish)

    if first_tile == 0:
        pl.when(t == 0)(ctx_tile)
        pl.when(t > 0)(lat_tile)
    else:
        lat_tile()


def _attn_d(h_odd, pair_tab, lat_only):
    w = D_HEADS * HEAD_DIM
    f = 1 if lat_only else 0
    return pl.pallas_call(
        functools.partial(_attn_d_kernel, f),
        grid=(BATCH, N_TILES - f),
        in_specs=[
            pl.BlockSpec((None, ROW_TILE, w), lambda b, t: (b, t + f, OD_DQ // w)),
            pl.BlockSpec((None, ROW_TILE, w), lambda b, t: (b, t + f, OD_DG // w)),
            pl.BlockSpec((None, T_ALL, w), lambda b, t: (b, 0, OD_DK // w)),
            pl.BlockSpec((None, T_ALL, 2 * w), lambda b, t: (b, 0, OD_DV // (2 * w))),
            pl.BlockSpec((D_HEADS, PAIR_ENTRIES, GRID_W, LANES), lambda b, t: (0, 0, 0, 0)),
        ],
        out_specs=pl.BlockSpec((None, ROW_TILE, w), lambda b, t: (b, t, 0)),
        out_shape=jax.ShapeDtypeStruct((BATCH, T_ALL - f * CTX_LEN, w), BF16),
        compiler_params=_cparams(2),
    )(h_odd, h_odd, h_odd, h_odd, pair_tab)


FINAL_TILE = 4 * ROW_TILE
FINAL_BLOCKS = FINAL_TILE // ROW_TILE


def _outproj_final_kernel(*refs):
    ya_refs, refs = refs[:FINAL_BLOCKS], refs[FINAL_BLOCKS:]
    yb_ref, w_ref = refs[:2]
    x_refs, (mod_ref, fg_ref, o_ref) = refs[2:2 + FINAL_BLOCKS], refs[2 + FINAL_BLOCKS:]
    half = w_ref.shape[0] // 2
    ya = jnp.concatenate([r[...] for r in ya_refs], axis=0)
    y = jnp.dot(ya, w_ref[0:half, :], preferred_element_type=F32)
    y = y + jnp.dot(yb_ref[...], w_ref[half:, :], preferred_element_type=F32)
    x = jnp.concatenate([r[...] for r in x_refs], axis=0) + mod_ref[:, 2 * D_MODEL:3 * D_MODEL] * y
    ms = jnp.mean(x * x, axis=-1, keepdims=True)
    o_ref[...] = x * lax.rsqrt(ms + EPS) * fg_ref[...]


def _outproj_final(ya, yb, w_out, xs, mods, layer, final_g):
    w_out_stack, j_out = w_out
    half = w_out_stack.shape[1] // 2
    blks = lambda width: [pl.BlockSpec((None, ROW_TILE, width),
                                       lambda b, t, off=off: (b, FINAL_BLOCKS * t + 1 + off, 0))
                          for off in range(FINAL_BLOCKS)]
    return pl.pallas_call(
        _outproj_final_kernel,
        grid=(BATCH, SEQ // FINAL_TILE),
        in_specs=[
            *blks(half),
            pl.BlockSpec((None, FINAL_TILE, half), lambda b, t: (b, t, 0)),
            pl.BlockSpec((None, 2 * half, D_MODEL), lambda b, t: (j_out, 0, 0)),
            *blks(D_MODEL),
            pl.BlockSpec((None, 1, 3 * D_MODEL), lambda b, t: (layer * MOD_ROWS + b, 0, 0)),
            pl.BlockSpec((1, D_MODEL), lambda b, t: (0, 0)),
        ],
        out_specs=pl.BlockSpec((None, FINAL_TILE, D_MODEL), lambda b, t: (b, t, 0)),
        out_shape=jax.ShapeDtypeStruct((BATCH, SEQ, D_MODEL), F32),
        compiler_params=_cparams(2),
    )(*[ya] * FINAL_BLOCKS, yb, w_out_stack, *[xs] * FINAL_BLOCKS, mods, final_g.reshape(1, D_MODEL))


def kernel(x, c, ctx, c_ctx, mod_w, mod_b, norm_g, ev_w_in, ev_w_out, a_q_gain, a_k_gain, b_sink,
           od_w_in, od_w_out, c_dw_w, c_dw_b, c_ln_g, c_ln_b, d_rpb, final_g):
    cond = jnp.concatenate(
        [c, c_ctx[None, :], jnp.zeros((MOD_ROWS - BATCH - 1, D_MODEL), F32)], axis=0)
    mods = _modulation_all(cond, mod_w, mod_b).reshape(DEPTH * MOD_ROWS, 1, 3 * D_MODEL)
    tables = _rope_tables()
    head_of_lane = np.arange(LANES) // HEAD_DIM
    bd = jnp.asarray((head_of_lane[:, None] == head_of_lane[None, :]).astype(np.float32) / HEAD_DIM, BF16)
    ev_w_in_b, ev_w_out_b, od_w_in_b, od_w_out_b = (w.astype(BF16) for w in (ev_w_in, ev_w_out, od_w_in, od_w_out))

    xs = None
    pending = None
    for i in range(DEPTH):
        j = i // 2
        src = dict(x=x, ctx=ctx) if xs is None else dict(xs=xs)
        if i % 2 == 0:
            extra = (a_q_gain[j], a_k_gain[j], tables, bd)
            stream, h_even = _proj("even", i, norm_g[i], (ev_w_in_b, j), mods, out=pending, even_extra=extra, **src)
            pending = (*_attn_ab(h_even, b_sink[j]), (ev_w_out_b, j), i)
        else:
            conv = (c_dw_w[j], c_dw_b[j], c_ln_g[j], c_ln_b[j])
            stream, h_odd, u, ya = _proj("odd", i, norm_g[i], (od_w_in_b, j), mods, out=pending, conv=conv, **src)
            ya = _conv_fixup(ya, u, h_odd, *conv)
            last = i == DEPTH - 1
            pending = (ya, _attn_d(h_odd, _na_pair_table(d_rpb[j]), lat_only=last), (od_w_out_b, j), i)
        if stream is not None:
            xs = stream
    ya, yb, w_out, layer = pending
    return _outproj_final(ya, yb, w_out, xs, mods, layer, final_g)
```

```python
import functools

import numpy as np
import jax
import jax.numpy as jnp
from jax import lax
from jax.experimental import pallas as pl
from jax.experimental.pallas import tpu as pltpu

D_MODEL = 1024
BATCH = 8
SEQ = 2048
DEPTH = 4
GRID_W = 64
CTX_LEN = 256
HEAD_DIM = 64
ROPE_THETA = 10000.0
EPS = 1e-6
NEG_INF = -1e30
ATTN_SCALE = HEAD_DIM ** -0.5
A_HEADS = 8
A_KV_HEADS = 2
B_HEADS = 8
B_KV_HEADS = 2
B_WINDOW = 128
C_WIDTH = 512
C_CONV = 31
D_HEADS = 8
NA_KH = 8
NA_KW = 16
ROWS = SEQ // GRID_W

T_ALL = CTX_LEN + SEQ
ROW_TILE = 256
N_TILES = T_ALL // ROW_TILE
PROJ_TILE = 3 * ROW_TILE
PROJ_TILES = T_ALL // PROJ_TILE
LANES = 128
MXU_N = 256
MOD_ROWS = 16
CTX_MOD_ROW = BATCH
VMEM_LIMIT = 48 * 1024 * 1024
PROJ_VMEM_LIMIT = 56 * 1024 * 1024

EVEN_IN = 2560
LOG2E = 1.4426950408889634
Q_SCALE = ATTN_SCALE * LOG2E
CTX_PIPELINE_DEPTH = 8
B_PIPELINE_DEPTH = 3
PIPELINE_DEPTH = 2
HALO = 16
B_TILE = ROW_TILE
BAND_B = B_TILE + 2 * B_WINDOW
BAND_D = 768

F32 = jnp.float32
BF16 = jnp.bfloat16
NT_DIMS = (((1,), (1,)), ((), ()))


def _cparams(n_grid, vmem=VMEM_LIMIT):
    return pltpu.CompilerParams(dimension_semantics=("arbitrary",) * n_grid, vmem_limit_bytes=vmem)


def _silu(v):
    return v * jax.nn.sigmoid(v)


def _mod_kernel(cond_ref, w_ref, b_ref, o_ref):
    cnd = cond_ref[...]
    act = _silu(cnd).astype(BF16)
    o_ref[...] = jnp.dot(act, w_ref[...].astype(BF16), preferred_element_type=F32) + b_ref[...]


def _modulation_all(cond, mod_w, mod_b):
    n_col = 3
    return pl.pallas_call(
        _mod_kernel,
        grid=(DEPTH, n_col),
        in_specs=[
            pl.BlockSpec((MOD_ROWS, D_MODEL), lambda i, j: (0, 0)),
            pl.BlockSpec((None, D_MODEL, D_MODEL), lambda i, j: (i, 0, j)),
            pl.BlockSpec((None, 1, D_MODEL), lambda i, j: (i, 0, j)),
        ],
        out_specs=pl.BlockSpec((None, MOD_ROWS, D_MODEL), lambda i, j: (i, 0, j)),
        out_shape=jax.ShapeDtypeStruct((DEPTH, MOD_ROWS, 3 * D_MODEL), F32),
        compiler_params=_cparams(2),
    )(cond, mod_w, mod_b.reshape(DEPTH, 1, 3 * D_MODEL))


EVEN_OUT = 2816
EV_AQ, EV_AG, EV_BQ, EV_BG, EV_AV, EV_BV, EV_AK, EV_BK = 0, 512, 1024, 1536, 2048, 2304, 2560, 2688

ODD_OUT = 3072
OD_CG, OD_DQ, OD_DK, OD_DG, OD_DV = 0, 512, 1024, 1536, 2048


def _even_pieces():
    pieces = []
    for j in range(4):
        pieces.append((0 + LANES * j, "aq", EV_AQ + LANES * j))
    pieces.append((512, "ak", EV_AK))
    pieces.append((640, "v", EV_AV))
    for j in range(4):
        pieces.append((768 + LANES * j, "gate", EV_AG + LANES * j))
    for j in range(4):
        pieces.append((1280 + LANES * j, "bq", EV_BQ + LANES * j))
    pieces.append((1792, "bk", EV_BK))
    pieces.append((1920, "v", EV_BV))
    for j in range(4):
        pieces.append((2048 + LANES * j, "gate", EV_BG + LANES * j))
    return pieces


def _rope(v, cos, sin_even, sin_odd):
    nxt = pltpu.roll(v, LANES - 1, 1)
    prv = pltpu.roll(v, 1, 1)
    return v * cos + nxt * sin_even + prv * sin_odd


def _head_rms(v, gain, bd_ref):
    ms = jnp.dot((v * v).astype(BF16), bd_ref[...], preferred_element_type=F32)
    return v * lax.rsqrt(ms + EPS) * gain


def _even_epilogue(h_slabs, w_ref, qg_ref, kg_ref, cos_ref, se_ref, so_ref, bd_ref, o_ref):
    cos, s_even, s_odd = cos_ref[...], se_ref[...], so_ref[...]
    pieces = _even_pieces()
    ones = jnp.ones((PROJ_TILE, LANES), BF16)
    n_chunks = EVEN_IN // MXU_N

    h = jnp.concatenate(h_slabs, axis=0)

    def proj(c):
        return jnp.dot(h, w_ref[:, c * MXU_N:(c + 1) * MXU_N], preferred_element_type=F32)

    acc_next = proj(0)
    for c in range(n_chunks):
        acc = acc_next
        if c + 1 < n_chunks:
            acc_next = proj(c + 1)
        for half in range(MXU_N // LANES):
            src = c * MXU_N + half * LANES
            (_, kind, dst), = [p for p in pieces if p[0] == src]
            v = acc[:, half * LANES:(half + 1) * LANES]
            if kind == "aq":
                v = _rope(_head_rms(v, qg_ref[...], bd_ref), cos, s_even, s_odd) * Q_SCALE
            elif kind == "ak":
                v = _rope(_head_rms(v, kg_ref[...], bd_ref), cos, s_even, s_odd)
            elif kind == "bq":
                v = _rope(v, cos, s_even, s_odd) * Q_SCALE
            elif kind == "bk":
                v = _rope(v, cos, s_even, s_odd)
            elif kind == "gate":
                v = _silu(v)
            elif kind == "v":
                o_ref[:, dst + LANES:dst + 2 * LANES] = ones
            o_ref[:, dst:dst + LANES] = v.astype(BF16)


def _odd_epilogue(t, h_slabs, w_ref, conv_refs, o_ref, u_ref, yc_ref):
    h = jnp.concatenate(h_slabs, axis=0)
    dw_ref, db_ref, lg_ref, lb_ref = conv_refs
    n_slabs = PROJ_TILE // ROW_TILE
    n_cc = C_WIDTH // MXU_N

    def proj(col):
        return jnp.dot(h, w_ref[:, col:col + MXU_N], preferred_element_type=F32)

    ones = jnp.ones((PROJ_TILE, HEAD_DIM), F32)
    heads_per_chunk = MXU_N // HEAD_DIM
    held, u_val, sg_val, conv_acc, windows = {}, {}, {}, {}, {}

    def window(s, c):
        zeros = jnp.zeros((HALO, MXU_N), F32)
        lo, hi = s * ROW_TILE, (s + 1) * ROW_TILE
        left = u_val[c][lo - HALO:lo] if s > 0 else zeros
        right = u_val[c][hi:hi + HALO] if s < n_slabs - 1 else zeros
        if s == 1:
            left = jnp.where(t == 0, zeros, left)
        if s == 0:
            right = jnp.where(t == 0, zeros, right)
        return jnp.concatenate([left, u_val[c][lo:hi], right], axis=0)

    def conv_piece(s, c, r):
        lanes = slice(c * MXU_N, (c + 1) * MXU_N)
        n_win = ROW_TILE + 2 * HALO
        if (s, c) not in windows:
            windows[(s, c)] = window(s, c)
        win = windows[(s, c)]
        shifted = win if r == 0 else pltpu.roll(win, n_win - r, 0)
        acc = conv_acc.get((s, c))
        if acc is None:
            acc = jnp.zeros((ROW_TILE, MXU_N), F32) + db_ref[:, lanes]
        for k in range(C_CONV):
            off = HALO - C_CONV // 2 + k
            if off % CONV_GROUPS == r:
                base = off - r
                acc = acc + shifted[base:base + ROW_TILE, :] * dw_ref[k:k + 1, lanes]
        conv_acc[(s, c)] = acc

    def conv_finish(s):
        acc = jnp.concatenate([conv_acc.pop((s, c)) for c in range(n_cc)], axis=1)
        sg = jnp.concatenate([sg_val[c][s * ROW_TILE:(s + 1) * ROW_TILE] for c in range(n_cc)], axis=1)
        yc_ref[s * ROW_TILE:(s + 1) * ROW_TILE, :] = _conv_tail(acc, sg, lg_ref, lb_ref)

    def finish(kind, c, v):
        lanes = slice(c * MXU_N, (c + 1) * MXU_N)
        if kind == "val":
            held[c] = v
        elif kind == "glu":
            u_val[c] = held.pop(c) * jax.nn.sigmoid(v)
            u_ref[:, lanes] = u_val[c]
        elif kind == "v":
            for i in range(heads_per_chunk):
                head = c * heads_per_chunk + i
                ext = jnp.concatenate([v[:, i * HEAD_DIM:(i + 1) * HEAD_DIM], ones], axis=1)
                o_ref[:, OD_DV + head * LANES:OD_DV + (head + 1) * LANES] = ext.astype(BF16)
        else:
            dst = {"cg": OD_CG, "q": OD_DQ, "k": OD_DK, "dg": OD_DG}[kind]
            if kind in ("cg", "dg"):
                v = _silu(v)
                if kind == "cg":
                    sg_val[c] = v
            elif kind == "q":
                v = v * Q_SCALE
            o_ref[:, dst + c * MXU_N:dst + (c + 1) * MXU_N] = v.astype(BF16)

    order = [("val", 0), ("glu", 1), ("cg", 2), ("q", 3), ("k", 4), ("v", 5), ("dg", 6)]
    steps = [(kind, c, g * C_WIDTH + c * MXU_N) for c in range(C_WIDTH // MXU_N) for kind, g in order[:2]]
    steps += [(kind, c, g * C_WIDTH + c * MXU_N) for kind, g in order[2:] for c in range(C_WIDTH // MXU_N)]
    acc_next = proj(steps[0][2])
    for i, (kind, c, _) in enumerate(steps):
        acc = acc_next
        if i + 1 < len(steps):
            acc_next = proj(steps[i + 1][2])
        finish(kind, c, acc)
    for s in range(n_slabs):
        for c in range(n_cc):
            for r in range(CONV_GROUPS):
                conv_piece(s, c, r)
        conv_finish(s)


def _make_proj_kernel(first, has_out, in_kind):
    def kern(*refs):
        it = iter(refs)
        t = pl.program_id(1)
        if first:
            ctx_ref, xa_ref, xb_ref, xc_ref = next(it), next(it), next(it), next(it)
        else:
            x_ref = next(it)
        if has_out:
            ya_ref, yb_ref, wo_ref, cur_ref, cur_ctx_ref = (next(it) for _ in range(5))
        norm_ref, nxt_ref, nxt_ctx_ref, wi_ref = (next(it) for _ in range(4))
        if in_kind == "even":
            extra = [next(it) for _ in range(6)]
        else:
            conv_refs = [next(it) for _ in range(4)]
        if has_out:
            xo_ref = next(it)
        o_ref = next(it)
        if in_kind == "odd":
            u_ref, yc_ref = next(it), next(it)

        h_slabs = []
        for s in range(PROJ_TILE // ROW_TILE):
            rows = slice(s * ROW_TILE, (s + 1) * ROW_TILE)
            is_ctx = jnp.logical_and(t == 0, s == 0)
            if first:
                x = jnp.where(t == 0, ctx_ref[...], xa_ref[...]) if s == 0 else (xb_ref, xc_ref)[s - 1][...]
            else:
                x = x_ref[rows, :]
            if has_out:
                half = wo_ref.shape[0] // 2
                y = jnp.dot(ya_ref[rows, :], wo_ref[0:half, :], preferred_element_type=F32)
                y = y + jnp.dot(yb_ref[rows, :], wo_ref[half:, :], preferred_element_type=F32)
                cur = jnp.where(is_ctx, cur_ctx_ref[...], cur_ref[...]) if s == 0 else cur_ref[...]
                x = x + cur[:, 2 * D_MODEL:3 * D_MODEL] * y
                xo_ref[rows, :] = x
            nxt = jnp.where(is_ctx, nxt_ctx_ref[...], nxt_ref[...]) if s == 0 else nxt_ref[...]
            ms = jnp.mean(x * x, axis=-1, keepdims=True)
            normed = x * lax.rsqrt(ms + EPS) * norm_ref[...]
            h_slabs.append((normed * (1.0 + nxt[:, D_MODEL:2 * D_MODEL]) + nxt[:, 0:D_MODEL]).astype(BF16))
        if in_kind == "even":
            _even_epilogue(h_slabs, wi_ref, *extra, o_ref)
        else:
            _odd_epilogue(t, h_slabs, wi_ref, conv_refs, o_ref, u_ref, yc_ref)

    return kern


def _proj(in_kind, layer_in, norm_g, w_in, mods, *, x=None, ctx=None, xs=None, out=None, even_extra=None,
          conv=None):
    first = xs is None
    has_out = out is not None
    const = lambda shape: pl.BlockSpec(shape, lambda b, t: (0,) * len(shape), pipeline_mode=pl.Buffered(1))
    layer_of = lambda stack, j: pl.BlockSpec((None,) + stack.shape[1:], lambda b, t: (j, 0, 0),
                                             pipeline_mode=pl.Buffered(1))
    tile = lambda width: pl.BlockSpec((None, PROJ_TILE, width), lambda b, t: (b, t, 0))
    mod_sample = lambda layer: pl.BlockSpec((None, 1, 3 * D_MODEL), lambda b, t: (layer * MOD_ROWS + b, 0, 0))
    mod_ctx = lambda layer: pl.BlockSpec((None, 1, 3 * D_MODEL),
                                         lambda b, t: (layer * MOD_ROWS + CTX_MOD_ROW, 0, 0))
    args, specs = [], []
    if first:
        per = PROJ_TILE // ROW_TILE
        blk = lambda off: pl.BlockSpec((None, ROW_TILE, D_MODEL),
                                       lambda b, t: (b, jnp.maximum(per * t + off, 0), 0))
        args += [ctx, x, x, x]
        specs += [pl.BlockSpec((None, CTX_LEN, D_MODEL), lambda b, t: (b, 0, 0)), blk(-1), blk(0), blk(1)]
    else:
        args.append(xs)
        specs.append(tile(D_MODEL))
    aliases = {}
    if has_out:
        ya, yb, (w_out_stack, j_out), layer_out = out
        half = w_out_stack.shape[1] // 2
        args += [ya, yb, w_out_stack, mods, mods]
        specs += [tile(half), tile(half), layer_of(w_out_stack, j_out), mod_sample(layer_out), mod_ctx(layer_out)]
        if not first:
            aliases = {0: 0}
    w_in_stack, j_in = w_in
    args += [norm_g.reshape(1, D_MODEL), mods, mods, w_in_stack]
    specs += [const((1, D_MODEL)), mod_sample(layer_in), mod_ctx(layer_in), layer_of(w_in_stack, j_in)]
    if in_kind == "even":
        q_gain, k_gain, (cos, s_even, s_odd), bd = even_extra
        tab = pl.BlockSpec((PROJ_TILE, LANES), lambda b, t: (t, 0))
        args += [jnp.tile(q_gain, LANES // HEAD_DIM).reshape(1, LANES),
                 jnp.tile(k_gain, LANES // HEAD_DIM).reshape(1, LANES), cos, s_even, s_odd, bd]
        specs += [const((1, LANES)), const((1, LANES)), tab, tab, tab, const((LANES, LANES))]
    else:
        dw_w, dw_b, ln_g, ln_b = conv
        args += [dw_w, dw_b.reshape(1, C_WIDTH), ln_g.reshape(1, C_WIDTH), ln_b.reshape(1, C_WIDTH)]
        specs += [const((C_CONV, C_WIDTH)), const((1, C_WIDTH)), const((1, C_WIDTH)), const((1, C_WIDTH))]
    out_shapes, out_specs = [], []
    if has_out:
        out_shapes.append(jax.ShapeDtypeStruct((BATCH, T_ALL, D_MODEL), F32))
        out_specs.append(tile(D_MODEL))
    if in_kind == "even":
        out_shapes.append(jax.ShapeDtypeStruct((BATCH, T_ALL, EVEN_OUT), BF16))
        out_specs.append(tile(EVEN_OUT))
    else:
        out_shapes += [jax.ShapeDtypeStruct((BATCH, T_ALL, ODD_OUT), BF16),
                       jax.ShapeDtypeStruct((BATCH, T_ALL, C_WIDTH), F32),
                       jax.ShapeDtypeStruct((BATCH, T_ALL, C_WIDTH), BF16)]
        out_specs += [tile(ODD_OUT), tile(C_WIDTH), tile(C_WIDTH)]
    res = pl.pallas_call(
        _make_proj_kernel(first, has_out, in_kind),
        grid=(BATCH, PROJ_TILES),
        in_specs=specs,
        out_specs=out_specs,
        out_shape=out_shapes,
        input_output_aliases=aliases,
        compiler_params=_cparams(2, PROJ_VMEM_LIMIT),
    )(*args)
    res = list(res)
    stream = res.pop(0) if has_out else None
    return (stream, *res)


def _rope_tables():
    t = np.arange(SEQ)
    row = (t // GRID_W).astype(np.float64)
    col = (t % GRID_W).astype(np.float64)
    half = HEAD_DIM // 2
    freqs = (np.float32(ROPE_THETA) ** (-np.arange(0, half, 2, dtype=np.float32) / half)).astype(np.float64)
    ang = np.concatenate([row[:, None] * freqs, col[:, None] * freqs], axis=-1).astype(np.float32)
    cos = np.repeat(np.cos(ang.astype(np.float64)), 2, axis=-1)
    sin = np.repeat(np.sin(ang.astype(np.float64)), 2, axis=-1)
    lane_even = (np.arange(HEAD_DIM) % 2 == 0)
    s_even = np.where(lane_even, -sin, 0.0)
    s_odd = np.where(lane_even, 0.0, sin)

    def full(tab, ctx_val):
        tab = np.concatenate([np.full((CTX_LEN, HEAD_DIM), ctx_val), tab], axis=0).astype(np.float32)
        return jnp.asarray(np.tile(tab, (1, LANES // HEAD_DIM)))

    return full(cos, 1.0), full(s_even, 0.0), full(s_odd, 0.0)


def _softmax_pv(s_list, v_list, sink=None, exp_dtype=F32):
    m = s_list[0].max(axis=-1, keepdims=True)
    for s in s_list[1:]:
        m = jnp.maximum(m, s.max(axis=-1, keepdims=True))
    if sink is not None:
        m = jnp.maximum(m, sink)
    acc = None
    for s, v in zip(s_list, v_list):
        p = jnp.exp2((s - m).astype(exp_dtype)).astype(BF16)
        part = jnp.dot(p, v, preferred_element_type=F32)
        acc = part if acc is None else acc + part
    extra = jnp.exp2(sink - m) if sink is not None else None
    return acc, extra


def _pipelined(n_heads, scores_fn, finish_fn, depth=PIPELINE_DEPTH):
    ahead = [scores_fn(h) for h in range(min(depth, n_heads))]
    for h in range(n_heads):
        if h + depth < n_heads:
            ahead.append(scores_fn(h + depth))
        finish_fn(h, ahead.pop(0))


class _PairStore:
    def __init__(self, o_ref, g_ref):
        self.o_ref, self.g_ref, self.held = o_ref, g_ref, None

    def put(self, h, o):
        if h % 2 == 0:
            self.held = o
            return
        lanes = slice((h - 1) * HEAD_DIM, (h + 1) * HEAD_DIM)
        pair = jnp.concatenate([self.held, o], axis=1)
        self.o_ref[:, lanes] = (pair * self.g_ref[:, lanes].astype(F32)).astype(BF16)


def _gqa_normalise(acc, extra, kv):
    den = acc[:, LANES:2 * LANES]
    if extra is not None:
        den = den + extra
    o = acc[:, 0:LANES] / den
    return o[:, kv * HEAD_DIM:(kv + 1) * HEAD_DIM]


def _attn_a_kernel(q_ref, g_ref, k_ref, v_ref, o_ref):
    group = A_HEADS // A_KV_HEADS

    def run(n_keys):
        out = _PairStore(o_ref, g_ref)

        def scores(h):
            kv = h // group
            q = q_ref[:, h * HEAD_DIM:(h + 1) * HEAD_DIM]
            k = k_ref[0:n_keys, kv * HEAD_DIM:(kv + 1) * HEAD_DIM]
            return [lax.dot_general(q, k, NT_DIMS, preferred_element_type=F32)]

        def finish(h, s_list):
            acc, extra = _softmax_pv(s_list, [v_ref[0:n_keys, :]])
            out.put(h, _gqa_normalise(acc, extra, h // group))

        _pipelined(A_HEADS, scores, finish, depth=CTX_PIPELINE_DEPTH if n_keys == CTX_LEN else PIPELINE_DEPTH)

    t = pl.program_id(1)
    pl.when(t == 0)(lambda: run(CTX_LEN))
    pl.when(t > 0)(lambda: run(T_ALL))


def _attn_b_kernel(sink_ref, q_ref, g_ref, k_ref, v_ref, o_ref):
    group = B_HEADS // B_KV_HEADS
    t = pl.program_id(1)

    def run(row_spans, band_bias):
        out = _PairStore(o_ref, g_ref)
        k_all = jnp.concatenate([k_ref[pl.ds(s, n), :] for s, n in row_spans], axis=0)
        v_all = jnp.concatenate([v_ref[pl.ds(s, n), :] for s, n in row_spans], axis=0)
        k_kv = [k_all[:, kv * HEAD_DIM:(kv + 1) * HEAD_DIM] for kv in range(B_KV_HEADS)]

        def scores(h):
            q = q_ref[:, h * HEAD_DIM:(h + 1) * HEAD_DIM]
            s = lax.dot_general(q, k_kv[h // group], NT_DIMS, preferred_element_type=F32)
            if band_bias is None:
                return s
            return jnp.concatenate([s[:, 0:CTX_LEN], s[:, CTX_LEN:] + band_bias], axis=1)

        def finish(h, s):
            kv = h // group
            sink = sink_ref[h] * LOG2E
            m = jnp.maximum(s.max(axis=-1, keepdims=True), sink)
            p = jnp.exp2(s - m).astype(BF16)
            acc = jnp.dot(p, v_all, preferred_element_type=F32)
            den = acc[:, LANES:2 * LANES] + jnp.exp2(sink - m)
            out.put(h, (acc[:, 0:LANES] / den)[:, kv * HEAD_DIM:(kv + 1) * HEAD_DIM])

        _pipelined(B_HEADS, scores, finish, depth=CTX_PIPELINE_DEPTH if band_bias is None else B_PIPELINE_DEPTH)

    def lat_tile():
        start = pl.multiple_of(jnp.clip(t * B_TILE - B_WINDOW, CTX_LEN, T_ALL - BAND_B), LANES)
        q_pos = t * B_TILE + lax.broadcasted_iota(jnp.int32, (B_TILE, BAND_B), 0)
        k_pos = start + lax.broadcasted_iota(jnp.int32, (B_TILE, BAND_B), 1)
        bias = jnp.where(jnp.abs(k_pos - q_pos) <= B_WINDOW, 0.0, NEG_INF).astype(F32)
        run([(0, CTX_LEN), (start, BAND_B)], bias)

    ctx_tiles = CTX_LEN // B_TILE
    pl.when(t < ctx_tiles)(lambda: run([(0, CTX_LEN)], None))
    pl.when(t >= ctx_tiles)(lat_tile)


def _attn_ab_kernel(sink_ref, qa_ref, ga_ref, ka_ref, va_ref, qb_ref, gb_ref, kb_ref, vb_ref, oa_ref, ob_ref):
    _attn_a_kernel(qa_ref, ga_ref, ka_ref, va_ref, oa_ref)
    _attn_b_kernel(sink_ref, qb_ref, gb_ref, kb_ref, vb_ref, ob_ref)


def _attn_ab(h_even, sink):
    assert B_TILE == ROW_TILE and A_HEADS == B_HEADS
    w = A_HEADS * HEAD_DIM
    tile = lambda col: pl.BlockSpec((None, ROW_TILE, w), lambda b, t: (b, t, col // w))
    keys = lambda col: pl.BlockSpec((None, T_ALL, LANES), lambda b, t: (b, 0, col // LANES))
    values = lambda col: pl.BlockSpec((None, T_ALL, 2 * LANES), lambda b, t: (b, 0, col // (2 * LANES)))
    out_spec = pl.BlockSpec((None, ROW_TILE, w), lambda b, t: (b, t, 0))
    out_shape = jax.ShapeDtypeStruct((BATCH, T_ALL, w), BF16)
    return pl.pallas_call(
        _attn_ab_kernel,
        grid=(BATCH, N_TILES),
        in_specs=[
            pl.BlockSpec(memory_space=pltpu.SMEM),
            tile(EV_AQ), tile(EV_AG), keys(EV_AK), values(EV_AV),
            tile(EV_BQ), tile(EV_BG), keys(EV_BK), values(EV_BV),
        ],
        out_specs=[out_spec, out_spec],
        out_shape=[out_shape, out_shape],
        compiler_params=_cparams(2),
    )(sink, *[h_even] * 8)


CONV_GROUPS = 8


def _conv_tail(acc, sg, lg_ref, lb_ref):
    mu = jnp.mean(acc, axis=-1, keepdims=True)
    ctr = acc - mu
    var = jnp.mean(ctr * ctr, axis=-1, keepdims=True)
    y = ctr * lax.rsqrt(var + EPS) * lg_ref[...] + lb_ref[...]
    return (_silu(y) * sg).astype(BF16)


def _conv_fixup_kernel(ul_ref, uc_ref, ur_ref, sg_ref, w_ref, b_ref, lg_ref, lb_ref, ya_ref, o_ref):
    del ya_ref
    n_win = 3 * HALO
    for b in range(BATCH):
        win = jnp.concatenate([ul_ref[b], uc_ref[b], ur_ref[b]], axis=0)
        acc = jnp.zeros((HALO, C_WIDTH), F32) + b_ref[...]
        for r in range(CONV_GROUPS):
            shifted = win if r == 0 else pltpu.roll(win, n_win - r, 0)
            for k in range(C_CONV):
                off = HALO - C_CONV // 2 + k
                if off % CONV_GROUPS == r:
                    base = off - r
                    acc = acc + shifted[base:base + HALO, :] * w_ref[k:k + 1, :]
        o_ref[b] = _conv_tail(acc, sg_ref[b].astype(F32), lg_ref, lb_ref)


def _conv_fixup(ya, u, h_odd, dw_w, dw_b, ln_g, ln_b):
    per_tile = PROJ_TILE // HALO
    edge_block = lambda k: (k // 2 + 1) * per_tile - 1 + k % 2
    blk = lambda off, col=0: pl.BlockSpec((BATCH, HALO, C_WIDTH), lambda k: (0, edge_block(k) + off, col))
    const = lambda shape: pl.BlockSpec(shape, lambda k: (0,) * len(shape))
    return pl.pallas_call(
        _conv_fixup_kernel,
        grid=(2 * (PROJ_TILES - 1),),
        in_specs=[
            blk(-1), blk(0), blk(1), blk(0, OD_CG // C_WIDTH),
            const((C_CONV, C_WIDTH)), const((1, C_WIDTH)), const((1, C_WIDTH)), const((1, C_WIDTH)),
            pl.BlockSpec(memory_space=pl.ANY),
        ],
        out_specs=blk(0),
        out_shape=jax.ShapeDtypeStruct((BATCH, T_ALL, C_WIDTH), BF16),
        input_output_aliases={8: 0},
        compiler_params=_cparams(1),
    )(u, u, u, h_odd, dw_w, dw_b.reshape(1, C_WIDTH), ln_g.reshape(1, C_WIDTH), ln_b.reshape(1, C_WIDTH), ya)


TILE_GRID_ROWS = ROW_TILE // GRID_W
BAND_GRID_ROWS = BAND_D // GRID_W
N_DR = 2 * NA_KH - 1
N_DC = 2 * NA_KW - 1
PAIR_ENTRIES = 17
PAIR_RIGHT_ONLY, PAIR_LEFT_ONLY = 15, 16


INTERIOR_FIRST_DR = NA_KH // 2 - 1
INTERIOR_LAST_DR = INTERIOR_FIRST_DR + NA_KH - 1


def _pair_halves(entry):
    if entry == 0:
        return None, None
    if entry == PAIR_RIGHT_ONLY:
        return None, INTERIOR_FIRST_DR
    if entry == PAIR_LEFT_ONLY:
        return INTERIOR_LAST_DR, None
    return entry - 1, entry


def _pair_table_kernel(ext_ref, o_ref):
    qc = lax.broadcasted_iota(jnp.int32, (GRID_W, LANES), 0)
    lane = lax.broadcasted_iota(jnp.int32, (GRID_W, LANES), 1)
    kc = jnp.bitwise_and(lane, GRID_W - 1)
    col_start = jnp.clip(qc - NA_KW // 2, 0, GRID_W - NA_KW)
    col_in = jnp.logical_and(kc >= col_start, kc < col_start + NA_KW)
    is_left = lane < GRID_W

    def half(dr, right):
        rows = jnp.broadcast_to(ext_ref[dr:dr + 1, :], (GRID_W, LANES))
        shift = 1 if right else LANES - GRID_W + 1
        return pltpu.roll(rows, shift, 1, stride=1, stride_axis=0)

    for entry in range(PAIR_ENTRIES):
        left_dr, right_dr = _pair_halves(entry)
        block = jnp.full((GRID_W, LANES), NEG_INF, F32)
        if left_dr is not None:
            block = jnp.where(jnp.logical_and(is_left, col_in), half(left_dr, False), block)
        if right_dr is not None:
            block = jnp.where(jnp.logical_and(jnp.logical_not(is_left), col_in), half(right_dr, True), block)
        o_ref[entry] = block


def _na_pair_table(rpb):
    lead = GRID_W - NA_KW
    ext = jnp.pad(rpb.astype(F32) * LOG2E, ((0, 0), (0, 0), (lead, LANES - N_DC - lead)), mode="edge")
    return pl.pallas_call(
        _pair_table_kernel,
        grid=(D_HEADS,),
        in_specs=[pl.BlockSpec((None, N_DR, LANES), lambda h: (h, 0, 0))],
        out_specs=pl.BlockSpec((None, PAIR_ENTRIES, GRID_W, LANES), lambda h: (h, 0, 0, 0)),
        out_shape=jax.ShapeDtypeStruct((D_HEADS, PAIR_ENTRIES, GRID_W, LANES), F32),
        compiler_params=_cparams(1),
    )(ext)


def _na_pair_entry(variant, a, p):
    if variant == "first":
        return 2 * p - a + NA_KH if p < NA_KH // 2 else 0
    if variant == "last":
        return 2 * p - a if p >= (BAND_GRID_ROWS - NA_KH) // 2 else 0
    left_ok = a <= 2 * p < a + NA_KH
    right_ok = a <= 2 * p + 1 < a + NA_KH
    if left_ok and right_ok:
        return 2 * p - a + NA_KH // 2
    if right_ok:
        return PAIR_RIGHT_ONLY
    if left_ok:
        return PAIR_LEFT_ONLY
    return 0


def _attn_d_kernel(first_tile, q_ref, g_ref, k_ref, v_ref, tab_ref, o_ref):
    t = pl.program_id(1) + first_tile

    def normalise(acc):
        return (acc / pltpu.roll(acc, HEAD_DIM, 1))[:, 0:HEAD_DIM]

    def ctx_tile():
        out = _PairStore(o_ref, g_ref)

        def scores(h):
            lanes = slice(h * HEAD_DIM, (h + 1) * HEAD_DIM)
            return [lax.dot_general(q_ref[:, lanes], k_ref[0:CTX_LEN, lanes], NT_DIMS, preferred_element_type=F32)]

        def finish(h, s_list):
            acc, _ = _softmax_pv(s_list, [v_ref[0:CTX_LEN, h * LANES:(h + 1) * LANES]])
            out.put(h, normalise(acc))

        _pipelined(D_HEADS, scores, finish, depth=CTX_PIPELINE_DEPTH)

    def lat_tile():
        out = _PairStore(o_ref, g_ref)
        start = pl.multiple_of(
            jnp.clip((t - 1) * ROW_TILE - (NA_KH // 2) * GRID_W, 0, SEQ - BAND_D) + CTX_LEN, LANES)
        is_first = t == 1
        is_last = t == N_TILES - 1

        def bias(h):
            rows = []
            for a in range(TILE_GRID_ROWS):
                blocks = []
                for p in range(BAND_GRID_ROWS // 2):
                    entry = jnp.where(is_first, _na_pair_entry("first", a, p),
                                      jnp.where(is_last, _na_pair_entry("last", a, p),
                                                _na_pair_entry("interior", a, p)))
                    blocks.append(tab_ref[h, entry])
                rows.append(jnp.concatenate(blocks, axis=1))
            return jnp.concatenate(rows, axis=0)

        def scores(h):
            lanes = slice(h * HEAD_DIM, (h + 1) * HEAD_DIM)
            q = q_ref[:, lanes]
            kb = k_ref[pl.ds(start, BAND_D), :][:, lanes]
            s_ctx = lax.dot_general(q, k_ref[0:CTX_LEN, lanes], NT_DIMS, preferred_element_type=F32)
            s_band = lax.dot_general(q, kb, NT_DIMS, preferred_element_type=F32) + bias(h)
            return [s_ctx, s_band]

        def finish(h, s_list):
            v_lanes = slice(h * LANES, (h + 1) * LANES)
            v_list = [v_ref[0:CTX_LEN, v_lanes], v_ref[pl.ds(start, BAND_D), :][:, v_lanes]]
            acc, _ = _softmax_pv(s_list, v_list, exp_dtype=BF16)
            out.put(h, normalise(acc))

        _pipelined(D_HEADS, scores, finish)

    if first_tile == 0:
        pl.when(t == 0)(ctx_tile)
        pl.when(t > 0)(lat_tile)
    else:
        lat_tile()


def _attn_d(h_odd, pair_tab, lat_only):
    w = D_HEADS * HEAD_DIM
    f = 1 if lat_only else 0
    return pl.pallas_call(
        functools.partial(_attn_d_kernel, f),
        grid=(BATCH, N_TILES - f),
        in_specs=[
            pl.BlockSpec((None, ROW_TILE, w), lambda b, t: (b, t + f, OD_DQ // w)),
            pl.BlockSpec((None, ROW_TILE, w), lambda b, t: (b, t + f, OD_DG // w)),
            pl.BlockSpec((None, T_ALL, w), lambda b, t: (b, 0, OD_DK // w)),
            pl.BlockSpec((None, T_ALL, 2 * w), lambda b, t: (b, 0, OD_DV // (2 * w))),
            pl.BlockSpec((D_HEADS, PAIR_ENTRIES, GRID_W, LANES), lambda b, t: (0, 0, 0, 0)),
        ],
        out_specs=pl.BlockSpec((None, ROW_TILE, w), lambda b, t: (b, t, 0)),
        out_shape=jax.ShapeDtypeStruct((BATCH, T_ALL - f * CTX_LEN, w), BF16),
        compiler_params=_cparams(2),
    )(h_odd, h_odd, h_odd, h_odd, pair_tab)


FINAL_TILE = 4 * ROW_TILE
FINAL_BLOCKS = FINAL_TILE // ROW_TILE


def _outproj_final_kernel(*refs):
    ya_refs, refs = refs[:FINAL_BLOCKS], refs[FINAL_BLOCKS:]
    yb_ref, w_ref = refs[:2]
    x_refs, (mod_ref, fg_ref, o_ref) = refs[2:2 + FINAL_BLOCKS], refs[2 + FINAL_BLOCKS:]
    half = w_ref.shape[0] // 2
    ya = jnp.concatenate([r[...] for r in ya_refs], axis=0)
    y = jnp.dot(ya, w_ref[0:half, :], preferred_element_type=F32)
    y = y + jnp.dot(yb_ref[...], w_ref[half:, :], preferred_element_type=F32)
    x = jnp.concatenate([r[...] for r in x_refs], axis=0) + mod_ref[:, 2 * D_MODEL:3 * D_MODEL] * y
    ms = jnp.mean(x * x, axis=-1, keepdims=True)
    o_ref[...] = x * lax.rsqrt(ms + EPS) * fg_ref[...]


def _outproj_final(ya, yb, w_out, xs, mods, layer, final_g):
    w_out_stack, j_out = w_out
    half = w_out_stack.shape[1] // 2
    blks = lambda width: [pl.BlockSpec((None, ROW_TILE, width),
                                       lambda b, t, off=off: (b, FINAL_BLOCKS * t + 1 + off, 0))
                          for off in range(FINAL_BLOCKS)]
    return pl.pallas_call(
        _outproj_final_kernel,
        grid=(BATCH, SEQ // FINAL_TILE),
        in_specs=[
            *blks(half),
            pl.BlockSpec((None, FINAL_TILE, half), lambda b, t: (b, t, 0)),
            pl.BlockSpec((None, 2 * half, D_MODEL), lambda b, t: (j_out, 0, 0)),
            *blks(D_MODEL),
            pl.BlockSpec((None, 1, 3 * D_MODEL), lambda b, t: (layer * MOD_ROWS + b, 0, 0)),
            pl.BlockSpec((1, D_MODEL), lambda b, t: (0, 0)),
        ],
        out_specs=pl.BlockSpec((None, FINAL_TILE, D_MODEL), lambda b, t: (b, t, 0)),
        out_shape=jax.ShapeDtypeStruct((BATCH, SEQ, D_MODEL), F32),
        compiler_params=_cparams(2),
    )(*[ya] * FINAL_BLOCKS, yb, w_out_stack, *[xs] * FINAL_BLOCKS, mods, final_g.reshape(1, D_MODEL))


def kernel(x, c, ctx, c_ctx, mod_w, mod_b, norm_g, ev_w_in, ev_w_out, a_q_gain, a_k_gain, b_sink,
           od_w_in, od_w_out, c_dw_w, c_dw_b, c_ln_g, c_ln_b, d_rpb, final_g):
    cond = jnp.concatenate(
        [c, c_ctx[None, :], jnp.zeros((MOD_ROWS - BATCH - 1, D_MODEL), F32)], axis=0)
    mods = _modulation_all(cond, mod_w, mod_b).reshape(DEPTH * MOD_ROWS, 1, 3 * D_MODEL)
    tables = _rope_tables()
    head_of_lane = np.arange(LANES) // HEAD_DIM
    bd = jnp.asarray((head_of_lane[:, None] == head_of_lane[None, :]).astype(np.float32) / HEAD_DIM, BF16)
    ev_w_in_b, ev_w_out_b, od_w_in_b, od_w_out_b = (w.astype(BF16) for w in (ev_w_in, ev_w_out, od_w_in, od_w_out))

    xs = None
    pending = None
    for i in range(DEPTH):
        j = i // 2
        src = dict(x=x, ctx=ctx) if xs is None else dict(xs=xs)
        if i % 2 == 0:
            extra = (a_q_gain[j], a_k_gain[j], tables, bd)
            stream, h_even = _proj("even", i, norm_g[i], (ev_w_in_b, j), mods, out=pending, even_extra=extra, **src)
            pending = (*_attn_ab(h_even, b_sink[j]), (ev_w_out_b, j), i)
        else:
            conv = (c_dw_w[j], c_dw_b[j], c_ln_g[j], c_ln_b[j])
            stream, h_odd, u, ya = _proj("odd", i, norm_g[i], (od_w_in_b, j), mods, out=pending, conv=conv, **src)
            ya = _conv_fixup(ya, u, h_odd, *conv)
            last = i == DEPTH - 1
            pending = (ya, _attn_d(h_odd, _na_pair_table(d_rpb[j]), lat_only=last), (od_w_out_b, j), i)
        if stream is not None:
            xs = stream
    ya, yb, w_out, layer = pending
    return _outproj_final(ya, yb, w_out, xs, mods, layer, final_g)
```

```python
import functools

import numpy as np
import jax
import jax.numpy as jnp
from jax import lax
from jax.experimental import pallas as pl
from jax.experimental.pallas import tpu as pltpu

D_MODEL = 1024
BATCH = 8
SEQ = 2048
DEPTH = 4
GRID_W = 64
CTX_LEN = 256
HEAD_DIM = 64
ROPE_THETA = 10000.0
EPS = 1e-6
NEG_INF = -1e30
ATTN_SCALE = HEAD_DIM ** -0.5
A_HEADS = 8
A_KV_HEADS = 2
B_HEADS = 8
B_KV_HEADS = 2
B_WINDOW = 128
C_WIDTH = 512
C_CONV = 31
D_HEADS = 8
NA_KH = 8
NA_KW = 16
ROWS = SEQ // GRID_W

T_ALL = CTX_LEN + SEQ
ROW_TILE = 256
N_TILES = T_ALL // ROW_TILE
PROJ_TILE = 3 * ROW_TILE
PROJ_TILES = T_ALL // PROJ_TILE
LANES = 128
MXU_N = 256
MOD_ROWS = 16
CTX_MOD_ROW = BATCH
VMEM_LIMIT = 48 * 1024 * 1024
PROJ_VMEM_LIMIT = 56 * 1024 * 1024

EVEN_IN = 2560
LOG2E = 1.4426950408889634
Q_SCALE = ATTN_SCALE * LOG2E
CTX_PIPELINE_DEPTH = 8
B_PIPELINE_DEPTH = 3
PIPELINE_DEPTH = 2
HALO = 16
A_KEY_CHUNK = T_ALL
B_TILE = ROW_TILE
BAND_B = B_TILE + 2 * B_WINDOW
BAND_D = 768

F32 = jnp.float32
BF16 = jnp.bfloat16
NT_DIMS = (((1,), (1,)), ((), ()))


def _cparams(n_grid, vmem=VMEM_LIMIT):
    return pltpu.CompilerParams(dimension_semantics=("arbitrary",) * n_grid, vmem_limit_bytes=vmem)


def _silu(v):
    return v * jax.nn.sigmoid(v)


def _mod_kernel(cond_ref, w_ref, b_ref, o_ref):
    cnd = cond_ref[...]
    act = _silu(cnd).astype(BF16)
    o_ref[...] = jnp.dot(act, w_ref[...].astype(BF16), preferred_element_type=F32) + b_ref[...]


def _modulation_all(cond, mod_w, mod_b):
    n_col = 3
    return pl.pallas_call(
        _mod_kernel,
        grid=(DEPTH, n_col),
        in_specs=[
            pl.BlockSpec((MOD_ROWS, D_MODEL), lambda i, j: (0, 0)),
            pl.BlockSpec((None, D_MODEL, D_MODEL), lambda i, j: (i, 0, j)),
            pl.BlockSpec((None, 1, D_MODEL), lambda i, j: (i, 0, j)),
        ],
        out_specs=pl.BlockSpec((None, MOD_ROWS, D_MODEL), lambda i, j: (i, 0, j)),
        out_shape=jax.ShapeDtypeStruct((DEPTH, MOD_ROWS, 3 * D_MODEL), F32),
        compiler_params=_cparams(2),
    )(cond, mod_w, mod_b.reshape(DEPTH, 1, 3 * D_MODEL))


EVEN_OUT = 2816
EV_AQ, EV_AG, EV_BQ, EV_BG, EV_AV, EV_BV, EV_AK, EV_BK = 0, 512, 1024, 1536, 2048, 2304, 2560, 2688

ODD_OUT = 3072
OD_CG, OD_DQ, OD_DK, OD_DG, OD_DV = 0, 512, 1024, 1536, 2048


def _even_pieces():
    pieces = []
    for j in range(4):
        pieces.append((0 + LANES * j, "aq", EV_AQ + LANES * j))
    pieces.append((512, "ak", EV_AK))
    pieces.append((640, "v", EV_AV))
    for j in range(4):
        pieces.append((768 + LANES * j, "gate", EV_AG + LANES * j))
    for j in range(4):
        pieces.append((1280 + LANES * j, "bq", EV_BQ + LANES * j))
    pieces.append((1792, "bk", EV_BK))
    pieces.append((1920, "v", EV_BV))
    for j in range(4):
        pieces.append((2048 + LANES * j, "gate", EV_BG + LANES * j))
    return pieces


def _rope(v, cos, sin_even, sin_odd):
    nxt = pltpu.roll(v, LANES - 1, 1)
    prv = pltpu.roll(v, 1, 1)
    return v * cos + nxt * sin_even + prv * sin_odd


def _head_rms(v, gain, bd_ref):
    ms = jnp.dot((v * v).astype(BF16), bd_ref[...], preferred_element_type=F32)
    return v * lax.rsqrt(ms + EPS) * gain


def _even_epilogue(h_slabs, w_ref, qg_ref, kg_ref, cos_ref, se_ref, so_ref, bd_ref, o_ref):
    cos, s_even, s_odd = cos_ref[...], se_ref[...], so_ref[...]
    pieces = _even_pieces()
    ones = jnp.ones((PROJ_TILE, LANES), BF16)
    n_chunks = EVEN_IN // MXU_N

    h = jnp.concatenate(h_slabs, axis=0)

    def proj(c):
        return jnp.dot(h, w_ref[:, c * MXU_N:(c + 1) * MXU_N], preferred_element_type=F32)

    acc_next = proj(0)
    for c in range(n_chunks):
        acc = acc_next
        if c + 1 < n_chunks:
            acc_next = proj(c + 1)
        for half in range(MXU_N // LANES):
            src = c * MXU_N + half * LANES
            (_, kind, dst), = [p for p in pieces if p[0] == src]
            v = acc[:, half * LANES:(half + 1) * LANES]
            if kind == "aq":
                v = _rope(_head_rms(v, qg_ref[...], bd_ref), cos, s_even, s_odd) * Q_SCALE
            elif kind == "ak":
                v = _rope(_head_rms(v, kg_ref[...], bd_ref), cos, s_even, s_odd)
            elif kind == "bq":
                v = _rope(v, cos, s_even, s_odd) * Q_SCALE
            elif kind == "bk":
                v = _rope(v, cos, s_even, s_odd)
            elif kind == "gate":
                v = _silu(v)
            elif kind == "v":
                o_ref[:, dst + LANES:dst + 2 * LANES] = ones
            o_ref[:, dst:dst + LANES] = v.astype(BF16)


def _odd_epilogue(t, h_slabs, w_ref, conv_refs, o_ref, u_ref, yc_ref):
    h = jnp.concatenate(h_slabs, axis=0)
    dw_ref, db_ref, lg_ref, lb_ref = conv_refs
    n_slabs = PROJ_TILE // ROW_TILE
    n_cc = C_WIDTH // MXU_N

    def proj(col):
        return jnp.dot(h, w_ref[:, col:col + MXU_N], preferred_element_type=F32)

    ones = jnp.ones((PROJ_TILE, HEAD_DIM), F32)
    heads_per_chunk = MXU_N // HEAD_DIM
    held, u_val, sg_val, conv_acc, windows = {}, {}, {}, {}, {}

    def window(s, c):
        zeros = jnp.zeros((HALO, MXU_N), F32)
        lo, hi = s * ROW_TILE, (s + 1) * ROW_TILE
        left = u_val[c][lo - HALO:lo] if s > 0 else zeros
        right = u_val[c][hi:hi + HALO] if s < n_slabs - 1 else zeros
        if s == 1:
            left = jnp.where(t == 0, zeros, left)
        if s == 0:
            right = jnp.where(t == 0, zeros, right)
        return jnp.concatenate([left, u_val[c][lo:hi], right], axis=0)

    def conv_piece(s, c, r):
        lanes = slice(c * MXU_N, (c + 1) * MXU_N)
        n_win = ROW_TILE + 2 * HALO
        if (s, c) not in windows:
            windows[(s, c)] = window(s, c)
        win = windows[(s, c)]
        shifted = win if r == 0 else pltpu.roll(win, n_win - r, 0)
        acc = conv_acc.get((s, c))
        if acc is None:
            acc = jnp.zeros((ROW_TILE, MXU_N), F32) + db_ref[:, lanes]
        for k in range(C_CONV):
            off = HALO - C_CONV // 2 + k
            if off % CONV_GROUPS == r:
                base = off - r
                acc = acc + shifted[base:base + ROW_TILE, :] * dw_ref[k:k + 1, lanes]
        conv_acc[(s, c)] = acc

    def conv_finish(s):
        acc = jnp.concatenate([conv_acc.pop((s, c)) for c in range(n_cc)], axis=1)
        sg = jnp.concatenate([sg_val[c][s * ROW_TILE:(s + 1) * ROW_TILE] for c in range(n_cc)], axis=1)
        yc_ref[s * ROW_TILE:(s + 1) * ROW_TILE, :] = _conv_tail(acc, sg, lg_ref, lb_ref)

    def finish(kind, c, v):
        lanes = slice(c * MXU_N, (c + 1) * MXU_N)
        if kind == "val":
            held[c] = v
        elif kind == "glu":
            u_val[c] = held.pop(c) * jax.nn.sigmoid(v)
            u_ref[:, lanes] = u_val[c]
        elif kind == "v":
            for i in range(heads_per_chunk):
                head = c * heads_per_chunk + i
                ext = jnp.concatenate([v[:, i * HEAD_DIM:(i + 1) * HEAD_DIM], ones], axis=1)
                o_ref[:, OD_DV + head * LANES:OD_DV + (head + 1) * LANES] = ext.astype(BF16)
        else:
            dst = {"cg": OD_CG, "q": OD_DQ, "k": OD_DK, "dg": OD_DG}[kind]
            if kind in ("cg", "dg"):
                v = _silu(v)
                if kind == "cg":
                    sg_val[c] = v
            elif kind == "q":
                v = v * Q_SCALE
            o_ref[:, dst + c * MXU_N:dst + (c + 1) * MXU_N] = v.astype(BF16)

    order = [("val", 0), ("glu", 1), ("cg", 2), ("q", 3), ("k", 4), ("v", 5), ("dg", 6)]
    steps = [(kind, c, g * C_WIDTH + c * MXU_N) for c in range(C_WIDTH // MXU_N) for kind, g in order[:2]]
    steps += [(kind, c, g * C_WIDTH + c * MXU_N) for kind, g in order[2:] for c in range(C_WIDTH // MXU_N)]
    acc_next = proj(steps[0][2])
    for i, (kind, c, _) in enumerate(steps):
        acc = acc_next
        if i + 1 < len(steps):
            acc_next = proj(steps[i + 1][2])
        finish(kind, c, acc)
    for s in range(n_slabs):
        for c in range(n_cc):
            for r in range(CONV_GROUPS):
                conv_piece(s, c, r)
        conv_finish(s)


def _make_proj_kernel(first, has_out, in_kind):
    def kern(*refs):
        it = iter(refs)
        t = pl.program_id(1)
        if first:
            ctx_ref, xa_ref, xb_ref, xc_ref = next(it), next(it), next(it), next(it)
        else:
            x_ref = next(it)
        if has_out:
            ya_ref, yb_ref, wo_ref, cur_ref, cur_ctx_ref = (next(it) for _ in range(5))
        norm_ref, nxt_ref, nxt_ctx_ref, wi_ref = (next(it) for _ in range(4))
        if in_kind == "even":
            extra = [next(it) for _ in range(6)]
        else:
            conv_refs = [next(it) for _ in range(4)]
        if has_out:
            xo_ref = next(it)
        o_ref = next(it)
        if in_kind == "odd":
            u_ref, yc_ref = next(it), next(it)

        h_slabs = []
        for s in range(PROJ_TILE // ROW_TILE):
            rows = slice(s * ROW_TILE, (s + 1) * ROW_TILE)
            is_ctx = jnp.logical_and(t == 0, s == 0)
            if first:
                x = jnp.where(t == 0, ctx_ref[...], xa_ref[...]) if s == 0 else (xb_ref, xc_ref)[s - 1][...]
            else:
                x = x_ref[rows, :]
            if has_out:
                half = wo_ref.shape[0] // 2
                y = jnp.dot(ya_ref[rows, :], wo_ref[0:half, :], preferred_element_type=F32)
                y = y + jnp.dot(yb_ref[rows, :], wo_ref[half:, :], preferred_element_type=F32)
                cur = jnp.where(is_ctx, cur_ctx_ref[...], cur_ref[...]) if s == 0 else cur_ref[...]
                x = x + cur[:, 2 * D_MODEL:3 * D_MODEL] * y
                xo_ref[rows, :] = x
            nxt = jnp.where(is_ctx, nxt_ctx_ref[...], nxt_ref[...]) if s == 0 else nxt_ref[...]
            ms = jnp.mean(x * x, axis=-1, keepdims=True)
            normed = x * lax.rsqrt(ms + EPS) * norm_ref[...]
            h_slabs.append((normed * (1.0 + nxt[:, D_MODEL:2 * D_MODEL]) + nxt[:, 0:D_MODEL]).astype(BF16))
        if in_kind == "even":
            _even_epilogue(h_slabs, wi_ref, *extra, o_ref)
        else:
            _odd_epilogue(t, h_slabs, wi_ref, conv_refs, o_ref, u_ref, yc_ref)

    return kern


def _proj(in_kind, layer_in, norm_g, w_in, mods, *, x=None, ctx=None, xs=None, out=None, even_extra=None,
          conv=None):
    first = xs is None
    has_out = out is not None
    const = lambda shape: pl.BlockSpec(shape, lambda b, t: (0,) * len(shape), pipeline_mode=pl.Buffered(1))
    layer_of = lambda stack, j: pl.BlockSpec((None,) + stack.shape[1:], lambda b, t: (j, 0, 0),
                                             pipeline_mode=pl.Buffered(1))
    tile = lambda width: pl.BlockSpec((None, PROJ_TILE, width), lambda b, t: (b, t, 0))
    mod_sample = lambda layer: pl.BlockSpec((None, 1, 3 * D_MODEL), lambda b, t: (layer * MOD_ROWS + b, 0, 0))
    mod_ctx = lambda layer: pl.BlockSpec((None, 1, 3 * D_MODEL),
                                         lambda b, t: (layer * MOD_ROWS + CTX_MOD_ROW, 0, 0))
    args, specs = [], []
    if first:
        per = PROJ_TILE // ROW_TILE
        blk = lambda off: pl.BlockSpec((None, ROW_TILE, D_MODEL),
                                       lambda b, t: (b, jnp.maximum(per * t + off, 0), 0))
        args += [ctx, x, x, x]
        specs += [pl.BlockSpec((None, CTX_LEN, D_MODEL), lambda b, t: (b, 0, 0)), blk(-1), blk(0), blk(1)]
    else:
        args.append(xs)
        specs.append(tile(D_MODEL))
    aliases = {}
    if has_out:
        ya, yb, (w_out_stack, j_out), layer_out = out
        half = w_out_stack.shape[1] // 2
        args += [ya, yb, w_out_stack, mods, mods]
        specs += [tile(half), tile(half), layer_of(w_out_stack, j_out), mod_sample(layer_out), mod_ctx(layer_out)]
        if not first:
            aliases = {0: 0}
    w_in_stack, j_in = w_in
    args += [norm_g.reshape(1, D_MODEL), mods, mods, w_in_stack]
    specs += [const((1, D_MODEL)), mod_sample(layer_in), mod_ctx(layer_in), layer_of(w_in_stack, j_in)]
    if in_kind == "even":
        q_gain, k_gain, (cos, s_even, s_odd), bd = even_extra
        tab = pl.BlockSpec((PROJ_TILE, LANES), lambda b, t: (t, 0))
        args += [jnp.tile(q_gain, LANES // HEAD_DIM).reshape(1, LANES),
                 jnp.tile(k_gain, LANES // HEAD_DIM).reshape(1, LANES), cos, s_even, s_odd, bd]
        specs += [const((1, LANES)), const((1, LANES)), tab, tab, tab, const((LANES, LANES))]
    else:
        dw_w, dw_b, ln_g, ln_b = conv
        args += [dw_w, dw_b.reshape(1, C_WIDTH), ln_g.reshape(1, C_WIDTH), ln_b.reshape(1, C_WIDTH)]
        specs += [const((C_CONV, C_WIDTH)), const((1, C_WIDTH)), const((1, C_WIDTH)), const((1, C_WIDTH))]
    out_shapes, out_specs = [], []
    if has_out:
        out_shapes.append(jax.ShapeDtypeStruct((BATCH, T_ALL, D_MODEL), F32))
        out_specs.append(tile(D_MODEL))
    if in_kind == "even":
        out_shapes.append(jax.ShapeDtypeStruct((BATCH, T_ALL, EVEN_OUT), BF16))
        out_specs.append(tile(EVEN_OUT))
    else:
        out_shapes += [jax.ShapeDtypeStruct((BATCH, T_ALL, ODD_OUT), BF16),
                       jax.ShapeDtypeStruct((BATCH, T_ALL, C_WIDTH), F32),
                       jax.ShapeDtypeStruct((BATCH, T_ALL, C_WIDTH), BF16)]
        out_specs += [tile(ODD_OUT), tile(C_WIDTH), tile(C_WIDTH)]
    res = pl.pallas_call(
        _make_proj_kernel(first, has_out, in_kind),
        grid=(BATCH, PROJ_TILES),
        in_specs=specs,
        out_specs=out_specs,
        out_shape=out_shapes,
        input_output_aliases=aliases,
        compiler_params=_cparams(2, PROJ_VMEM_LIMIT),
    )(*args)
    res = list(res)
    stream = res.pop(0) if has_out else None
    return (stream, *res)


def _rope_tables():
    t = np.arange(SEQ)
    row = (t // GRID_W).astype(np.float64)
    col = (t % GRID_W).astype(np.float64)
    half = HEAD_DIM // 2
    freqs = (np.float32(ROPE_THETA) ** (-np.arange(0, half, 2, dtype=np.float32) / half)).astype(np.float64)
    ang = np.concatenate([row[:, None] * freqs, col[:, None] * freqs], axis=-1).astype(np.float32)
    cos = np.repeat(np.cos(ang.astype(np.float64)), 2, axis=-1)
    sin = np.repeat(np.sin(ang.astype(np.float64)), 2, axis=-1)
    lane_even = (np.arange(HEAD_DIM) % 2 == 0)
    s_even = np.where(lane_even, -sin, 0.0)
    s_odd = np.where(lane_even, 0.0, sin)

    def full(tab, ctx_val):
        tab = np.concatenate([np.full((CTX_LEN, HEAD_DIM), ctx_val), tab], axis=0).astype(np.float32)
        return jnp.asarray(np.tile(tab, (1, LANES // HEAD_DIM)))

    return full(cos, 1.0), full(s_even, 0.0), full(s_odd, 0.0)


def _softmax_pv(s_list, v_list, sink=None):
    m = s_list[0].max(axis=-1, keepdims=True)
    for s in s_list[1:]:
        m = jnp.maximum(m, s.max(axis=-1, keepdims=True))
    if sink is not None:
        m = jnp.maximum(m, sink)
    acc = None
    for s, v in zip(s_list, v_list):
        p = jnp.exp2(s - m).astype(BF16)
        part = jnp.dot(p, v, preferred_element_type=F32)
        acc = part if acc is None else acc + part
    extra = jnp.exp2(sink - m) if sink is not None else None
    return acc, extra


def _online_softmax_pv(s_list, v_list):
    m, acc = None, None
    for s, v in zip(s_list, v_list):
        m_new = s.max(axis=-1, keepdims=True)
        if m is not None:
            m_new = jnp.maximum(m, m_new)
        part = jnp.dot(jnp.exp2(s - m_new).astype(BF16), v, preferred_element_type=F32)
        acc = part if acc is None else acc * jnp.exp2(m - m_new) + part
        m = m_new
    return acc


def _pipelined(n_heads, scores_fn, finish_fn, depth=PIPELINE_DEPTH):
    ahead = [scores_fn(h) for h in range(min(depth, n_heads))]
    for h in range(n_heads):
        if h + depth < n_heads:
            ahead.append(scores_fn(h + depth))
        finish_fn(h, ahead.pop(0))


class _PairStore:
    def __init__(self, o_ref, g_ref):
        self.o_ref, self.g_ref, self.held = o_ref, g_ref, None

    def put(self, h, o):
        if h % 2 == 0:
            self.held = o
            return
        lanes = slice((h - 1) * HEAD_DIM, (h + 1) * HEAD_DIM)
        pair = jnp.concatenate([self.held, o], axis=1)
        self.o_ref[:, lanes] = (pair * self.g_ref[:, lanes].astype(F32)).astype(BF16)


def _gqa_normalise(acc, extra, kv):
    den = acc[:, LANES:2 * LANES]
    if extra is not None:
        den = den + extra
    o = acc[:, 0:LANES] / den
    return o[:, kv * HEAD_DIM:(kv + 1) * HEAD_DIM]


def _attn_a_kernel(q_ref, g_ref, k_ref, v_ref, o_ref):
    group = A_HEADS // A_KV_HEADS

    def run(n_keys):
        out = _PairStore(o_ref, g_ref)

        spans = [(lo, min(lo + A_KEY_CHUNK, n_keys)) for lo in range(0, n_keys, A_KEY_CHUNK)]

        def scores(h):
            kv = h // group
            q = q_ref[:, h * HEAD_DIM:(h + 1) * HEAD_DIM]
            return [lax.dot_general(q, k_ref[lo:hi, kv * HEAD_DIM:(kv + 1) * HEAD_DIM], NT_DIMS,
                                    preferred_element_type=F32) for lo, hi in spans]

        def finish(h, s_list):
            acc = _online_softmax_pv(s_list, [v_ref[lo:hi, :] for lo, hi in spans])
            out.put(h, _gqa_normalise(acc, None, h // group))

        _pipelined(A_HEADS, scores, finish, depth=CTX_PIPELINE_DEPTH if n_keys == CTX_LEN else PIPELINE_DEPTH)

    t = pl.program_id(1)
    pl.when(t == 0)(lambda: run(CTX_LEN))
    pl.when(t > 0)(lambda: run(T_ALL))


def _attn_b_kernel(sink_ref, q_ref, g_ref, k_ref, v_ref, o_ref):
    group = B_HEADS // B_KV_HEADS
    t = pl.program_id(1)

    def run(row_spans, band_bias):
        out = _PairStore(o_ref, g_ref)
        k_all = jnp.concatenate([k_ref[pl.ds(s, n), :] for s, n in row_spans], axis=0)
        v_all = jnp.concatenate([v_ref[pl.ds(s, n), :] for s, n in row_spans], axis=0)
        k_kv = [k_all[:, kv * HEAD_DIM:(kv + 1) * HEAD_DIM] for kv in range(B_KV_HEADS)]

        def scores(h):
            q = q_ref[:, h * HEAD_DIM:(h + 1) * HEAD_DIM]
            s = lax.dot_general(q, k_kv[h // group], NT_DIMS, preferred_element_type=F32)
            if band_bias is None:
                return s
            return jnp.concatenate([s[:, 0:CTX_LEN], s[:, CTX_LEN:] + band_bias], axis=1)

        def finish(h, s):
            kv = h // group
            sink = sink_ref[h] * LOG2E
            m = jnp.maximum(s.max(axis=-1, keepdims=True), sink)
            p = jnp.exp2(s - m).astype(BF16)
            acc = jnp.dot(p, v_all, preferred_element_type=F32)
            den = acc[:, LANES:2 * LANES] + jnp.exp2(sink - m)
            out.put(h, (acc[:, 0:LANES] / den)[:, kv * HEAD_DIM:(kv + 1) * HEAD_DIM])

        _pipelined(B_HEADS, scores, finish, depth=CTX_PIPELINE_DEPTH if band_bias is None else B_PIPELINE_DEPTH)

    def lat_tile():
        start = pl.multiple_of(jnp.clip(t * B_TILE - B_WINDOW, CTX_LEN, T_ALL - BAND_B), LANES)
        q_pos = t * B_TILE + lax.broadcasted_iota(jnp.int32, (B_TILE, BAND_B), 0)
        k_pos = start + lax.broadcasted_iota(jnp.int32, (B_TILE, BAND_B), 1)
        bias = jnp.where(jnp.abs(k_pos - q_pos) <= B_WINDOW, 0.0, NEG_INF).astype(F32)
        run([(0, CTX_LEN), (start, BAND_B)], bias)

    ctx_tiles = CTX_LEN // B_TILE
    pl.when(t < ctx_tiles)(lambda: run([(0, CTX_LEN)], None))
    pl.when(t >= ctx_tiles)(lat_tile)


def _attn_ab_kernel(sink_ref, qa_ref, ga_ref, ka_ref, va_ref, qb_ref, gb_ref, kb_ref, vb_ref, oa_ref, ob_ref):
    _attn_a_kernel(qa_ref, ga_ref, ka_ref, va_ref, oa_ref)
    _attn_b_kernel(sink_ref, qb_ref, gb_ref, kb_ref, vb_ref, ob_ref)


def _attn_ab(h_even, sink):
    assert B_TILE == ROW_TILE and A_HEADS == B_HEADS
    w = A_HEADS * HEAD_DIM
    tile = lambda col: pl.BlockSpec((None, ROW_TILE, w), lambda b, t: (b, t, col // w))
    keys = lambda col: pl.BlockSpec((None, T_ALL, LANES), lambda b, t: (b, 0, col // LANES))
    values = lambda col: pl.BlockSpec((None, T_ALL, 2 * LANES), lambda b, t: (b, 0, col // (2 * LANES)))
    out_spec = pl.BlockSpec((None, ROW_TILE, w), lambda b, t: (b, t, 0))
    out_shape = jax.ShapeDtypeStruct((BATCH, T_ALL, w), BF16)
    return pl.pallas_call(
        _attn_ab_kernel,
        grid=(BATCH, N_TILES),
        in_specs=[
            pl.BlockSpec(memory_space=pltpu.SMEM),
            tile(EV_AQ), tile(EV_AG), keys(EV_AK), values(EV_AV),
            tile(EV_BQ), tile(EV_BG), keys(EV_BK), values(EV_BV),
        ],
        out_specs=[out_spec, out_spec],
        out_shape=[out_shape, out_shape],
        compiler_params=_cparams(2),
    )(sink, *[h_even] * 8)


CONV_GROUPS = 8


def _conv_tail(acc, sg, lg_ref, lb_ref):
    mu = jnp.mean(acc, axis=-1, keepdims=True)
    ctr = acc - mu
    var = jnp.mean(ctr * ctr, axis=-1, keepdims=True)
    y = ctr * lax.rsqrt(var + EPS) * lg_ref[...] + lb_ref[...]
    return (_silu(y) * sg).astype(BF16)


def _conv_fixup_kernel(ul_ref, uc_ref, ur_ref, sg_ref, w_ref, b_ref, lg_ref, lb_ref, ya_ref, o_ref):
    del ya_ref
    n_win = 3 * HALO
    for b in range(BATCH):
        win = jnp.concatenate([ul_ref[b], uc_ref[b], ur_ref[b]], axis=0)
        acc = jnp.zeros((HALO, C_WIDTH), F32) + b_ref[...]
        for r in range(CONV_GROUPS):
            shifted = win if r == 0 else pltpu.roll(win, n_win - r, 0)
            for k in range(C_CONV):
                off = HALO - C_CONV // 2 + k
                if off % CONV_GROUPS == r:
                    base = off - r
                    acc = acc + shifted[base:base + HALO, :] * w_ref[k:k + 1, :]
        o_ref[b] = _conv_tail(acc, sg_ref[b].astype(F32), lg_ref, lb_ref)


def _conv_fixup(ya, u, h_odd, dw_w, dw_b, ln_g, ln_b):
    per_tile = PROJ_TILE // HALO
    edge_block = lambda k: (k // 2 + 1) * per_tile - 1 + k % 2
    blk = lambda off, col=0: pl.BlockSpec((BATCH, HALO, C_WIDTH), lambda k: (0, edge_block(k) + off, col))
    const = lambda shape: pl.BlockSpec(shape, lambda k: (0,) * len(shape))
    return pl.pallas_call(
        _conv_fixup_kernel,
        grid=(2 * (PROJ_TILES - 1),),
        in_specs=[
            blk(-1), blk(0), blk(1), blk(0, OD_CG // C_WIDTH),
            const((C_CONV, C_WIDTH)), const((1, C_WIDTH)), const((1, C_WIDTH)), const((1, C_WIDTH)),
            pl.BlockSpec(memory_space=pl.ANY),
        ],
        out_specs=blk(0),
        out_shape=jax.ShapeDtypeStruct((BATCH, T_ALL, C_WIDTH), BF16),
        input_output_aliases={8: 0},
        compiler_params=_cparams(1),
    )(u, u, u, h_odd, dw_w, dw_b.reshape(1, C_WIDTH), ln_g.reshape(1, C_WIDTH), ln_b.reshape(1, C_WIDTH), ya)


TILE_GRID_ROWS = ROW_TILE // GRID_W
BAND_GRID_ROWS = BAND_D // GRID_W
N_DR = 2 * NA_KH - 1
N_DC = 2 * NA_KW - 1
PAIR_ENTRIES = 17
PAIR_RIGHT_ONLY, PAIR_LEFT_ONLY = 15, 16


INTERIOR_FIRST_DR = NA_KH // 2 - 1
INTERIOR_LAST_DR = INTERIOR_FIRST_DR + NA_KH - 1


def _pair_halves(entry):
    if entry == 0:
        return None, None
    if entry == PAIR_RIGHT_ONLY:
        return None, INTERIOR_FIRST_DR
    if entry == PAIR_LEFT_ONLY:
        return INTERIOR_LAST_DR, None
    return entry - 1, entry


def _pair_table_kernel(ext_ref, o_ref):
    qc = lax.broadcasted_iota(jnp.int32, (GRID_W, LANES), 0)
    lane = lax.broadcasted_iota(jnp.int32, (GRID_W, LANES), 1)
    kc = jnp.bitwise_and(lane, GRID_W - 1)
    col_start = jnp.clip(qc - NA_KW // 2, 0, GRID_W - NA_KW)
    col_in = jnp.logical_and(kc >= col_start, kc < col_start + NA_KW)
    is_left = lane < GRID_W

    def half(dr, right):
        rows = jnp.broadcast_to(ext_ref[dr:dr + 1, :], (GRID_W, LANES))
        shift = 1 if right else LANES - GRID_W + 1
        return pltpu.roll(rows, shift, 1, stride=1, stride_axis=0)

    for entry in range(PAIR_ENTRIES):
        left_dr, right_dr = _pair_halves(entry)
        block = jnp.full((GRID_W, LANES), NEG_INF, F32)
        if left_dr is not None:
            block = jnp.where(jnp.logical_and(is_left, col_in), half(left_dr, False), block)
        if right_dr is not None:
            block = jnp.where(jnp.logical_and(jnp.logical_not(is_left), col_in), half(right_dr, True), block)
        o_ref[entry] = block


def _na_pair_table(rpb):
    lead = GRID_W - NA_KW
    ext = jnp.pad(rpb.astype(F32) * LOG2E, ((0, 0), (0, 0), (lead, LANES - N_DC - lead)), mode="edge")
    return pl.pallas_call(
        _pair_table_kernel,
        grid=(D_HEADS,),
        in_specs=[pl.BlockSpec((None, N_DR, LANES), lambda h: (h, 0, 0))],
        out_specs=pl.BlockSpec((None, PAIR_ENTRIES, GRID_W, LANES), lambda h: (h, 0, 0, 0)),
        out_shape=jax.ShapeDtypeStruct((D_HEADS, PAIR_ENTRIES, GRID_W, LANES), F32),
        compiler_params=_cparams(1),
    )(ext)


def _na_pair_entry(variant, a, p):
    if variant == "first":
        return 2 * p - a + NA_KH if p < NA_KH // 2 else 0
    if variant == "last":
        return 2 * p - a if p >= (BAND_GRID_ROWS - NA_KH) // 2 else 0
    left_ok = a <= 2 * p < a + NA_KH
    right_ok = a <= 2 * p + 1 < a + NA_KH
    if left_ok and right_ok:
        return 2 * p - a + NA_KH // 2
    if right_ok:
        return PAIR_RIGHT_ONLY
    if left_ok:
        return PAIR_LEFT_ONLY
    return 0


def _attn_d_kernel(first_tile, q_ref, g_ref, k_ref, v_ref, tab_ref, o_ref):
    t = pl.program_id(1) + first_tile

    def normalise(acc):
        return (acc / pltpu.roll(acc, HEAD_DIM, 1))[:, 0:HEAD_DIM]

    def ctx_tile():
        out = _PairStore(o_ref, g_ref)

        def scores(h):
            lanes = slice(h * HEAD_DIM, (h + 1) * HEAD_DIM)
            return [lax.dot_general(q_ref[:, lanes], k_ref[0:CTX_LEN, lanes], NT_DIMS, preferred_element_type=F32)]

        def finish(h, s_list):
            acc, _ = _softmax_pv(s_list, [v_ref[0:CTX_LEN, h * LANES:(h + 1) * LANES]])
            out.put(h, normalise(acc))

        _pipelined(D_HEADS, scores, finish, depth=CTX_PIPELINE_DEPTH)

    def lat_tile():
        out = _PairStore(o_ref, g_ref)
        start = pl.multiple_of(
            jnp.clip((t - 1) * ROW_TILE - (NA_KH // 2) * GRID_W, 0, SEQ - BAND_D) + CTX_LEN, LANES)
        is_first = t == 1
        is_last = t == N_TILES - 1

        def bias(h):
            rows = []
            for a in range(TILE_GRID_ROWS):
                blocks = []
                for p in range(BAND_GRID_ROWS // 2):
                    entry = jnp.where(is_first, _na_pair_entry("first", a, p),
                                      jnp.where(is_last, _na_pair_entry("last", a, p),
                                                _na_pair_entry("interior", a, p)))
                    blocks.append(tab_ref[h, entry])
                rows.append(jnp.concatenate(blocks, axis=1))
            return jnp.concatenate(rows, axis=0)

        def scores(h):
            lanes = slice(h * HEAD_DIM, (h + 1) * HEAD_DIM)
            q = q_ref[:, lanes]
            kb = k_ref[pl.ds(start, BAND_D), :][:, lanes]
            s_ctx = lax.dot_general(q, k_ref[0:CTX_LEN, lanes], NT_DIMS, preferred_element_type=F32)
            s_band = lax.dot_general(q, kb, NT_DIMS, preferred_element_type=F32) + bias(h)
            return [s_ctx, s_band]

        def finish(h, s_list):
            v_lanes = slice(h * LANES, (h + 1) * LANES)
            v_list = [v_ref[0:CTX_LEN, v_lanes], v_ref[pl.ds(start, BAND_D), :][:, v_lanes]]
            out.put(h, normalise(_online_softmax_pv(s_list, v_list)))

        _pipelined(D_HEADS, scores, finish)

    if first_tile == 0:
        pl.when(t == 0)(ctx_tile)
        pl.when(t > 0)(lat_tile)
    else:
        lat_tile()


def _attn_d(h_odd, pair_tab, lat_only):
    w = D_HEADS * HEAD_DIM
    f = 1 if lat_only else 0
    return pl.pallas_call(
        functools.partial(_attn_d_kernel, f),
        grid=(BATCH, N_TILES - f),
        in_specs=[
            pl.BlockSpec((None, ROW_TILE, w), lambda b, t: (b, t + f, OD_DQ // w)),
            pl.BlockSpec((None, ROW_TILE, w), lambda b, t: (b, t + f, OD_DG // w)),
            pl.BlockSpec((None, T_ALL, w), lambda b, t: (b, 0, OD_DK // w)),
            pl.BlockSpec((None, T_ALL, 2 * w), lambda b, t: (b, 0, OD_DV // (2 * w))),
            pl.BlockSpec((D_HEADS, PAIR_ENTRIES, GRID_W, LANES), lambda b, t: (0, 0, 0, 0)),
        ],
        out_specs=pl.BlockSpec((None, ROW_TILE, w), lambda b, t: (b, t, 0)),
        out_shape=jax.ShapeDtypeStruct((BATCH, T_ALL - f * CTX_LEN, w), BF16),
        compiler_params=_cparams(2),
    )(h_odd, h_odd, h_odd, h_odd, pair_tab)


FINAL_TILE = 4 * ROW_TILE
FINAL_BLOCKS = FINAL_TILE // ROW_TILE


def _outproj_final_kernel(*refs):
    ya_refs, refs = refs[:FINAL_BLOCKS], refs[FINAL_BLOCKS:]
    yb_ref, w_ref = refs[:2]
    x_refs, (mod_ref, fg_ref, o_ref) = refs[2:2 + FINAL_BLOCKS], refs[2 + FINAL_BLOCKS:]
    half = w_ref.shape[0] // 2
    ya = jnp.concatenate([r[...] for r in ya_refs], axis=0)
    y = jnp.dot(ya, w_ref[0:half, :], preferred_element_type=F32)
    y = y + jnp.dot(yb_ref[...], w_ref[half:, :], preferred_element_type=F32)
    x = jnp.concatenate([r[...] for r in x_refs], axis=0) + mod_ref[:, 2 * D_MODEL:3 * D_MODEL] * y
    ms = jnp.mean(x * x, axis=-1, keepdims=True)
    o_ref[...] = x * lax.rsqrt(ms + EPS) * fg_ref[...]


def _outproj_final(ya, yb, w_out, xs, mods, layer, final_g):
    w_out_stack, j_out = w_out
    half = w_out_stack.shape[1] // 2
    blks = lambda width: [pl.BlockSpec((None, ROW_TILE, width),
                                       lambda b, t, off=off: (b, FINAL_BLOCKS * t + 1 + off, 0))
                          for off in range(FINAL_BLOCKS)]
    return pl.pallas_call(
        _outproj_final_kernel,
        grid=(BATCH, SEQ // FINAL_TILE),
        in_specs=[
            *blks(half),
            pl.BlockSpec((None, FINAL_TILE, half), lambda b, t: (b, t, 0)),
            pl.BlockSpec((None, 2 * half, D_MODEL), lambda b, t: (j_out, 0, 0)),
            *blks(D_MODEL),
            pl.BlockSpec((None, 1, 3 * D_MODEL), lambda b, t: (layer * MOD_ROWS + b, 0, 0)),
            pl.BlockSpec((1, D_MODEL), lambda b, t: (0, 0)),
        ],
        out_specs=pl.BlockSpec((None, FINAL_TILE, D_MODEL), lambda b, t: (b, t, 0)),
        out_shape=jax.ShapeDtypeStruct((BATCH, SEQ, D_MODEL), F32),
        compiler_params=_cparams(2),
    )(*[ya] * FINAL_BLOCKS, yb, w_out_stack, *[xs] * FINAL_BLOCKS, mods, final_g.reshape(1, D_MODEL))


def kernel(x, c, ctx, c_ctx, mod_w, mod_b, norm_g, ev_w_in, ev_w_out, a_q_gain, a_k_gain, b_sink,
           od_w_in, od_w_out, c_dw_w, c_dw_b, c_ln_g, c_ln_b, d_rpb, final_g):
    cond = jnp.concatenate(
        [c, c_ctx[None, :], jnp.zeros((MOD_ROWS - BATCH - 1, D_MODEL), F32)], axis=0)
    mods = _modulation_all(cond, mod_w, mod_b).reshape(DEPTH * MOD_ROWS, 1, 3 * D_MODEL)
    tables = _rope_tables()
    head_of_lane = np.arange(LANES) // HEAD_DIM
    bd = jnp.asarray((head_of_lane[:, None] == head_of_lane[None, :]).astype(np.float32) / HEAD_DIM, BF16)
    ev_w_in_b, ev_w_out_b, od_w_in_b, od_w_out_b = (w.astype(BF16) for w in (ev_w_in, ev_w_out, od_w_in, od_w_out))

    xs = None
    pending = None
    for i in range(DEPTH):
        j = i // 2
        src = dict(x=x, ctx=ctx) if xs is None else dict(xs=xs)
        if i % 2 == 0:
            extra = (a_q_gain[j], a_k_gain[j], tables, bd)
            stream, h_even = _proj("even", i, norm_g[i], (ev_w_in_b, j), mods, out=pending, even_extra=extra, **src)
            pending = (*_attn_ab(h_even, b_sink[j]), (ev_w_out_b, j), i)
        else:
            conv = (c_dw_w[j], c_dw_b[j], c_ln_g[j], c_ln_b[j])
            stream, h_odd, u, ya = _proj("odd", i, norm_g[i], (od_w_in_b, j), mods, out=pending, conv=conv, **src)
            ya = _conv_fixup(ya, u, h_odd, *conv)
            last = i == DEPTH - 1
            pending = (ya, _attn_d(h_odd, _na_pair_table(d_rpb[j]), lat_only=last), (od_w_out_b, j), i)
        if stream is not None:
            xs = stream
    ya, yb, w_out, layer = pending
    return _outproj_final(ya, yb, w_out, xs, mods, layer, final_g)
```
